```python
import jax, jax.numpy as jnp
from jax import lax
import numpy as np

D_MODEL = 1024
BATCH = 2
SEQ = 8192
DEPTH = 2

CTX_LEN = 256
GRID_W = 64

SGU_HEADS = 4
SGU_HEAD_DIM = 64
SGU_W = SGU_HEADS * SGU_HEAD_DIM
CHUNK = 128
FNET_GROUPS = 4
FNET_GROUP_DIM = 64
FNET_W = FNET_GROUPS * FNET_GROUP_DIM
MLA_HEADS = 4
QK_NOPE_DIM = 128
QK_ROPE_DIM = 64
QK_DIM = QK_NOPE_DIM + QK_ROPE_DIM
V_DIM = 128
Q_LORA = 256
KV_LORA = 128
MLA_W = MLA_HEADS * V_DIM
Q_BLOCK = 128
ROPE_THETA = 10000.0
OFF_U = 0
OFF_V = OFF_U + SGU_W
OFF_F = OFF_V + SGU_W
OFF_Q = OFF_F + FNET_W
OFF_KV = OFF_Q + Q_LORA
OFF_KR = OFF_KV + KV_LORA
IN_W = OFF_KR + QK_ROPE_DIM
MIX_W = SGU_W + FNET_W + MLA_W
N_EXPERTS = 16
EC_FACTOR = 2
D_EXPERT = 512
EPS = 1e-6

kernel_name = "hybrid_dit_sgu_fnet_mla_ecmoe"


def rms_norm(x):
    xf = x.astype(jnp.float32)
    return (xf * lax.rsqrt(jnp.mean(xf * xf, axis=-1, keepdims=True) + EPS)).astype(x.dtype)


def ada_modulation(cond, w_ada, b_ada):
    m = (jax.nn.silu(cond) @ w_ada + b_ada)[:, None, :]
    return jnp.split(m, 6, axis=-1)


def modulate(x, shift, scale):
    return rms_norm(x) * (1 + scale) + shift


def rope_1d(x, pos):
    half = x.shape[-1] // 2
    freqs = ROPE_THETA ** (-jnp.arange(half, dtype=jnp.float32) / half)
    ang = pos.astype(jnp.float32)[:, None] * freqs
    cos = jnp.cos(ang)[:, None, :]
    sin = jnp.sin(ang)[:, None, :]
    xf = x.astype(jnp.float32)
    x1, x2 = xf[..., :half], xf[..., half:]
    return jnp.concatenate([x1 * cos - x2 * sin, x1 * sin + x2 * cos], axis=-1).astype(x.dtype)


def rope_2d(x, pos_row, pos_col):
    half = x.shape[-1] // 2
    return jnp.concatenate([rope_1d(x[..., :half], pos_row), rope_1d(x[..., half:], pos_col)], axis=-1)


def with_rope(t, pos_row, pos_col):
    return jnp.concatenate([t[..., :QK_NOPE_DIM], rope_2d(t[..., QK_NOPE_DIM:], pos_row, pos_col)], axis=-1)


def chunk_mlp(pu, pv, sgu_norm, w_sgu, b_sgu):
    b, n, _ = pu.shape
    u = jax.nn.gelu(pu)
    v = rms_norm(jax.nn.gelu(pv)) * sgu_norm
    v = v.reshape(b, n // CHUNK, CHUNK, SGU_HEADS, SGU_HEAD_DIM)
    z = jnp.einsum('hpq,bcqhd->bcphd', w_sgu, v) + b_sgu.T[:, :, None]
    return u * z.reshape(b, n, SGU_W)


def fourier_mix(pf):
    b, n, _ = pf.shape
    f = pf.reshape(b, n, FNET_GROUPS, FNET_GROUP_DIM).astype(jnp.float32)
    y = jnp.fft.fft2(f, axes=(1, 3), norm="ortho").real
    return y.reshape(b, n, FNET_W).astype(pf.dtype)


def mla_queries(pq, q_lora_norm, w_uq, q_norm):
    b, n, _ = pq.shape
    cq = rms_norm(pq) * q_lora_norm
    q = (cq @ w_uq).reshape(b, n, MLA_HEADS, QK_DIM)
    return rms_norm(q) * q_norm


def mla_keys_values(pkv, pkr, kv_lora_norm, w_ukv, k_norm):
    b, n, _ = pkv.shape
    ckv = rms_norm(pkv) * kv_lora_norm
    kv = (ckv @ w_ukv).reshape(b, n, MLA_HEADS, QK_NOPE_DIM + V_DIM)
    k_nope, v = kv[..., :QK_NOPE_DIM], kv[..., QK_NOPE_DIM:]
    k_rope = jnp.broadcast_to(pkr[:, :, None, :], (b, n, MLA_HEADS, QK_ROPE_DIM))
    k = rms_norm(jnp.concatenate([k_nope, k_rope], axis=-1)) * k_norm
    return k, v


def block_attention(q, k, v):
    b, n, h, _ = q.shape
    scale = QK_DIM ** -0.5
    qb = jnp.moveaxis(q.reshape(b, n // Q_BLOCK, Q_BLOCK, h, QK_DIM), 1, 0)

    def one_block(qblk):
        s = jnp.einsum('bqhd,bkhd->bhqk', qblk, k).astype(jnp.float32) * scale
        p = jax.nn.softmax(s, axis=-1).astype(v.dtype)
        return jnp.einsum('bhqk,bkhv->bqhv', p, v)

    o = lax.map(one_block, qb)
    return jnp.moveaxis(o, 0, 1).reshape(b, n, h * V_DIM)


def local_head_groups(p, sgu_norm, w_sgu, b_sgu, q_lora_norm, w_uq, q_norm):
    ya = chunk_mlp(p[..., OFF_U:OFF_V], p[..., OFF_V:OFF_F], sgu_norm, w_sgu, b_sgu)
    yb = fourier_mix(p[..., OFF_F:OFF_Q])
    q = mla_queries(p[..., OFF_Q:OFF_KV], q_lora_norm, w_uq, q_norm)
    return ya, yb, q


def expert_choice_moe(h, w_router, w_gate, w_up, w_down):
    b, n, d = h.shape
    cap = EC_FACTOR * n // N_EXPERTS
    aff = jax.nn.softmax(jnp.einsum('bnd,de->bne', h, w_router).astype(jnp.float32), axis=-1)
    g, idx = lax.top_k(jnp.swapaxes(aff, 1, 2), cap)
    xs = jax.vmap(lambda hb, ib: hb[ib])(h, idx)
    hid = jax.nn.silu(jnp.einsum('becd,edf->becf', xs, w_gate)) * jnp.einsum('becd,edf->becf', xs, w_up)
    y = jnp.einsum('becf,efd->becd', hid, w_down) * g[..., None].astype(h.dtype)
    return jax.vmap(lambda yb, ib: jnp.zeros((n, d), h.dtype).at[ib.reshape(-1)].add(yb.reshape(-1, d)))(y, idx)


def setup_inputs(seed: int = 0) -> dict:
    key = jax.random.key(seed)
    ks = jax.random.split(key, 22)

    def nrm(k, shape, scale):
        return jax.random.normal(k, shape, jnp.float32) * scale

    return {
        "x": nrm(ks[0], (BATCH, SEQ, D_MODEL), 1.0),
        "c": nrm(ks[1], (BATCH, D_MODEL), 1.0),
        "ctx": nrm(ks[2], (BATCH, CTX_LEN, D_MODEL), 1.0),
        "c_ctx": nrm(ks[3], (D_MODEL,), 1.0),
        "w_ada": nrm(ks[4], (DEPTH, D_MODEL, 6 * D_MODEL), 0.5 * D_MODEL ** -0.5),
        "b_ada": nrm(ks[5], (DEPTH, 6 * D_MODEL), 0.01),
        "w_in": nrm(ks[6], (DEPTH, D_MODEL, IN_W), D_MODEL ** -0.5),
        "sgu_norm": 1.0 + nrm(ks[7], (DEPTH, SGU_W), 0.01),
        "w_sgu": nrm(ks[8], (DEPTH, SGU_HEADS, CHUNK, CHUNK), CHUNK ** -0.5),
        "b_sgu": 1.0 + nrm(ks[9], (DEPTH, SGU_HEADS, CHUNK), 0.01),
        "q_lora_norm": 1.0 + nrm(ks[10], (DEPTH, Q_LORA), 0.01),
        "w_uq": nrm(ks[11], (DEPTH, Q_LORA, MLA_HEADS * QK_DIM), Q_LORA ** -0.5),
        "kv_lora_norm": 1.0 + nrm(ks[12], (DEPTH, KV_LORA), 0.01),
        "w_ukv": nrm(ks[13], (DEPTH, KV_LORA, MLA_HEADS * (QK_NOPE_DIM + V_DIM)), KV_LORA ** -0.5),
        "q_norm": 1.0 + nrm(ks[14], (DEPTH, QK_DIM), 0.01),
        "k_norm": 1.0 + nrm(ks[15], (DEPTH, QK_DIM), 0.01),
        "w_out": nrm(ks[16], (DEPTH, MIX_W, D_MODEL), MIX_W ** -0.5),
        "w_router": nrm(ks[17], (DEPTH, D_MODEL, N_EXPERTS), D_MODEL ** -0.5),
        "w_gate": nrm(ks[18], (DEPTH, N_EXPERTS, D_MODEL, D_EXPERT), D_MODEL ** -0.5),
        "w_up": nrm(ks[19], (DEPTH, N_EXPERTS, D_MODEL, D_EXPERT), D_MODEL ** -0.5),
        "w_down": nrm(ks[20], (DEPTH, N_EXPERTS, D_EXPERT, D_MODEL), D_EXPERT ** -0.5),
    }


def reference(x, c, ctx, c_ctx, w_ada, b_ada, w_in, sgu_norm, w_sgu, b_sgu, q_lora_norm, w_uq,
              kv_lora_norm, w_ukv, q_norm, k_norm, w_out, w_router, w_gate, w_up, w_down):
    n = x.shape[1]
    rows = n // GRID_W
    pos_row = jnp.repeat(jnp.arange(rows, dtype=jnp.int32), GRID_W)
    pos_col = jnp.tile(jnp.arange(GRID_W, dtype=jnp.int32), rows)
    xc = ctx
    for l in range(DEPTH):
        last = l == DEPTH - 1
        sh1, sc1, g1, sh2, sc2, g2 = ada_modulation(c, w_ada[l], b_ada[l])
        csh1, csc1, cg1, csh2, csc2, cg2 = ada_modulation(c_ctx[None], w_ada[l], b_ada[l])

        hx = modulate(x, sh1, sc1)
        hc = modulate(xc, csh1, csc1)
        col0 = OFF_KV if last else 0
        pc = hc @ w_in[l][:, col0:]
        k_ctx, v_ctx = mla_keys_values(pc[..., OFF_KV - col0:OFF_KR - col0], pc[..., OFF_KR - col0:],
                                       kv_lora_norm[l], w_ukv[l], k_norm[l])

        px = hx @ w_in[l]
        ya, yb, q = local_head_groups(px, sgu_norm[l], w_sgu[l], b_sgu[l], q_lora_norm[l], w_uq[l], q_norm[l])
        k, v = mla_keys_values(px[..., OFF_KV:OFF_KR], px[..., OFF_KR:], kv_lora_norm[l], w_ukv[l], k_norm[l])
        q = with_rope(q, pos_row, pos_col)
        k = with_rope(k, pos_row, pos_col)
        yc = block_attention(q, jnp.concatenate([k_ctx, k], axis=1), jnp.concatenate([v_ctx, v], axis=1))
        x = x + g1 * (jnp.concatenate([ya, yb, yc], axis=-1) @ w_out[l])

        if not last:
            ya_c, yb_c, q_c = local_head_groups(pc, sgu_norm[l], w_sgu[l], b_sgu[l], q_lora_norm[l], w_uq[l], q_norm[l])
            yc_c = block_attention(q_c, k_ctx, v_ctx)
            xc = xc + cg1 * (jnp.concatenate([ya_c, yb_c, yc_c], axis=-1) @ w_out[l])

        x = x + g2 * expert_choice_moe(modulate(x, sh2, sc2), w_router[l], w_gate[l], w_up[l], w_down[l])
        if not last:
            xc = xc + cg2 * expert_choice_moe(modulate(xc, csh2, csc2), w_router[l], w_gate[l], w_up[l], w_down[l])
    return x
```

```python
import functools
import math

import numpy as np
import jax
import jax.numpy as jnp
from jax import lax
from jax.experimental import pallas as pl
from jax.experimental.pallas import tpu as pltpu

F32 = jnp.float32
BF16 = jnp.bfloat16

GRID_WIDTH = 64
SGU_HEADS = 4
SGU_HEAD_DIM = 64
SGU_W = SGU_HEADS * SGU_HEAD_DIM
CHUNK = 128
FNET_GROUP_DIM = 64
FNET_W = 256
MLA_HEADS = 4
QK_NOPE = 128
QK_ROPE = 64
QK_DIM = QK_NOPE + QK_ROPE
V_DIM = 128
Q_LORA = 256
KV_LORA = 128
OFF_V = 256
OFF_F = 512
OFF_Q = 768
OFF_KV = 1024
OFF_KR = 1152
N_EXPERTS = 16
EC_FACTOR = 2
ROPE_THETA = 10000.0
EPS = 1e-6

LANES = 128
SUBLANES = 8
VMEM_LIMIT_BYTES = 56 * 1024 * 1024

QK_PAD = 2 * LANES
ATTN_TQ = 512
ATTN_TK = 512
LOG2_E = 1.4426950408889634


def _cparams(sem, vmem=None):
    return pltpu.CompilerParams(dimension_semantics=sem, vmem_limit_bytes=vmem or VMEM_LIMIT_BYTES)


def _rms_scale(v, width):
    return lax.rsqrt(jnp.sum(v * v, axis=-1, keepdims=True) * (1.0 / width) + EPS)


def _gelu_tanh(v):
    return 0.5 * v * (1.0 + jnp.tanh(0.7978845608028654 * (v + 0.044715 * v * v * v)))


def _silu(v):
    return v / (1.0 + jnp.exp(-v))


def _ada_kernel(cond_ref, w_ref, b_ref, o_ref):
    s = _silu(cond_ref[...])
    o_ref[...] = jnp.dot(s, w_ref[...], preferred_element_type=F32,
                         precision=lax.Precision.HIGHEST) + b_ref[...]


def _ada_modulation(cond, w_ada, b_ada):
    depth, d, six_d = w_ada.shape
    tn = 1536
    return pl.pallas_call(
        _ada_kernel,
        out_shape=jax.ShapeDtypeStruct((depth, SUBLANES, six_d), F32),
        grid=(depth, six_d // tn),
        in_specs=[
            pl.BlockSpec((SUBLANES, d), lambda l, j: (0, 0)),
            pl.BlockSpec((None, d, tn), lambda l, j: (l, 0, j)),
            pl.BlockSpec((None, 1, tn), lambda l, j: (l, 0, j)),
        ],
        out_specs=pl.BlockSpec((None, SUBLANES, tn), lambda l, j: (l, 0, j)),
        compiler_params=_cparams(("arbitrary", "arbitrary")),
        name="ada_modulation",
    )(cond, w_ada, b_ada.reshape(depth, 1, six_d))


def _swap16(v):
    lane = lax.broadcasted_iota(jnp.int32, v.shape, 1)
    up = pltpu.roll(v, LANES - 16, axis=1)
    dn = pltpu.roll(v, 16, axis=1)
    return jnp.where((lane % 32) < 16, up, dn)


def _mix_in_kernel(x_ref, sh_ref, sc_ref, win_ref, sgun_ref, wsgu_ref, bsgu_ref, fdft_ref,
                   qln_ref, wuq_ref, qn_ref, kvn_ref, wukv_ref, kn_ref, cos_ref, sin_ref,
                   ya_ref, a_ref, b_ref, q_ref, k_ref, v_ref, *, tm, use_rope):
    x = x_ref[...]
    d = x.shape[-1]
    h = x * _rms_scale(x, d) * (1.0 + sc_ref[...]) + sh_ref[...]
    p = jnp.dot(h.astype(BF16), win_ref[...], preferred_element_type=F32)

    u = _gelu_tanh(p[:, 0:OFF_V])
    gv = _gelu_tanh(p[:, OFF_V:OFF_F])
    vn = gv * _rms_scale(gv, SGU_W) * sgun_ref[...]
    head_of_lane = lax.broadcasted_iota(jnp.int32, (CHUNK, SGU_W), 1) // SGU_HEAD_DIM
    for c in range(tm // CHUNK):
        rows = slice(c * CHUNK, (c + 1) * CHUNK)
        vc = vn[rows]
        vstack = jnp.concatenate(
            [jnp.where(head_of_lane == hh, vc, 0.0) for hh in range(SGU_HEADS)], axis=0).astype(BF16)
        z = jnp.dot(wsgu_ref[...], vstack, preferred_element_type=F32) + bsgu_ref[...]
        ya_ref[rows, :] = (u[rows] * z).astype(ya_ref.dtype)

    ab = jnp.dot(p[:, OFF_F:OFF_Q].astype(BF16), fdft_ref[...], preferred_element_type=F32)
    a_ref[...] = ab[:, 0:FNET_W].astype(a_ref.dtype)
    b_ref[...] = ab[:, FNET_W:2 * FNET_W].astype(b_ref.dtype)

    lane = lax.broadcasted_iota(jnp.int32, (1, LANES), 1)
    lo = lane < QK_ROPE
    if use_rope:
        cos = cos_ref[...]
        sin = sin_ref[...]

    def rope(t):
        return t * cos + _swap16(t) * sin if use_rope else t

    pq = p[:, OFF_Q:OFF_KV]
    cq = pq * _rms_scale(pq, Q_LORA) * qln_ref[...]
    q = jnp.dot(cq.astype(BF16), wuq_ref[...], preferred_element_type=F32)
    qnorm = qn_ref[...]
    for hh in range(MLA_HEADS):
        qn = q[:, LANES * hh:LANES * (hh + 1)]
        pair = hh // 2
        qr = q[:, MLA_HEADS * LANES + LANES * pair:MLA_HEADS * LANES + LANES * (pair + 1)]
        qr = jnp.where(lo if hh % 2 == 0 else jnp.logical_not(lo), qr, 0.0)
        ssq = jnp.sum(qn * qn, axis=-1, keepdims=True) + jnp.sum(qr * qr, axis=-1, keepdims=True)
        rinv = lax.rsqrt(ssq * (1.0 / QK_DIM) + EPS) * (QK_DIM ** -0.5 * LOG2_E)
        q_ref[hh, :, 0:LANES] = (qn * rinv * qnorm[:, 0:LANES]).astype(q_ref.dtype)
        q_ref[hh, :, LANES:QK_PAD] = rope(qr * rinv * qnorm[:, LANES:QK_PAD]).astype(q_ref.dtype)

    pkv = p[:, OFF_KV:OFF_KR]
    ckv = pkv * _rms_scale(pkv, KV_LORA) * kvn_ref[...]
    kv = jnp.dot(ckv.astype(BF16), wukv_ref[...], preferred_element_type=F32)
    krr = p[:, OFF_KR:OFF_KR + LANES]
    ssq_kr = 0.5 * jnp.sum(krr * krr, axis=-1, keepdims=True)
    knorm = kn_ref[...]
    rinvs = []
    for hh in range(MLA_HEADS):
        kn = kv[:, 2 * LANES * hh:2 * LANES * hh + LANES]
        ssq = jnp.sum(kn * kn, axis=-1, keepdims=True) + ssq_kr
        rinv = lax.rsqrt(ssq * (1.0 / QK_DIM) + EPS)
        rinvs.append(rinv)
        k_ref[hh, :, 0:LANES] = (kn * rinv * knorm[:, 0:LANES]).astype(k_ref.dtype)
        v_ref[hh] = kv[:, 2 * LANES * hh + LANES:2 * LANES * (hh + 1)].astype(v_ref.dtype)
    for pair in range(MLA_HEADS // 2):
        r2 = jnp.where(lo, rinvs[2 * pair], rinvs[2 * pair + 1])
        kr = rope(krr * r2 * knorm[:, LANES:QK_PAD]).astype(k_ref.dtype)
        k_ref[2 * pair, :, LANES:QK_PAD] = kr
        k_ref[2 * pair + 1, :, LANES:QK_PAD] = kr


def _mix_in(x, sh, sc, lw, rope_tabs, tm):
    bsz, nt, d = x.shape
    use_rope = rope_tabs is not None
    if use_rope:
        cos_t, sin_t = rope_tabs
    else:
        cos_t = sin_t = jnp.zeros((nt, LANES), F32)
    full = lambda arr: pl.BlockSpec(arr.shape, lambda b, i: (0,) * arr.ndim)
    vec = pl.BlockSpec((None, 1, d), lambda b, i: (b, 0, 0))
    in_w = lw["w_in"].shape[1]
    kern = functools.partial(_mix_in_kernel, tm=tm, use_rope=use_rope)
    return pl.pallas_call(
        kern,
        out_shape=(
            jax.ShapeDtypeStruct((bsz, nt, SGU_W), BF16),
            jax.ShapeDtypeStruct((bsz, nt, FNET_W), F32),
            jax.ShapeDtypeStruct((bsz, nt, FNET_W), F32),
            jax.ShapeDtypeStruct((bsz, MLA_HEADS, nt, QK_PAD), BF16),
            jax.ShapeDtypeStruct((bsz, MLA_HEADS, nt, QK_PAD), BF16),
            jax.ShapeDtypeStruct((bsz, MLA_HEADS, nt, V_DIM), BF16),
        ),
        grid=(bsz, nt // tm),
        in_specs=[
            pl.BlockSpec((None, tm, d), lambda b, i: (b, i, 0)),
            vec, vec,
            full(lw["w_in"]), full(lw["sgu_norm"]), full(lw["w_sgu"]), full(lw["b_sgu"]), full(lw["fdft"]),
            full(lw["q_lora_norm"]), full(lw["w_uq"]), full(lw["q_norm"]),
            full(lw["kv_lora_norm"]), full(lw["w_ukv"]), full(lw["k_norm"]),
            pl.BlockSpec((tm, LANES), lambda b, i: (i, 0)),
            pl.BlockSpec((tm, LANES), lambda b, i: (i, 0)),
        ],
        out_specs=(
            pl.BlockSpec((None, tm, SGU_W), lambda b, i: (b, i, 0)),
            pl.BlockSpec((None, tm, FNET_W), lambda b, i: (b, i, 0)),
            pl.BlockSpec((None, tm, FNET_W), lambda b, i: (b, i, 0)),
            pl.BlockSpec((None, MLA_HEADS, tm, QK_PAD), lambda b, i: (b, 0, i, 0)),
            pl.BlockSpec((None, MLA_HEADS, tm, QK_PAD), lambda b, i: (b, 0, i, 0)),
            pl.BlockSpec((None, MLA_HEADS, tm, V_DIM), lambda b, i: (b, 0, i, 0)),
        ),
        compiler_params=_cparams(("parallel", "parallel")),
        name="mix_in",
    )(x, sh, sc, lw["w_in"], lw["sgu_norm"], lw["w_sgu"], lw["b_sgu"], lw["fdft"],
      lw["q_lora_norm"], lw["w_uq"], lw["q_norm"], lw["kv_lora_norm"], lw["w_ukv"], lw["k_norm"],
      cos_t, sin_t)


def _dft_kernel(a_ref, b_ref, w_ref, m_ref, o_ref, yr_ref, yi_ref, *, n1, n2):
    w1 = w_ref[...]

    def step1(i2, _):
        rows = pl.ds(i2, n1, stride=n2)
        ab = jnp.concatenate([a_ref[rows, :], b_ref[rows, :]], axis=0).astype(BF16)
        r = jnp.dot(w1, ab, preferred_element_type=F32)
        yr_ref[rows, :] = r[0:n1]
        yi_ref[rows, :] = r[n1:2 * n1]
        return 0

    lax.fori_loop(0, n2, step1, 0)

    def step2(k1, _):
        rows = pl.ds(pl.multiple_of(k1 * n2, n2), n2)
        ycat = jnp.concatenate([yr_ref[rows, :], yi_ref[rows, :]], axis=0).astype(BF16)
        o_ref[pl.ds(k1, n2, stride=n1), :] = jnp.dot(m_ref[k1], ycat, preferred_element_type=F32)
        return 0

    lax.fori_loop(0, n1, step2, 0)


def _dft_factors(n):
    n1 = 64 if n >= 1024 else 32
    return n1, n // n1


@functools.lru_cache(maxsize=None)
def _dft_tables(n):
    n1, n2 = _dft_factors(n)
    j = np.arange(n1, dtype=np.float64)
    ang1 = 2.0 * np.pi * np.outer(j, j) / n1
    c1 = np.cos(ang1) / math.sqrt(n1)
    s1 = np.sin(ang1) / math.sqrt(n1)
    w1 = np.block([[c1, -s1], [s1, c1]])
    k1 = np.arange(n1, dtype=np.float64)[:, None, None]
    k2 = np.arange(n2, dtype=np.float64)[None, :, None]
    m2 = np.arange(n2, dtype=np.float64)[None, None, :]
    ang2 = 2.0 * np.pi * m2 * (k1 + n1 * k2) / n
    m2cat = np.concatenate([np.cos(ang2), -np.sin(ang2)], axis=-1) / math.sqrt(n2)
    return w1.astype(np.float32), m2cat.astype(np.float32)


@functools.lru_cache(maxsize=None)
def _channel_dft_table():
    j = np.arange(FNET_GROUP_DIM, dtype=np.float64)
    ang = 2.0 * np.pi * np.outer(j, j) / FNET_GROUP_DIM
    groups = FNET_W // FNET_GROUP_DIM
    cg = np.kron(np.eye(groups), np.cos(ang)) / math.sqrt(FNET_GROUP_DIM)
    sg = np.kron(np.eye(groups), np.sin(ang)) / math.sqrt(FNET_GROUP_DIM)
    return np.concatenate([cg, sg], axis=1).astype(np.float32)


def _fourier_positions(a, b):
    bsz, n, c = a.shape
    n1, n2 = _dft_factors(n)
    w1_np, m2_np = _dft_tables(n)
    w1 = jnp.asarray(w1_np).astype(BF16)
    m2cat = jnp.asarray(m2_np).astype(BF16)
    cb = LANES
    blk = pl.BlockSpec((None, n, cb), lambda bi, j: (bi, 0, j))
    return pl.pallas_call(
        functools.partial(_dft_kernel, n1=n1, n2=n2),
        out_shape=jax.ShapeDtypeStruct((bsz, n, c), F32),
        grid=(bsz, c // cb),
        in_specs=[blk, blk,
                  pl.BlockSpec((2 * n1, 2 * n1), lambda bi, j: (0, 0)),
                  pl.BlockSpec((n1, n2, 2 * n2), lambda bi, j: (0, 0, 0))],
        out_specs=blk,
        scratch_shapes=[pltpu.VMEM((n, cb), F32)] * 2,
        compiler_params=_cparams(("parallel", "parallel")),
        name="fourier",
    )(a, b, w1, m2cat)


NEG_BIG = -1e30


def _attn_kernel(*refs, tk, n_lat_blocks):
    if n_lat_blocks:
        qt_ref, kc_ref, vtc_ref, kl_ref, vtl_ref, o_ref, acc_ref, sa_ref, sb_ref = refs
    else:
        qt_ref, kc_ref, vtc_ref, o_ref, acc_ref = refs
    qt = qt_ref[...]
    tq = qt.shape[1]

    def scores(kj):
        return jnp.dot(kj, qt, preferred_element_type=F32)

    def consume(s, vtj, m, l, first):
        m_new = jnp.maximum(m, jnp.max(s, axis=0, keepdims=True))
        alpha = jnp.exp2(m - m_new)
        p = jnp.exp2(s - m_new)
        l_new = alpha * l + jnp.sum(p, axis=0, keepdims=True)
        pv = jnp.dot(vtj, p.astype(BF16), preferred_element_type=F32)
        acc_ref[...] = pv if first else alpha * acc_ref[...] + pv
        return m_new, l_new

    def lat_keys(j):
        return kl_ref[pl.ds(pl.multiple_of(j * tk, tk), tk), :]

    m0 = jnp.full((1, tq), NEG_BIG, F32)
    l0 = jnp.zeros((1, tq), F32)
    if n_lat_blocks:
        sa_ref[...] = scores(lat_keys(0))
    m, l = consume(scores(kc_ref[...]), vtc_ref[...], m0, l0, True)
    if n_lat_blocks:
        def body(i, carry):
            sb_ref[...] = scores(lat_keys(2 * i + 1))
            m, l = consume(sa_ref[...], vtl_ref[2 * i], carry[0], carry[1], False)
            sa_ref[...] = scores(lat_keys(2 * i + 2))
            return consume(sb_ref[...], vtl_ref[2 * i + 1], m, l, False)

        m, l = lax.fori_loop(0, n_lat_blocks // 2 - 1, body, (m, l))
        sb_ref[...] = scores(lat_keys(n_lat_blocks - 1))
        m, l = consume(sa_ref[...], vtl_ref[n_lat_blocks - 2], m, l, False)
        m, l = consume(sb_ref[...], vtl_ref[n_lat_blocks - 1], m, l, False)
    o_ref[...] = jnp.transpose(acc_ref[...] / l).astype(o_ref.dtype)


def _attention(qt, k_ctx, vt_ctx, k_lat, vt_lat, tq, tk):
    bsz, heads, _, nq = qt.shape
    m_ctx = k_ctx.shape[2]
    n_lat_blocks = 0 if k_lat is None else k_lat.shape[2] // tk
    in_specs = [
        pl.BlockSpec((None, None, QK_PAD, tq), lambda b, h, i: (b, h, 0, i)),
        pl.BlockSpec((None, None, m_ctx, QK_PAD), lambda b, h, i: (b, h, 0, 0)),
        pl.BlockSpec((None, None, V_DIM, m_ctx), lambda b, h, i: (b, h, 0, 0)),
    ]
    args = [qt, k_ctx, vt_ctx]
    if n_lat_blocks:
        in_specs += [
            pl.BlockSpec((None, None, k_lat.shape[2], QK_PAD), lambda b, h, i: (b, h, 0, 0)),
            pl.BlockSpec((None, None, n_lat_blocks, V_DIM, tk), lambda b, h, i: (b, h, 0, 0, 0)),
        ]
        args += [k_lat, vt_lat]
    return pl.pallas_call(
        functools.partial(_attn_kernel, tk=tk, n_lat_blocks=n_lat_blocks),
        out_shape=jax.ShapeDtypeStruct((bsz, nq, heads * V_DIM), BF16),
        grid=(bsz, heads, nq // tq),
        in_specs=in_specs,
        out_specs=pl.BlockSpec((None, tq, V_DIM), lambda b, h, i: (b, i, h)),
        scratch_shapes=[pltpu.VMEM((V_DIM, tq), F32)] + [pltpu.VMEM((tk, tq), F32)] * (2 if n_lat_blocks else 0),
        compiler_params=_cparams(("parallel", "parallel", "arbitrary")),
        name="attention",
    )(*args)


def _mix_out_kernel(x_ref, ya_ref, yb_ref, yc_ref, wo_ref, g1_ref, sh_ref, sc_ref, wrh_ref, wrl_ref,
                    xn_ref, h2_ref, aff_ref, *, tm):
    mix = jnp.dot(ya_ref[...], wo_ref[0:SGU_W, :], preferred_element_type=F32)
    mix += jnp.dot(yb_ref[...].astype(BF16), wo_ref[SGU_W:SGU_W + FNET_W, :], preferred_element_type=F32)
    mix += jnp.dot(yc_ref[...], wo_ref[SGU_W + FNET_W:, :], preferred_element_type=F32)
    xn = x_ref[...] + g1_ref[...] * mix
    xn_ref[...] = xn
    d = xn.shape[-1]
    h2 = xn * _rms_scale(xn, d) * (1.0 + sc_ref[...]) + sh_ref[...]
    for i8 in range(tm // SUBLANES):
        for j in range(d // LANES):
            h2_ref[pl.ds(i8 * SUBLANES * SUBLANES + j, SUBLANES, stride=SUBLANES), :] = (
                h2[i8 * SUBLANES:(i8 + 1) * SUBLANES, j * LANES:(j + 1) * LANES])
    h_hi = h2.astype(BF16)
    h_lo = (h2 - h_hi.astype(F32)).astype(BF16)
    nt_dims = (((1,), (1,)), ((), ()))
    logits = lax.dot_general(wrh_ref[...], h_hi, nt_dims, preferred_element_type=F32)
    logits += lax.dot_general(wrh_ref[...], h_lo, nt_dims, preferred_element_type=F32)
    logits += lax.dot_general(wrl_ref[...], h_hi, nt_dims, preferred_element_type=F32)
    e = jnp.exp(logits - jnp.max(logits, axis=0, keepdims=True))
    aff_ref[...] = e / jnp.sum(e, axis=0, keepdims=True)


def _mix_out(x, ya, yb, yc, w_out, g1, sh2, sc2, wr_hi, wr_lo, tm):
    bsz, nt, d = x.shape
    vec = pl.BlockSpec((None, 1, d), lambda b, i: (b, 0, 0))
    full = lambda arr: pl.BlockSpec(arr.shape, lambda b, i: (0,) * arr.ndim)
    return pl.pallas_call(
        functools.partial(_mix_out_kernel, tm=tm),
        out_shape=(
            jax.ShapeDtypeStruct((bsz, nt, d), F32),
            jax.ShapeDtypeStruct((bsz, nt * SUBLANES, LANES), F32),
            jax.ShapeDtypeStruct((bsz, N_EXPERTS, nt), F32),
        ),
        grid=(bsz, nt // tm),
        in_specs=[
            pl.BlockSpec((None, tm, d), lambda b, i: (b, i, 0)),
            pl.BlockSpec((None, tm, SGU_W), lambda b, i: (b, i, 0)),
            pl.BlockSpec((None, tm, FNET_W), lambda b, i: (b, i, 0)),
            pl.BlockSpec((None, tm, MLA_HEADS * V_DIM), lambda b, i: (b, i, 0)),
            full(w_out), vec, vec, vec, full(wr_hi), full(wr_lo),
        ],
        out_specs=(
            pl.BlockSpec((None, tm, d), lambda b, i: (b, i, 0)),
            pl.BlockSpec((None, tm * SUBLANES, LANES), lambda b, i: (b, i, 0)),
            pl.BlockSpec((None, N_EXPERTS, tm), lambda b, i: (b, 0, i)),
        ),
        compiler_params=_cparams(("parallel", "parallel")),
        name="mix_out",
    )(x, ya, yb, yc, w_out, g1, sh2, sc2, wr_hi, wr_lo)


def _prefix_count(mask, tri):
    e, n = mask.shape
    w = tri.shape[0]
    carry = jnp.zeros((e, 1), F32)
    outs = []
    ends = []
    for c in range(n // w):
        local = jnp.dot(mask[:, c * w:(c + 1) * w].astype(BF16), tri, preferred_element_type=F32)
        outs.append(local + carry)
        carry = carry + local[:, w - 1:w]
        ends.append(carry)
    return jnp.concatenate(outs, axis=1), ends


ROUTE_CHUNK = 2 * LANES


def _route_select_kernel(aff_ref, key_ref, cend_ref, *, cap):
    n_exp, n = aff_ref.shape
    w = min(n, ROUTE_CHUNK)
    aff = aff_ref[...]

    def step(i, thr_bits):
        cand = thr_bits | jnp.left_shift(jnp.int32(1), 30 - i)
        cnt = jnp.sum((aff >= pltpu.bitcast(cand, F32)).astype(jnp.int32), axis=-1, keepdims=True)
        return jnp.where(cnt >= cap, cand, thr_bits)

    thr = pltpu.bitcast(lax.fori_loop(0, 31, step, jnp.zeros((n_exp, 1), jnp.int32)), F32)
    gt = aff > thr
    eq = aff == thr
    need = cap - jnp.sum(gt.astype(jnp.int32), axis=-1, keepdims=True)
    r = lax.broadcasted_iota(jnp.int32, (w, w), 0)
    c = lax.broadcasted_iota(jnp.int32, (w, w), 1)
    tri = (r <= c).astype(BF16)
    eq_rank, _ = _prefix_count(eq.astype(F32), tri)
    sel = jnp.logical_or(gt, jnp.logical_and(eq, eq_rank <= need.astype(F32)))
    cum, ends = _prefix_count(sel.astype(F32), tri)
    key_ref[...] = jnp.where(sel, cum, 0.0)
    lane = lax.broadcasted_iota(jnp.int32, (n_exp, LANES), 1)
    cend = jnp.zeros((n_exp, LANES), F32)
    for ci, end in enumerate(ends):
        cend = jnp.where(lane == ci, end, cend)
    cend_ref[...] = cend.astype(jnp.int32)


def _route_compact_kernel(cend_ref, key_ref, aff_ref, idx_ref, gate_ref, acc_i_ref, acc_g_ref, *, cap, jb):
    b = pl.program_id(0)
    e = pl.program_id(1)
    n_chunks, w = key_ref.shape
    n_sb = cap // jb
    base = (b * pl.num_programs(1) + e) * LANES
    shift = jb.bit_length() - 1
    acc_i_ref[...] = jnp.zeros_like(acc_i_ref)
    acc_g_ref[...] = jnp.zeros_like(acc_g_ref)

    def chunk(c, _):
        c_start = jnp.where(c > 0, cend_ref[base + jnp.maximum(c - 1, 0)], 0)
        c_end = cend_ref[base + c]

        @pl.when(c_end > c_start)
        def _any_selected():
            krow = key_ref[pl.ds(c, 1), :]
            arow = aff_ref[pl.ds(c, 1), :]
            tpos = (lax.broadcasted_iota(jnp.int32, (1, w), 1) + c * w).astype(F32)
            sb_lo = lax.shift_right_logical(c_start, shift)
            sb_hi = jnp.minimum(lax.shift_right_logical(c_end - 1, shift), n_sb - 1)

            def slot_block(sb, _):
                slot = (lax.broadcasted_iota(jnp.int32, (jb, 1), 0) + (sb * jb + 1)).astype(F32)
                hit = krow == slot
                pi = jnp.where(hit, tpos, 0.0)
                pg = jnp.where(hit, arow, 0.0)
                fi = pi[:, 0:LANES]
                fg = pg[:, 0:LANES]
                for k in range(1, w // LANES):
                    fi = fi + pi[:, k * LANES:(k + 1) * LANES]
                    fg = fg + pg[:, k * LANES:(k + 1) * LANES]
                acc_i_ref[sb] = acc_i_ref[sb] + fi
                acc_g_ref[sb] = acc_g_ref[sb] + fg
                return 0

            lax.fori_loop(sb_lo, sb_hi + 1, slot_block, 0)

        return 0

    lax.fori_loop(0, n_chunks, chunk, 0)
    for sb in range(n_sb):
        idx_ref[sb * jb:(sb + 1) * jb, :] = jnp.sum(acc_i_ref[sb], axis=-1, keepdims=True).astype(jnp.int32)
        gate_ref[sb * jb:(sb + 1) * jb, :] = jnp.sum(acc_g_ref[sb], axis=-1, keepdims=True)


def _route(aff_t, cap):
    bsz, n_exp, n = aff_t.shape
    jb = min(cap, LANES)
    w = min(n, ROUTE_CHUNK)
    key, cend = pl.pallas_call(
        functools.partial(_route_select_kernel, cap=cap),
        out_shape=(jax.ShapeDtypeStruct((bsz, n_exp, n), F32),
                   jax.ShapeDtypeStruct((bsz, n_exp, LANES), jnp.int32)),
        grid=(bsz,),
        in_specs=[pl.BlockSpec((None, n_exp, n), lambda b: (b, 0, 0))],
        out_specs=(pl.BlockSpec((None, n_exp, n), lambda b: (b, 0, 0)),
                   pl.BlockSpec((None, n_exp, LANES), lambda b: (b, 0, 0))),
        compiler_params=_cparams(("parallel",)),
        name="route_select",
    )(aff_t)
    chunked = pl.BlockSpec((None, None, n // w, w), lambda b, e, cend: (b, e, 0, 0))
    out_blk = pl.BlockSpec((None, None, cap, 1), lambda b, e, cend: (b, e, 0, 0))
    grid_spec = pltpu.PrefetchScalarGridSpec(
        num_scalar_prefetch=1,
        grid=(bsz, n_exp),
        in_specs=[chunked, chunked],
        out_specs=(out_blk, out_blk),
        scratch_shapes=[pltpu.VMEM((cap // jb, jb, LANES), F32)] * 2,
    )
    return pl.pallas_call(
        functools.partial(_route_compact_kernel, cap=cap, jb=jb),
        out_shape=(jax.ShapeDtypeStruct((bsz, n_exp, cap, 1), jnp.int32),
                   jax.ShapeDtypeStruct((bsz, n_exp, cap, 1), F32)),
        grid_spec=grid_spec,
        compiler_params=_cparams(("parallel", "parallel")),
        name="route_compact",
    )(cend.reshape(-1), key.reshape(bsz, n_exp, n // w, w), aff_t.reshape(bsz, n_exp, n // w, w))


GATHER_UNROLL = 8


def _gather_start(idx_ref, h_hbm, xs_ref, sem, slot, e, b, n_exp, cap):
    base = (b * n_exp + e) * cap
    unroll = math.gcd(cap, GATHER_UNROLL)

    def issue(g, _):
        for u in range(unroll):
            i = g * unroll + u
            tok = idx_ref[base + i]
            pltpu.make_async_copy(
                h_hbm.at[b, pl.ds(pl.multiple_of(tok * SUBLANES, SUBLANES), SUBLANES), :],
                xs_ref.at[slot, pl.ds(pl.multiple_of(i * SUBLANES, SUBLANES), SUBLANES), :],
                sem.at[slot]).start()
        return 0

    lax.fori_loop(0, cap // unroll, issue, 0)


def _expert_kernel(idx_ref, h_hbm, wg_ref, wu_ref, wd_ref, gate_ref, ys_ref, xs_ref, sem, *, cap, d):
    e = pl.program_id(0)
    b = pl.program_id(1)
    n_exp = pl.num_programs(0)
    n_b = pl.num_programs(1)
    step = e * n_b + b
    slot = lax.rem(step, 2)

    @pl.when(step == 0)
    def _first():
        _gather_start(idx_ref, h_hbm, xs_ref, sem, 0, e, b, n_exp, cap)

    nxt = step + 1

    @pl.when(nxt < n_exp * n_b)
    def _prefetch():
        _gather_start(idx_ref, h_hbm, xs_ref, sem, 1 - slot, lax.div(nxt, n_b), lax.rem(nxt, n_b), n_exp, cap)

    pltpu.make_async_copy(h_hbm.at[b, pl.ds(0, cap * SUBLANES), :], xs_ref.at[slot], sem.at[slot]).wait()

    x = jnp.concatenate(
        [xs_ref[slot, pl.ds(j, cap, stride=SUBLANES), :] for j in range(d // LANES)], axis=1).astype(BF16)
    hid = _silu(jnp.dot(x, wg_ref[...].astype(BF16), preferred_element_type=F32))
    hid = hid * jnp.dot(x, wu_ref[...].astype(BF16), preferred_element_type=F32)
    y = jnp.dot(hid.astype(BF16), wd_ref[...].astype(BF16), preferred_element_type=F32) * gate_ref[...]
    for j in range(d // LANES):
        ys_ref[pl.ds(j, cap, stride=SUBLANES), :] = y[:, j * LANES:(j + 1) * LANES]


def _experts(idx_flat, h2_tiles, wg, wu, wd, layer, gate, cap):
    bsz = h2_tiles.shape[0]
    _, n_exp, d, f = wg.shape
    grid_spec = pltpu.PrefetchScalarGridSpec(
        num_scalar_prefetch=1,
        grid=(n_exp, bsz),
        in_specs=[
            pl.BlockSpec(memory_space=pl.ANY),
            pl.BlockSpec((None, None, d, f), lambda e, b, idx: (layer, e, 0, 0)),
            pl.BlockSpec((None, None, d, f), lambda e, b, idx: (layer, e, 0, 0)),
            pl.BlockSpec((None, None, f, d), lambda e, b, idx: (layer, e, 0, 0)),
            pl.BlockSpec((None, None, cap, 1), lambda e, b, idx: (b, e, 0, 0)),
        ],
        out_specs=pl.BlockSpec((None, None, cap * SUBLANES, LANES), lambda e, b, idx: (b, e, 0, 0)),
        scratch_shapes=[pltpu.VMEM((2, cap * SUBLANES, LANES), F32), pltpu.SemaphoreType.DMA((2,))],
    )
    return pl.pallas_call(
        functools.partial(_expert_kernel, cap=cap, d=d),
        out_shape=jax.ShapeDtypeStruct((bsz, n_exp, cap * SUBLANES, LANES), F32),
        grid_spec=grid_spec,
        compiler_params=_cparams(("arbitrary", "arbitrary")),
        name="experts",
    )(idx_flat, h2_tiles, wg, wu, wd, gate)


def _combine_kernel(idx_ref, ys_ref, o_hbm, acc_ref, sem, *, cap, unroll):
    b = pl.program_id(0)
    e = pl.program_id(1)
    n_exp = pl.num_programs(1)
    base = (b * n_exp + e) * cap

    @pl.when(e == 0)
    def _zero():
        acc_ref[...] = jnp.zeros_like(acc_ref)

    def rows(g, _):
        i0 = g * unroll
        toks = [pl.multiple_of(idx_ref[base + i0 + u] * SUBLANES, SUBLANES) for u in range(unroll)]
        sums = [acc_ref[pl.ds(toks[u], SUBLANES), :]
                + ys_ref[pl.ds(pl.multiple_of((i0 + u) * SUBLANES, SUBLANES), SUBLANES), :]
                for u in range(unroll)]
        for u in range(unroll):
            acc_ref[pl.ds(toks[u], SUBLANES), :] = sums[u]
        return 0

    lax.fori_loop(0, cap // unroll, rows, 0)

    @pl.when(e == n_exp - 1)
    def _flush():
        cp = pltpu.make_async_copy(acc_ref, o_hbm.at[b], sem)
        cp.start()
        cp.wait()


def _combine(idx_flat, ys_tiles, nt, cap):
    bsz, n_exp = ys_tiles.shape[:2]
    grid_spec = pltpu.PrefetchScalarGridSpec(
        num_scalar_prefetch=1,
        grid=(bsz, n_exp),
        in_specs=[pl.BlockSpec((None, None, cap * SUBLANES, LANES), lambda b, e, idx: (b, e, 0, 0))],
        out_specs=pl.BlockSpec(memory_space=pl.ANY),
        scratch_shapes=[pltpu.VMEM((nt * SUBLANES, LANES), F32), pltpu.SemaphoreType.DMA(())],
    )
    return pl.pallas_call(
        functools.partial(_combine_kernel, cap=cap, unroll=4),
        out_shape=jax.ShapeDtypeStruct((bsz, nt * SUBLANES, LANES), F32),
        grid_spec=grid_spec,
        compiler_params=_cparams(("arbitrary", "arbitrary")),
        name="combine",
    )(idx_flat, ys_tiles)


def _residual_kernel(x_ref, m_ref, g_ref, o_ref, *, tm):
    d = x_ref.shape[-1]
    moe = jnp.concatenate(
        [m_ref[pl.ds(j, tm, stride=SUBLANES), :] for j in range(d // LANES)], axis=1)
    o_ref[...] = x_ref[...] + g_ref[...] * moe


def _residual(x, moe_tiles, g2, tm):
    bsz, nt, d = x.shape
    return pl.pallas_call(
        functools.partial(_residual_kernel, tm=tm),
        out_shape=jax.ShapeDtypeStruct((bsz, nt, d), F32),
        grid=(bsz, nt // tm),
        in_specs=[
            pl.BlockSpec((None, tm, d), lambda b, i: (b, i, 0)),
            pl.BlockSpec((None, tm * SUBLANES, LANES), lambda b, i: (b, i, 0)),
            pl.BlockSpec((None, 1, d), lambda b, i: (b, 0, 0)),
        ],
        out_specs=pl.BlockSpec((None, tm, d), lambda b, i: (b, i, 0)),
        compiler_params=_cparams(("parallel", "parallel")),
        name="residual",
    )(x, moe_tiles, g2)


def _prepare_layer(l, w_in, sgu_norm, w_sgu, b_sgu, q_lora_norm, w_uq, kv_lora_norm, w_ukv, q_norm, k_norm,
                   w_out, w_router, w_gate, w_up, w_down):
    w_in_l = w_in[l]
    kr_cols = w_in_l[:, OFF_KR:OFF_KR + QK_ROPE]
    w_in_ext = jnp.concatenate([w_in_l[:, :OFF_KR], kr_cols, kr_cols], axis=1).astype(BF16)
    uq = w_uq[l].reshape(Q_LORA, MLA_HEADS, QK_DIM)
    w_uq_p = jnp.concatenate(
        [uq[:, :, :QK_NOPE].reshape(Q_LORA, MLA_HEADS * QK_NOPE),
         uq[:, :, QK_NOPE:].reshape(Q_LORA, MLA_HEADS * QK_ROPE)], axis=1).astype(BF16)

    def norm_pad(v):
        return jnp.concatenate([v[:QK_NOPE], v[QK_NOPE:], v[QK_NOPE:]])[None, :]

    wr_t = w_router[l].T
    wr_hi = wr_t.astype(BF16)
    wr_lo = (wr_t - wr_hi.astype(F32)).astype(BF16)
    return dict(
        w_in=w_in_ext,
        sgu_norm=sgu_norm[l][None, :],
        w_sgu=jnp.concatenate([w_sgu[l][hh] for hh in range(SGU_HEADS)], axis=1).astype(BF16),
        b_sgu=jnp.repeat(b_sgu[l].T, SGU_HEAD_DIM, axis=1),
        fdft=jnp.asarray(_channel_dft_table()).astype(BF16),
        q_lora_norm=q_lora_norm[l][None, :],
        w_uq=w_uq_p,
        q_norm=norm_pad(q_norm[l]),
        kv_lora_norm=kv_lora_norm[l][None, :],
        w_ukv=w_ukv[l].astype(BF16),
        k_norm=norm_pad(k_norm[l]),
        w_out=w_out[l].astype(BF16),
        wr_hi=wr_hi, wr_lo=wr_lo,
        w_gate=w_gate, w_up=w_up, w_down=w_down, layer=l,
    )


def _rope_tables(n):
    pos = np.arange(n)
    half = QK_ROPE // 4
    freqs = ROPE_THETA ** (-np.arange(half, dtype=np.float64) / half)
    ang_r = (pos // GRID_WIDTH)[:, None] * freqs
    ang_c = (pos % GRID_WIDTH)[:, None] * freqs
    cos64 = np.concatenate([np.cos(ang_r), np.cos(ang_r), np.cos(ang_c), np.cos(ang_c)], axis=1)
    sin64 = np.concatenate([-np.sin(ang_r), np.sin(ang_r), -np.sin(ang_c), np.sin(ang_c)], axis=1)
    return (jnp.asarray(np.tile(cos64, (1, 2)), F32), jnp.asarray(np.tile(sin64, (1, 2)), F32))


def _token_tile(nt, pref):
    return pref if nt % pref == 0 else nt


def _moe(x_new, h2_tiles, aff_t, lw, g2, tm):
    bsz, nt, d = x_new.shape
    cap = EC_FACTOR * nt // N_EXPERTS
    idx, gate = _route(aff_t, cap)
    idx_flat = idx.reshape(-1)
    ys = _experts(idx_flat, h2_tiles, lw["w_gate"], lw["w_up"], lw["w_down"], lw["layer"], gate, cap)
    moe_tiles = _combine(idx_flat, ys, nt, cap)
    return _residual(x_new, moe_tiles, g2, tm)


def kernel(x, c, ctx, c_ctx, w_ada, b_ada, w_in, sgu_norm, w_sgu, b_sgu, q_lora_norm, w_uq, kv_lora_norm, w_ukv,
           q_norm, k_norm, w_out, w_router, w_gate, w_up, w_down):
    bsz, n, d = x.shape
    n_ctx = ctx.shape[1]
    depth = w_ada.shape[0]
    assert bsz + 1 <= SUBLANES

    cond = jnp.zeros((SUBLANES, d), F32).at[:bsz].set(c).at[bsz].set(c_ctx)
    mod = _ada_modulation(cond, w_ada, b_ada)
    rope_tabs = _rope_tables(n)
    tm = _token_tile(n, 512)
    tm_in = _token_tile(n, 256)
    tm_c = _token_tile(n_ctx, 256)

    xc = ctx
    for l in range(depth):
        last = l == depth - 1
        lw = _prepare_layer(l, w_in, sgu_norm, w_sgu, b_sgu, q_lora_norm, w_uq, kv_lora_norm, w_ukv,
                            q_norm, k_norm, w_out, w_router, w_gate, w_up, w_down)
        parts = [mod[l, :, i * d:(i + 1) * d] for i in range(6)]
        lat = [p[:bsz, None, :] for p in parts]
        cx = [jnp.broadcast_to(p[bsz][None, None, :], (bsz, 1, d)) for p in parts]

        ya_c, a_c, b_c, q_c, k_c, v_c = _mix_in(xc, cx[0], cx[1], lw, None, tm_c)
        ya, a, b, q, k, v = _mix_in(x, lat[0], lat[1], lw, rope_tabs, tm_in)
        yb = _fourier_positions(a, b)
        tk = _token_tile(n, ATTN_TK)
        vt_c = jnp.swapaxes(v_c, 2, 3)
        vt = jnp.swapaxes(v.reshape(bsz, MLA_HEADS, n // tk, tk, V_DIM), 3, 4)
        yc = _attention(jnp.swapaxes(q, 2, 3), k_c, vt_c, k, vt, tq=_token_tile(n, ATTN_TQ), tk=tk)
        x_new, h2_tiles, aff_t = _mix_out(x, ya, yb, yc, lw["w_out"], lat[2], lat[3], lat[4],
                                          lw["wr_hi"], lw["wr_lo"], tm)
        if not last:
            yb_c = _fourier_positions(a_c, b_c)
            yc_c = _attention(jnp.swapaxes(q_c, 2, 3), k_c, vt_c, None, None,
                              tq=_token_tile(n_ctx, ATTN_TQ), tk=n_ctx)
            xc_new, h2c_tiles, affc_t = _mix_out(xc, ya_c, yb_c, yc_c, lw["w_out"], cx[2], cx[3], cx[4],
                                                 lw["wr_hi"], lw["wr_lo"], tm_c)
            xc = _moe(xc_new, h2c_tiles, affc_t, lw, cx[5], tm_c)
        x = _moe(x_new, h2_tiles, aff_t, lw, lat[5], tm)
    return x
```

```python
import functools
import math

import numpy as np
import jax
import jax.numpy as jnp
from jax import lax
from jax.experimental import pallas as pl
from jax.experimental.pallas import tpu as pltpu

F32 = jnp.float32
BF16 = jnp.bfloat16

GRID_WIDTH = 64
SGU_HEADS = 4
SGU_HEAD_DIM = 64
SGU_W = SGU_HEADS * SGU_HEAD_DIM
CHUNK = 128
FNET_GROUP_DIM = 64
FNET_W = 256
MLA_HEADS = 4
QK_NOPE = 128
QK_ROPE = 64
QK_DIM = QK_NOPE + QK_ROPE
V_DIM = 128
Q_LORA = 256
KV_LORA = 128
OFF_V = 256
OFF_F = 512
OFF_Q = 768
OFF_KV = 1024
OFF_KR = 1152
N_EXPERTS = 16
EC_FACTOR = 2
ROPE_THETA = 10000.0
EPS = 1e-6

LANES = 128
SUBLANES = 8
VMEM_LIMIT_BYTES = 56 * 1024 * 1024

QK_PAD = 2 * LANES
ATTN_TQ = 1024
ATTN_TK = 512
LOG2_E = 1.4426950408889634
MIX_ROWS = 128
MIX_OUT_ROWS = 512


def _cparams(sem, vmem=None):
    return pltpu.CompilerParams(dimension_semantics=sem, vmem_limit_bytes=vmem or VMEM_LIMIT_BYTES)


def _rms_scale(v, width):
    return lax.rsqrt(jnp.sum(v * v, axis=-1, keepdims=True) * (1.0 / width) + EPS)


def _gelu_tanh(v):
    return 0.5 * v * (1.0 + jnp.tanh(0.7978845608028654 * (v + 0.044715 * v * v * v)))


def _silu(v):
    return v / (1.0 + jnp.exp(-v))


def _ada_kernel(cond_ref, w_ref, b_ref, o_ref):
    s = _silu(cond_ref[...])
    o_ref[...] = jnp.dot(s, w_ref[...], preferred_element_type=F32,
                         precision=lax.Precision.HIGHEST) + b_ref[...]


def _ada_modulation(cond, w_ada, b_ada):
    depth, d, six_d = w_ada.shape
    tn = 1536
    return pl.pallas_call(
        _ada_kernel,
        out_shape=jax.ShapeDtypeStruct((depth, SUBLANES, six_d), F32),
        grid=(depth, six_d // tn),
        in_specs=[
            pl.BlockSpec((SUBLANES, d), lambda l, j: (0, 0)),
            pl.BlockSpec((None, d, tn), lambda l, j: (l, 0, j)),
            pl.BlockSpec((None, 1, tn), lambda l, j: (l, 0, j)),
        ],
        out_specs=pl.BlockSpec((None, SUBLANES, tn), lambda l, j: (l, 0, j)),
        compiler_params=_cparams(("arbitrary", "arbitrary")),
        name="ada_modulation",
    )(cond, w_ada, b_ada.reshape(depth, 1, six_d))


def _swap16(v):
    lane = lax.broadcasted_iota(jnp.int32, v.shape, 1)
    up = pltpu.roll(v, LANES - 16, axis=1)
    dn = pltpu.roll(v, 16, axis=1)
    return jnp.where((lane % 32) < 16, up, dn)


def _mix_in_kernel(x_ref, sh_ref, sc_ref, win_ref, sgun_ref, wsgu_ref, bsgu_ref, fdft_ref,
                   qln_ref, wuq_ref, qn_ref, kvn_ref, wukv_ref, kn_ref, cos_ref, sin_ref,
                   ya_ref, a_ref, b_ref, q_ref, k_ref, v_ref, *, tm, use_rope):
    rows_per_group = min(tm, MIX_ROWS)
    for g in range(tm // rows_per_group):
        _mix_in_rows(slice(g * rows_per_group, (g + 1) * rows_per_group), x_ref, sh_ref, sc_ref, win_ref,
                     sgun_ref, wsgu_ref, bsgu_ref, fdft_ref, qln_ref, wuq_ref, qn_ref, kvn_ref, wukv_ref, kn_ref,
                     cos_ref, sin_ref, ya_ref, a_ref, b_ref, q_ref, k_ref, v_ref, use_rope)


def _mix_in_rows(rg, x_ref, sh_ref, sc_ref, win_ref, sgun_ref, wsgu_ref, bsgu_ref, fdft_ref,
                 qln_ref, wuq_ref, qn_ref, kvn_ref, wukv_ref, kn_ref, cos_ref, sin_ref,
                 ya_ref, a_ref, b_ref, q_ref, k_ref, v_ref, use_rope):
    x = x_ref[rg, :]
    tm = x.shape[0]
    d = x.shape[-1]
    h = x * _rms_scale(x, d) * (1.0 + sc_ref[...]) + sh_ref[...]
    p = jnp.dot(h.astype(BF16), win_ref[...], preferred_element_type=F32)

    u = _gelu_tanh(p[:, 0:OFF_V])
    gv = _gelu_tanh(p[:, OFF_V:OFF_F])
    vn = gv * _rms_scale(gv, SGU_W) * sgun_ref[...]
    head_of_lane = lax.broadcasted_iota(jnp.int32, (CHUNK, SGU_W), 1) // SGU_HEAD_DIM
    for c in range(tm // CHUNK):
        rows = slice(c * CHUNK, (c + 1) * CHUNK)
        vc = vn[rows]
        vstack = jnp.concatenate(
            [jnp.where(head_of_lane == hh, vc, 0.0) for hh in range(SGU_HEADS)], axis=0).astype(BF16)
        z = jnp.dot(wsgu_ref[...], vstack, preferred_element_type=F32) + bsgu_ref[...]
        ya_ref[rg.start + c * CHUNK:rg.start + (c + 1) * CHUNK, :] = (u[rows] * z).astype(ya_ref.dtype)

    ab = jnp.dot(p[:, OFF_F:OFF_Q].astype(BF16), fdft_ref[...], preferred_element_type=F32)
    a_ref[rg, :] = ab[:, 0:FNET_W].astype(a_ref.dtype)
    b_ref[rg, :] = ab[:, FNET_W:2 * FNET_W].astype(b_ref.dtype)

    lane = lax.broadcasted_iota(jnp.int32, (1, LANES), 1)
    lo = lane < QK_ROPE
    if use_rope:
        cos = cos_ref[rg, :]
        sin = sin_ref[rg, :]

    def rope(t):
        return t * cos + _swap16(t) * sin if use_rope else t

    pq = p[:, OFF_Q:OFF_KV]
    cq = pq * _rms_scale(pq, Q_LORA) * qln_ref[...]
    q = jnp.dot(cq.astype(BF16), wuq_ref[...], preferred_element_type=F32)
    qnorm = qn_ref[...]
    for hh in range(MLA_HEADS):
        qn = q[:, LANES * hh:LANES * (hh + 1)]
        pair = hh // 2
        qr = q[:, MLA_HEADS * LANES + LANES * pair:MLA_HEADS * LANES + LANES * (pair + 1)]
        qr = jnp.where(lo if hh % 2 == 0 else jnp.logical_not(lo), qr, 0.0)
        ssq = jnp.sum(qn * qn, axis=-1, keepdims=True) + jnp.sum(qr * qr, axis=-1, keepdims=True)
        rinv = lax.rsqrt(ssq * (1.0 / QK_DIM) + EPS) * (QK_DIM ** -0.5 * LOG2_E)
        q_ref[hh, rg, 0:LANES] = (qn * rinv * qnorm[:, 0:LANES]).astype(q_ref.dtype)
        q_ref[hh, rg, LANES:QK_PAD] = rope(qr * rinv * qnorm[:, LANES:QK_PAD]).astype(q_ref.dtype)

    pkv = p[:, OFF_KV:OFF_KR]
    ckv = pkv * _rms_scale(pkv, KV_LORA) * kvn_ref[...]
    kv = jnp.dot(ckv.astype(BF16), wukv_ref[...], preferred_element_type=F32)
    krr = p[:, OFF_KR:OFF_KR + LANES]
    ssq_kr = 0.5 * jnp.sum(krr * krr, axis=-1, keepdims=True)
    knorm = kn_ref[...]
    rinvs = []
    for hh in range(MLA_HEADS):
        kn = kv[:, 2 * LANES * hh:2 * LANES * hh + LANES]
        ssq = jnp.sum(kn * kn, axis=-1, keepdims=True) + ssq_kr
        rinv = lax.rsqrt(ssq * (1.0 / QK_DIM) + EPS)
        rinvs.append(rinv)
        k_ref[hh, rg, 0:LANES] = (kn * rinv * knorm[:, 0:LANES]).astype(k_ref.dtype)
        v_ref[hh, rg, :] = kv[:, 2 * LANES * hh + LANES:2 * LANES * (hh + 1)].astype(v_ref.dtype)
    for pair in range(MLA_HEADS // 2):
        r2 = jnp.where(lo, rinvs[2 * pair], rinvs[2 * pair + 1])
        kr = rope(krr * r2 * knorm[:, LANES:QK_PAD]).astype(k_ref.dtype)
        k_ref[2 * pair, rg, LANES:QK_PAD] = kr
        k_ref[2 * pair + 1, rg, LANES:QK_PAD] = kr


def _mix_in(x, sh, sc, lw, rope_tabs, tm):
    bsz, nt, d = x.shape
    use_rope = rope_tabs is not None
    if use_rope:
        cos_t, sin_t = rope_tabs
    else:
        cos_t = sin_t = jnp.zeros((nt, LANES), F32)
    full = lambda arr: pl.BlockSpec(arr.shape, lambda b, i: (0,) * arr.ndim)
    vec = pl.BlockSpec((None, 1, d), lambda b, i: (b, 0, 0))
    in_w = lw["w_in"].shape[1]
    kern = functools.partial(_mix_in_kernel, tm=tm, use_rope=use_rope)
    return pl.pallas_call(
        kern,
        out_shape=(
            jax.ShapeDtypeStruct((bsz, nt, SGU_W), BF16),
            jax.ShapeDtypeStruct((bsz, nt, FNET_W), F32),
            jax.ShapeDtypeStruct((bsz, nt, FNET_W), F32),
            jax.ShapeDtypeStruct((bsz, MLA_HEADS, nt, QK_PAD), BF16),
            jax.ShapeDtypeStruct((bsz, MLA_HEADS, nt, QK_PAD), BF16),
            jax.ShapeDtypeStruct((bsz, MLA_HEADS, nt, V_DIM), BF16),
        ),
        grid=(bsz, nt // tm),
        in_specs=[
            pl.BlockSpec((None, tm, d), lambda b, i: (b, i, 0)),
            vec, vec,
            full(lw["w_in"]), full(lw["sgu_norm"]), full(lw["w_sgu"]), full(lw["b_sgu"]), full(lw["fdft"]),
            full(lw["q_lora_norm"]), full(lw["w_uq"]), full(lw["q_norm"]),
            full(lw["kv_lora_norm"]), full(lw["w_ukv"]), full(lw["k_norm"]),
            pl.BlockSpec((tm, LANES), lambda b, i: (i, 0)),
            pl.BlockSpec((tm, LANES), lambda b, i: (i, 0)),
        ],
        out_specs=(
            pl.BlockSpec((None, tm, SGU_W), lambda b, i: (b, i, 0)),
            pl.BlockSpec((None, tm, FNET_W), lambda b, i: (b, i, 0)),
            pl.BlockSpec((None, tm, FNET_W), lambda b, i: (b, i, 0)),
            pl.BlockSpec((None, MLA_HEADS, tm, QK_PAD), lambda b, i: (b, 0, i, 0)),
            pl.BlockSpec((None, MLA_HEADS, tm, QK_PAD), lambda b, i: (b, 0, i, 0)),
            pl.BlockSpec((None, MLA_HEADS, tm, V_DIM), lambda b, i: (b, 0, i, 0)),
        ),
        compiler_params=_cparams(("parallel", "parallel")),
        name="mix_in",
    )(x, sh, sc, lw["w_in"], lw["sgu_norm"], lw["w_sgu"], lw["b_sgu"], lw["fdft"],
      lw["q_lora_norm"], lw["w_uq"], lw["q_norm"], lw["kv_lora_norm"], lw["w_ukv"], lw["k_norm"],
      cos_t, sin_t)


DFT_UNROLL = 8


def _dft_kernel(a_ref, b_ref, w_ref, m_ref, o_ref, yr_ref, yi_ref, *, n1, n2):
    w1 = w_ref[...]
    c = a_ref.shape[-1]

    def step1(g, _):
        cols = []
        for u in range(DFT_UNROLL):
            rows = pl.ds(g * DFT_UNROLL + u, n1, stride=n2)
            cols.append(jnp.concatenate([a_ref[rows, :], b_ref[rows, :]], axis=0))
        r = jnp.dot(w1, jnp.concatenate(cols, axis=1).astype(BF16), preferred_element_type=F32)
        for u in range(DFT_UNROLL):
            out_rows = pl.ds(pl.multiple_of((g * DFT_UNROLL + u) * n1, n1), n1)
            yr_ref[out_rows, :] = r[0:n1, u * c:(u + 1) * c]
            yi_ref[out_rows, :] = r[n1:2 * n1, u * c:(u + 1) * c]
        return 0

    lax.fori_loop(0, n2 // DFT_UNROLL, step1, 0)

    def step2(g, _):
        for u in range(DFT_UNROLL):
            k1 = g * DFT_UNROLL + u
            rows = pl.ds(k1, n2, stride=n1)
            ycat = jnp.concatenate([yr_ref[rows, :], yi_ref[rows, :]], axis=0).astype(BF16)
            o_ref[rows, :] = jnp.dot(m_ref[k1], ycat, preferred_element_type=F32)
        return 0

    lax.fori_loop(0, n1 // DFT_UNROLL, step2, 0)


def _dft_factors(n):
    n1 = 64 if n >= 1024 else 32
    return n1, n // n1


@functools.lru_cache(maxsize=None)
def _dft_tables(n):
    n1, n2 = _dft_factors(n)
    j = np.arange(n1, dtype=np.float64)
    ang1 = 2.0 * np.pi * np.outer(j, j) / n1
    c1 = np.cos(ang1) / math.sqrt(n1)
    s1 = np.sin(ang1) / math.sqrt(n1)
    w1 = np.block([[c1, -s1], [s1, c1]])
    k1 = np.arange(n1, dtype=np.float64)[:, None, None]
    k2 = np.arange(n2, dtype=np.float64)[None, :, None]
    m2 = np.arange(n2, dtype=np.float64)[None, None, :]
    ang2 = 2.0 * np.pi * m2 * (k1 + n1 * k2) / n
    m2cat = np.concatenate([np.cos(ang2), -np.sin(ang2)], axis=-1) / math.sqrt(n2)
    return w1.astype(np.float32), m2cat.astype(np.float32)


@functools.lru_cache(maxsize=None)
def _channel_dft_table():
    j = np.arange(FNET_GROUP_DIM, dtype=np.float64)
    ang = 2.0 * np.pi * np.outer(j, j) / FNET_GROUP_DIM
    groups = FNET_W // FNET_GROUP_DIM
    cg = np.kron(np.eye(groups), np.cos(ang)) / math.sqrt(FNET_GROUP_DIM)
    sg = np.kron(np.eye(groups), np.sin(ang)) / math.sqrt(FNET_GROUP_DIM)
    return np.concatenate([cg, sg], axis=1).astype(np.float32)


def _fourier_positions(a, b):
    bsz, n, c = a.shape
    n1, n2 = _dft_factors(n)
    w1_np, m2_np = _dft_tables(n)
    w1 = jnp.asarray(w1_np).astype(BF16)
    m2cat = jnp.asarray(m2_np).astype(BF16)
    cb = LANES
    blk = pl.BlockSpec((None, n, cb), lambda bi, j: (bi, 0, j))
    return pl.pallas_call(
        functools.partial(_dft_kernel, n1=n1, n2=n2),
        out_shape=jax.ShapeDtypeStruct((bsz, n, c), F32),
        grid=(bsz, c // cb),
        in_specs=[blk, blk,
                  pl.BlockSpec((2 * n1, 2 * n1), lambda bi, j: (0, 0)),
                  pl.BlockSpec((n1, n2, 2 * n2), lambda bi, j: (0, 0, 0))],
        out_specs=blk,
        scratch_shapes=[pltpu.VMEM((n, cb), F32)] * 2,
        compiler_params=_cparams(("parallel", "parallel")),
        name="fourier",
    )(a, b, w1, m2cat)


NEG_BIG = -1e30


def _attn_kernel(*refs, tk, n_lat_blocks):
    if n_lat_blocks:
        qt_ref, kc_ref, vtc_ref, kl_ref, vtl_ref, o_ref, acc_ref, sa_ref, sb_ref = refs
    else:
        qt_ref, kc_ref, vtc_ref, o_ref, acc_ref = refs
    qt = qt_ref[...]
    tq = qt.shape[1]

    def scores(kj):
        return jnp.dot(kj, qt, preferred_element_type=F32)

    def consume(s, vtj, m, l, first):
        m_new = jnp.maximum(m, jnp.max(s, axis=0, keepdims=True))
        alpha = jnp.exp2(m - m_new)
        p = jnp.exp2(s - m_new)
        l_new = alpha * l + jnp.sum(p, axis=0, keepdims=True)
        pv = jnp.dot(vtj, p.astype(BF16), preferred_element_type=F32)
        acc_ref[...] = pv if first else alpha * acc_ref[...] + pv
        return m_new, l_new

    def lat_keys(j):
        return kl_ref[pl.ds(pl.multiple_of(j * tk, tk), tk), :]

    m0 = jnp.full((1, tq), NEG_BIG, F32)
    l0 = jnp.zeros((1, tq), F32)
    if n_lat_blocks:
        sa_ref[...] = scores(lat_keys(0))
    m, l = consume(scores(kc_ref[...]), vtc_ref[...], m0, l0, True)
    if n_lat_blocks:
        def body(i, carry):
            sb_ref[...] = scores(lat_keys(2 * i + 1))
            m, l = consume(sa_ref[...], vtl_ref[2 * i], carry[0], carry[1], False)
            sa_ref[...] = scores(lat_keys(2 * i + 2))
            return consume(sb_ref[...], vtl_ref[2 * i + 1], m, l, False)

        m, l = lax.fori_loop(0, n_lat_blocks // 2 - 1, body, (m, l))
        sb_ref[...] = scores(lat_keys(n_lat_blocks - 1))
        m, l = consume(sa_ref[...], vtl_ref[n_lat_blocks - 2], m, l, False)
        m, l = consume(sb_ref[...], vtl_ref[n_lat_blocks - 1], m, l, False)
    o_ref[...] = jnp.transpose(acc_ref[...] / l).astype(o_ref.dtype)


def _attention(qt, k_ctx, vt_ctx, k_lat, vt_lat, tq, tk):
    bsz, heads, _, nq = qt.shape
    m_ctx = k_ctx.shape[2]
    n_lat_blocks = 0 if k_lat is None else k_lat.shape[2] // tk
    in_specs = [
        pl.BlockSpec((None, None, QK_PAD, tq), lambda b, h, i: (b, h, 0, i)),
        pl.BlockSpec((None, None, m_ctx, QK_PAD), lambda b, h, i: (b, h, 0, 0)),
        pl.BlockSpec((None, None, V_DIM, m_ctx), lambda b, h, i: (b, h, 0, 0)),
    ]
    args = [qt, k_ctx, vt_ctx]
    if n_lat_blocks:
        in_specs += [
            pl.BlockSpec((None, None, k_lat.shape[2], QK_PAD), lambda b, h, i: (b, h, 0, 0)),
            pl.BlockSpec((None, None, n_lat_blocks, V_DIM, tk), lambda b, h, i: (b, h, 0, 0, 0)),
        ]
        args += [k_lat, vt_lat]
    return pl.pallas_call(
        functools.partial(_attn_kernel, tk=tk, n_lat_blocks=n_lat_blocks),
        out_shape=jax.ShapeDtypeStruct((bsz, nq, heads * V_DIM), BF16),
        grid=(bsz, heads, nq // tq),
        in_specs=in_specs,
        out_specs=pl.BlockSpec((None, tq, V_DIM), lambda b, h, i: (b, i, h)),
        scratch_shapes=[pltpu.VMEM((V_DIM, tq), F32)] + [pltpu.VMEM((tk, tq), F32)] * (2 if n_lat_blocks else 0),
        compiler_params=_cparams(("parallel", "parallel", "arbitrary")),
        name="attention",
    )(*args)


def _mix_out_kernel(x_ref, ya_ref, yb_ref, yc_ref, wo_ref, g1_ref, sh_ref, sc_ref, wrh_ref, wrl_ref,
                    xn_ref, h2_ref, aff_ref, *, tm):
    rows_per_group = min(tm, MIX_OUT_ROWS)
    for g in range(tm // rows_per_group):
        _mix_out_rows(g * rows_per_group, rows_per_group, x_ref, ya_ref, yb_ref, yc_ref, wo_ref, g1_ref, sh_ref,
                      sc_ref, wrh_ref, wrl_ref, xn_ref, h2_ref, aff_ref)


def _mix_out_rows(r0, nr, x_ref, ya_ref, yb_ref, yc_ref, wo_ref, g1_ref, sh_ref, sc_ref, wrh_ref, wrl_ref,
                  xn_ref, h2_ref, aff_ref):
    rg = slice(r0, r0 + nr)
    mix = jnp.dot(ya_ref[rg, :], wo_ref[0:SGU_W, :], preferred_element_type=F32)
    mix += jnp.dot(yb_ref[rg, :].astype(BF16), wo_ref[SGU_W:SGU_W + FNET_W, :], preferred_element_type=F32)
    mix += jnp.dot(yc_ref[rg, :], wo_ref[SGU_W + FNET_W:, :], preferred_element_type=F32)
    xn = x_ref[rg, :] + g1_ref[...] * mix
    xn_ref[rg, :] = xn
    d = xn.shape[-1]
    h2 = xn * _rms_scale(xn, d) * (1.0 + sc_ref[...]) + sh_ref[...]
    for j in range(d // LANES):
        h2_ref[pl.ds(r0 * SUBLANES + j, nr, stride=SUBLANES), :] = h2[:, j * LANES:(j + 1) * LANES]
    h_hi = h2.astype(BF16)
    h_lo = (h2 - h_hi.astype(F32)).astype(BF16)
    nt_dims = (((1,), (1,)), ((), ()))
    logits = lax.dot_general(wrh_ref[...], h_hi, nt_dims, preferred_element_type=F32)
    logits += lax.dot_general(wrh_ref[...], h_lo, nt_dims, preferred_element_type=F32)
    logits += lax.dot_general(wrl_ref[...], h_hi, nt_dims, preferred_element_type=F32)
    e = jnp.exp(logits - jnp.max(logits, axis=0, keepdims=True))
    aff_ref[:, rg] = e / jnp.sum(e, axis=0, keepdims=True)


def _mix_out(x, ya, yb, yc, w_out, g1, sh2, sc2, wr_hi, wr_lo, tm):
    bsz, nt, d = x.shape
    vec = pl.BlockSpec((None, 1, d), lambda b, i: (b, 0, 0))
    full = lambda arr: pl.BlockSpec(arr.shape, lambda b, i: (0,) * arr.ndim)
    return pl.pallas_call(
        functools.partial(_mix_out_kernel, tm=tm),
        out_shape=(
            jax.ShapeDtypeStruct((bsz, nt, d), F32),
            jax.ShapeDtypeStruct((bsz, nt * SUBLANES, LANES), F32),
            jax.ShapeDtypeStruct((bsz, N_EXPERTS, nt), F32),
        ),
        grid=(bsz, nt // tm),
        in_specs=[
            pl.BlockSpec((None, tm, d), lambda b, i: (b, i, 0)),
            pl.BlockSpec((None, tm, SGU_W), lambda b, i: (b, i, 0)),
            pl.BlockSpec((None, tm, FNET_W), lambda b, i: (b, i, 0)),
            pl.BlockSpec((None, tm, MLA_HEADS * V_DIM), lambda b, i: (b, i, 0)),
            full(w_out), vec, vec, vec, full(wr_hi), full(wr_lo),
        ],
        out_specs=(
            pl.BlockSpec((None, tm, d), lambda b, i: (b, i, 0)),
            pl.BlockSpec((None, tm * SUBLANES, LANES), lambda b, i: (b, i, 0)),
            pl.BlockSpec((None, N_EXPERTS, tm), lambda b, i: (b, 0, i)),
        ),
        compiler_params=_cparams(("parallel", "parallel")),
        name="mix_out",
    )(x, ya, yb, yc, w_out, g1, sh2, sc2, wr_hi, wr_lo)


def _prefix_count(mask, tri):
    e, n = mask.shape
    w = tri.shape[0]
    carry = jnp.zeros((e, 1), F32)
    outs = []
    ends = []
    for c in range(n // w):
        local = jnp.dot(mask[:, c * w:(c + 1) * w].astype(BF16), tri, preferred_element_type=F32)
        outs.append(local + carry)
        carry = carry + local[:, w - 1:w]
        ends.append(carry)
    return jnp.concatenate(outs, axis=1), ends


ROUTE_CHUNK = 2 * LANES


def _route_select_kernel(aff_ref, key_ref, cend_ref, *, cap):
    n_exp, n = aff_ref.shape
    w = min(n, ROUTE_CHUNK)
    aff = aff_ref[...]

    def step(i, thr_bits):
        cand = thr_bits | jnp.left_shift(jnp.int32(1), 30 - i)
        cnt = jnp.sum((aff >= pltpu.bitcast(cand, F32)).astype(jnp.int32), axis=-1, keepdims=True)
        return jnp.where(cnt >= cap, cand, thr_bits)

    thr = pltpu.bitcast(lax.fori_loop(0, 31, step, jnp.zeros((n_exp, 1), jnp.int32)), F32)
    gt = aff > thr
    eq = aff == thr
    need = cap - jnp.sum(gt.astype(jnp.int32), axis=-1, keepdims=True)
    r = lax.broadcasted_iota(jnp.int32, (w, w), 0)
    c = lax.broadcasted_iota(jnp.int32, (w, w), 1)
    tri = (r <= c).astype(BF16)
    eq_rank, _ = _prefix_count(eq.astype(F32), tri)
    sel = jnp.logical_or(gt, jnp.logical_and(eq, eq_rank <= need.astype(F32)))
    cum, ends = _prefix_count(sel.astype(F32), tri)
    key_ref[...] = jnp.where(sel, cum, 0.0)
    lane = lax.broadcasted_iota(jnp.int32, (n_exp, LANES), 1)
    cend = jnp.zeros((n_exp, LANES), F32)
    for ci, end in enumerate(ends):
        cend = jnp.where(lane == ci, end, cend)
    cend_ref[...] = cend.astype(jnp.int32)


def _route_compact_kernel(cend_ref, key_ref, aff_ref, idx_ref, gate_ref, acc_i_ref, acc_g_ref, *, cap, jb):
    b = pl.program_id(0)
    e = pl.program_id(1)
    n_chunks, w = key_ref.shape
    n_sb = cap // jb
    base = (b * pl.num_programs(1) + e) * LANES
    shift = jb.bit_length() - 1
    acc_i_ref[...] = jnp.zeros_like(acc_i_ref)
    acc_g_ref[...] = jnp.zeros_like(acc_g_ref)

    def chunk(c, _):
        c_start = jnp.where(c > 0, cend_ref[base + jnp.maximum(c - 1, 0)], 0)
        c_end = cend_ref[base + c]

        @pl.when(c_end > c_start)
        def _any_selected():
            krow = key_ref[pl.ds(c, 1), :]
            arow = aff_ref[pl.ds(c, 1), :]
            tpos = (lax.broadcasted_iota(jnp.int32, (1, w), 1) + c * w).astype(F32)
            sb_lo = lax.shift_right_logical(c_start, shift)
            sb_hi = jnp.minimum(lax.shift_right_logical(c_end - 1, shift), n_sb - 1)

            def slot_block(sb, _):
                slot = (lax.broadcasted_iota(jnp.int32, (jb, 1), 0) + (sb * jb + 1)).astype(F32)
                hit = krow == slot
                pi = jnp.where(hit, tpos, 0.0)
                pg = jnp.where(hit, arow, 0.0)
                fi = pi[:, 0:LANES]
                fg = pg[:, 0:LANES]
                for k in range(1, w // LANES):
                    fi = fi + pi[:, k * LANES:(k + 1) * LANES]
                    fg = fg + pg[:, k * LANES:(k + 1) * LANES]
                acc_i_ref[sb] = acc_i_ref[sb] + fi
                acc_g_ref[sb] = acc_g_ref[sb] + fg
                return 0

            lax.fori_loop(sb_lo, sb_hi + 1, slot_block, 0)

        return 0

    lax.fori_loop(0, n_chunks, chunk, 0)
    for sb in range(n_sb):
        idx_ref[sb * jb:(sb + 1) * jb, :] = jnp.sum(acc_i_ref[sb], axis=-1, keepdims=True).astype(jnp.int32)
        gate_ref[sb * jb:(sb + 1) * jb, :] = jnp.sum(acc_g_ref[sb], axis=-1, keepdims=True)


def _route(aff_t, cap):
    bsz, n_exp, n = aff_t.shape
    jb = min(cap, LANES)
    w = min(n, ROUTE_CHUNK)
    key, cend = pl.pallas_call(
        functools.partial(_route_select_kernel, cap=cap),
        out_shape=(jax.ShapeDtypeStruct((bsz, n_exp, n), F32),
                   jax.ShapeDtypeStruct((bsz, n_exp, LANES), jnp.int32)),
        grid=(bsz,),
        in_specs=[pl.BlockSpec((None, n_exp, n), lambda b: (b, 0, 0))],
        out_specs=(pl.BlockSpec((None, n_exp, n), lambda b: (b, 0, 0)),
                   pl.BlockSpec((None, n_exp, LANES), lambda b: (b, 0, 0))),
        compiler_params=_cparams(("parallel",)),
        name="route_select",
    )(aff_t)
    chunked = pl.BlockSpec((None, None, n // w, w), lambda b, e, cend: (b, e, 0, 0))
    out_blk = pl.BlockSpec((None, None, cap, 1), lambda b, e, cend: (b, e, 0, 0))
    grid_spec = pltpu.PrefetchScalarGridSpec(
        num_scalar_prefetch=1,
        grid=(bsz, n_exp),
        in_specs=[chunked, chunked],
        out_specs=(out_blk, out_blk),
        scratch_shapes=[pltpu.VMEM((cap // jb, jb, LANES), F32)] * 2,
    )
    return pl.pallas_call(
        functools.partial(_route_compact_kernel, cap=cap, jb=jb),
        out_shape=(jax.ShapeDtypeStruct((bsz, n_exp, cap, 1), jnp.int32),
                   jax.ShapeDtypeStruct((bsz, n_exp, cap, 1), F32)),
        grid_spec=grid_spec,
        compiler_params=_cparams(("parallel", "parallel")),
        name="route_compact",
    )(cend.reshape(-1), key.reshape(bsz, n_exp, n // w, w), aff_t.reshape(bsz, n_exp, n // w, w))


GATHER_UNROLL = 8


def _gather_start(idx_ref, h_hbm, xs_ref, sem, slot, e, b, n_exp, cap):
    base = (b * n_exp + e) * cap
    unroll = math.gcd(cap, GATHER_UNROLL)

    def issue(g, _):
        for u in range(unroll):
            i = g * unroll + u
            tok = idx_ref[base + i]
            pltpu.make_async_copy(
                h_hbm.at[b, pl.ds(pl.multiple_of(tok * SUBLANES, SUBLANES), SUBLANES), :],
                xs_ref.at[slot, pl.ds(pl.multiple_of(i * SUBLANES, SUBLANES), SUBLANES), :],
                sem.at[slot]).start()
        return 0

    lax.fori_loop(0, cap // unroll, issue, 0)


def _expert_kernel(idx_ref, h_hbm, wg_ref, wu_ref, wd_ref, gate_ref, ys_ref, xs_ref, sem, *, cap, d):
    e = pl.program_id(0)
    b = pl.program_id(1)
    n_exp = pl.num_programs(0)
    n_b = pl.num_programs(1)
    step = e * n_b + b
    slot = lax.rem(step, 2)

    @pl.when(step == 0)
    def _first():
        _gather_start(idx_ref, h_hbm, xs_ref, sem, 0, e, b, n_exp, cap)

    nxt = step + 1

    @pl.when(nxt < n_exp * n_b)
    def _prefetch():
        _gather_start(idx_ref, h_hbm, xs_ref, sem, 1 - slot, lax.div(nxt, n_b), lax.rem(nxt, n_b), n_exp, cap)

    pltpu.make_async_copy(h_hbm.at[b, pl.ds(0, cap * SUBLANES), :], xs_ref.at[slot], sem.at[slot]).wait()

    x = jnp.concatenate(
        [xs_ref[slot, pl.ds(j, cap, stride=SUBLANES), :] for j in range(d // LANES)], axis=1).astype(BF16)
    hid = _silu(jnp.dot(x, wg_ref[...].astype(BF16), preferred_element_type=F32))
    hid = hid * jnp.dot(x, wu_ref[...].astype(BF16), preferred_element_type=F32)
    y = jnp.dot(hid.astype(BF16), wd_ref[...].astype(BF16), preferred_element_type=F32) * gate_ref[...]
    for j in range(d // LANES):
        ys_ref[pl.ds(j, cap, stride=SUBLANES), :] = y[:, j * LANES:(j + 1) * LANES]


def _experts(idx_flat, h2_tiles, wg, wu, wd, layer, gate, cap):
    bsz = h2_tiles.shape[0]
    _, n_exp, d, f = wg.shape
    grid_spec = pltpu.PrefetchScalarGridSpec(
        num_scalar_prefetch=1,
        grid=(n_exp, bsz),
        in_specs=[
            pl.BlockSpec(memory_space=pl.ANY),
            pl.BlockSpec((None, None, d, f), lambda e, b, idx: (layer, e, 0, 0)),
            pl.BlockSpec((None, None, d, f), lambda e, b, idx: (layer, e, 0, 0)),
            pl.BlockSpec((None, None, f, d), lambda e, b, idx: (layer, e, 0, 0)),
            pl.BlockSpec((None, None, cap, 1), lambda e, b, idx: (b, e, 0, 0)),
        ],
        out_specs=pl.BlockSpec((None, None, cap * SUBLANES, LANES), lambda e, b, idx: (b, e, 0, 0)),
        scratch_shapes=[pltpu.VMEM((2, cap * SUBLANES, LANES), F32), pltpu.SemaphoreType.DMA((2,))],
    )
    return pl.pallas_call(
        functools.partial(_expert_kernel, cap=cap, d=d),
        out_shape=jax.ShapeDtypeStruct((bsz, n_exp, cap * SUBLANES, LANES), F32),
        grid_spec=grid_spec,
        compiler_params=_cparams(("arbitrary", "arbitrary")),
        name="experts",
    )(idx_flat, h2_tiles, wg, wu, wd, gate)


def _combine_kernel(idx_ref, ys_ref, o_hbm, acc_ref, sem, *, cap, unroll):
    b = pl.program_id(0)
    e = pl.program_id(1)
    n_exp = pl.num_programs(1)
    base = (b * n_exp + e) * cap

    @pl.when(e == 0)
    def _zero():
        acc_ref[...] = jnp.zeros_like(acc_ref)

    def rows(g, _):
        i0 = g * unroll
        toks = [pl.multiple_of(idx_ref[base + i0 + u] * SUBLANES, SUBLANES) for u in range(unroll)]
        sums = [acc_ref[pl.ds(toks[u], SUBLANES), :]
                + ys_ref[pl.ds(pl.multiple_of((i0 + u) * SUBLANES, SUBLANES), SUBLANES), :]
                for u in range(unroll)]
        for u in range(unroll):
            acc_ref[pl.ds(toks[u], SUBLANES), :] = sums[u]
        return 0

    lax.fori_loop(0, cap // unroll, rows, 0)

    @pl.when(e == n_exp - 1)
    def _flush():
        cp = pltpu.make_async_copy(acc_ref, o_hbm.at[b], sem)
        cp.start()
        cp.wait()


def _combine(idx_flat, ys_tiles, nt, cap):
    bsz, n_exp = ys_tiles.shape[:2]
    grid_spec = pltpu.PrefetchScalarGridSpec(
        num_scalar_prefetch=1,
        grid=(bsz, n_exp),
        in_specs=[pl.BlockSpec((None, None, cap * SUBLANES, LANES), lambda b, e, idx: (b, e, 0, 0))],
        out_specs=pl.BlockSpec(memory_space=pl.ANY),
        scratch_shapes=[pltpu.VMEM((nt * SUBLANES, LANES), F32), pltpu.SemaphoreType.DMA(())],
    )
    return pl.pallas_call(
        functools.partial(_combine_kernel, cap=cap, unroll=4),
        out_shape=jax.ShapeDtypeStruct((bsz, nt * SUBLANES, LANES), F32),
        grid_spec=grid_spec,
        compiler_params=_cparams(("arbitrary", "arbitrary")),
        name="combine",
    )(idx_flat, ys_tiles)


def _residual_kernel(x_ref, m_ref, g_ref, o_ref, *, tm):
    d = x_ref.shape[-1]
    moe = jnp.concatenate(
        [m_ref[pl.ds(j, tm, stride=SUBLANES), :] for j in range(d // LANES)], axis=1)
    o_ref[...] = x_ref[...] + g_ref[...] * moe


def _residual(x, moe_tiles, g2, tm):
    bsz, nt, d = x.shape
    return pl.pallas_call(
        functools.partial(_residual_kernel, tm=tm),
        out_shape=jax.ShapeDtypeStruct((bsz, nt, d), F32),
        grid=(bsz, nt // tm),
        in_specs=[
            pl.BlockSpec((None, tm, d), lambda b, i: (b, i, 0)),
            pl.BlockSpec((None, tm * SUBLANES, LANES), lambda b, i: (b, i, 0)),
            pl.BlockSpec((None, 1, d), lambda b, i: (b, 0, 0)),
        ],
        out_specs=pl.BlockSpec((None, tm, d), lambda b, i: (b, i, 0)),
        compiler_params=_cparams(("parallel", "parallel")),
        name="residual",
    )(x, moe_tiles, g2)


def _prepare_layer(l, w_in, sgu_norm, w_sgu, b_sgu, q_lora_norm, w_uq, kv_lora_norm, w_ukv, q_norm, k_norm,
                   w_out, w_router, w_gate, w_up, w_down):
    w_in_l = w_in[l]
    kr_cols = w_in_l[:, OFF_KR:OFF_KR + QK_ROPE]
    w_in_ext = jnp.concatenate([w_in_l[:, :OFF_KR], kr_cols, kr_cols], axis=1).astype(BF16)
    uq = w_uq[l].reshape(Q_LORA, MLA_HEADS, QK_DIM)
    w_uq_p = jnp.concatenate(
        [uq[:, :, :QK_NOPE].reshape(Q_LORA, MLA_HEADS * QK_NOPE),
         uq[:, :, QK_NOPE:].reshape(Q_LORA, MLA_HEADS * QK_ROPE)], axis=1).astype(BF16)

    def norm_pad(v):
        return jnp.concatenate([v[:QK_NOPE], v[QK_NOPE:], v[QK_NOPE:]])[None, :]

    wr_t = w_router[l].T
    wr_hi = wr_t.astype(BF16)
    wr_lo = (wr_t - wr_hi.astype(F32)).astype(BF16)
    return dict(
        w_in=w_in_ext,
        sgu_norm=sgu_norm[l][None, :],
        w_sgu=jnp.concatenate([w_sgu[l][hh] for hh in range(SGU_HEADS)], axis=1).astype(BF16),
        b_sgu=jnp.repeat(b_sgu[l].T, SGU_HEAD_DIM, axis=1),
        fdft=jnp.asarray(_channel_dft_table()).astype(BF16),
        q_lora_norm=q_lora_norm[l][None, :],
        w_uq=w_uq_p,
        q_norm=norm_pad(q_norm[l]),
        kv_lora_norm=kv_lora_norm[l][None, :],
        w_ukv=w_ukv[l].astype(BF16),
        k_norm=norm_pad(k_norm[l]),
        w_out=w_out[l].astype(BF16),
        wr_hi=wr_hi, wr_lo=wr_lo,
        w_gate=w_gate, w_up=w_up, w_down=w_down, layer=l,
    )


def _rope_tables(n):
    pos = np.arange(n)
    half = QK_ROPE // 4
    freqs = ROPE_THETA ** (-np.arange(half, dtype=np.float64) / half)
    ang_r = (pos // GRID_WIDTH)[:, None] * freqs
    ang_c = (pos % GRID_WIDTH)[:, None] * freqs
    cos64 = np.concatenate([np.cos(ang_r), np.cos(ang_r), np.cos(ang_c), np.cos(ang_c)], axis=1)
    sin64 = np.concatenate([-np.sin(ang_r), np.sin(ang_r), -np.sin(ang_c), np.sin(ang_c)], axis=1)
    return (jnp.asarray(np.tile(cos64, (1, 2)), F32), jnp.asarray(np.tile(sin64, (1, 2)), F32))


def _token_tile(nt, pref):
    return pref if nt % pref == 0 else nt


def _moe(x_new, h2_tiles, aff_t, lw, g2, tm):
    bsz, nt, d = x_new.shape
    cap = EC_FACTOR * nt // N_EXPERTS
    idx, gate = _route(aff_t, cap)
    idx_flat = idx.reshape(-1)
    ys = _experts(idx_flat, h2_tiles, lw["w_gate"], lw["w_up"], lw["w_down"], lw["layer"], gate, cap)
    moe_tiles = _combine(idx_flat, ys, nt, cap)
    return _residual(x_new, moe_tiles, g2, tm)


def kernel(x, c, ctx, c_ctx, w_ada, b_ada, w_in, sgu_norm, w_sgu, b_sgu, q_lora_norm, w_uq, kv_lora_norm, w_ukv,
           q_norm, k_norm, w_out, w_router, w_gate, w_up, w_down):
    bsz, n, d = x.shape
    n_ctx = ctx.shape[1]
    depth = w_ada.shape[0]
    assert bsz + 1 <= SUBLANES

    cond = jnp.zeros((SUBLANES, d), F32).at[:bsz].set(c).at[bsz].set(c_ctx)
    mod = _ada_modulation(cond, w_ada, b_ada)
    rope_tabs = _rope_tables(n)
    tm = _token_tile(n, 512)
    tm_in = _token_tile(n, 512)
    tm_c = _token_tile(n_ctx, 256)

    xc = ctx
    for l in range(depth):
        last = l == depth - 1
        lw = _prepare_layer(l, w_in, sgu_norm, w_sgu, b_sgu, q_lora_norm, w_uq, kv_lora_norm, w_ukv,
                            q_norm, k_norm, w_out, w_router, w_gate, w_up, w_down)
        parts = [mod[l, :, i * d:(i + 1) * d] for i in range(6)]
        lat = [p[:bsz, None, :] for p in parts]
        cx = [jnp.broadcast_to(p[bsz][None, None, :], (bsz, 1, d)) for p in parts]

        ya_c, a_c, b_c, q_c, k_c, v_c = _mix_in(xc, cx[0], cx[1], lw, None, tm_c)
        ya, a, b, q, k, v = _mix_in(x, lat[0], lat[1], lw, rope_tabs, tm_in)
        yb = _fourier_positions(a, b)
        tk = _token_tile(n, ATTN_TK)
        vt_c = jnp.swapaxes(v_c, 2, 3)
        vt = jnp.swapaxes(v.reshape(bsz, MLA_HEADS, n // tk, tk, V_DIM), 3, 4)
        yc = _attention(jnp.swapaxes(q, 2, 3), k_c, vt_c, k, vt, tq=_token_tile(n, ATTN_TQ), tk=tk)
        x_new, h2_tiles, aff_t = _mix_out(x, ya, yb, yc, lw["w_out"], lat[2], lat[3], lat[4],
                                          lw["wr_hi"], lw["wr_lo"], tm)
        if not last:
            yb_c = _fourier_positions(a_c, b_c)
            yc_c = _attention(jnp.swapaxes(q_c, 2, 3), k_c, vt_c, None, None,
                              tq=_token_tile(n_ctx, ATTN_TQ), tk=n_ctx)
            xc_new, h2c_tiles, affc_t = _mix_out(xc, ya_c, yb_c, yc_c, lw["w_out"], cx[2], cx[3], cx[4],
                                                 lw["wr_hi"], lw["wr_lo"], tm_c)
            xc = _moe(xc_new, h2c_tiles, affc_t, lw, cx[5], tm_c)
        x = _moe(x_new, h2_tiles, aff_t, lw, lat[5], tm)
    return x
```

```python
import functools
import math

import numpy as np
import jax
import jax.numpy as jnp
from jax import lax
from jax.experimental import pallas as pl
from jax.experimental.pallas import tpu as pltpu

F32 = jnp.float32
BF16 = jnp.bfloat16

GRID_WIDTH = 64
SGU_HEADS = 4
SGU_HEAD_DIM = 64
SGU_W = SGU_HEADS * SGU_HEAD_DIM
CHUNK = 128
FNET_GROUP_DIM = 64
FNET_W = 256
MLA_HEADS = 4
QK_NOPE = 128
QK_ROPE = 64
QK_DIM = QK_NOPE + QK_ROPE
V_DIM = 128
Q_LORA = 256
KV_LORA = 128
OFF_V = 256
OFF_F = 512
OFF_Q = 768
OFF_KV = 1024
OFF_KR = 1152
N_EXPERTS = 16
EC_FACTOR = 2
ROPE_THETA = 10000.0
EPS = 1e-6

LANES = 128
SUBLANES = 8
VMEM_LIMIT_BYTES = 56 * 1024 * 1024

QK_PAD = 2 * LANES
ATTN_TQ = 1024
ATTN_TK = 512
LOG2_E = 1.4426950408889634
MIX_ROWS = 128
MIX_OUT_ROWS = 512


def _cparams(sem, vmem=None):
    return pltpu.CompilerParams(dimension_semantics=sem, vmem_limit_bytes=vmem or VMEM_LIMIT_BYTES)


def _rms_scale(v, width):
    return lax.rsqrt(jnp.sum(v * v, axis=-1, keepdims=True) * (1.0 / width) + EPS)


def _gelu_tanh(v):
    return 0.5 * v * (1.0 + jnp.tanh(0.7978845608028654 * (v + 0.044715 * v * v * v)))


def _silu(v):
    return v / (1.0 + jnp.exp(-v))


def _ada_kernel(cond_ref, w_ref, b_ref, o_ref):
    s = _silu(cond_ref[...])
    o_ref[...] = jnp.dot(s, w_ref[...], preferred_element_type=F32,
                         precision=lax.Precision.HIGHEST) + b_ref[...]


def _ada_modulation(cond, w_ada, b_ada):
    depth, d, six_d = w_ada.shape
    tn = 1536
    return pl.pallas_call(
        _ada_kernel,
        out_shape=jax.ShapeDtypeStruct((depth, SUBLANES, six_d), F32),
        grid=(depth, six_d // tn),
        in_specs=[
            pl.BlockSpec((SUBLANES, d), lambda l, j: (0, 0)),
            pl.BlockSpec((None, d, tn), lambda l, j: (l, 0, j)),
            pl.BlockSpec((None, 1, tn), lambda l, j: (l, 0, j)),
        ],
        out_specs=pl.BlockSpec((None, SUBLANES, tn), lambda l, j: (l, 0, j)),
        compiler_params=_cparams(("arbitrary", "arbitrary")),
        name="ada_modulation",
    )(cond, w_ada, b_ada.reshape(depth, 1, six_d))


def _swap16(v):
    lane = lax.broadcasted_iota(jnp.int32, v.shape, 1)
    up = pltpu.roll(v, LANES - 16, axis=1)
    dn = pltpu.roll(v, 16, axis=1)
    return jnp.where((lane % 32) < 16, up, dn)


def _mix_in_kernel(*refs, tm, use_rope, fuse_residual):
    rows_per_group = min(tm, MIX_ROWS)
    for g in range(tm // rows_per_group):
        _mix_in_rows(slice(g * rows_per_group, (g + 1) * rows_per_group), refs, use_rope, fuse_residual)


def _mix_in_rows(rg, refs, use_rope, fuse_residual):
    if fuse_residual:
        x_ref, moe_ref, g2_ref = refs[:3]
        xo_ref = refs[-1]
        refs = refs[:1] + refs[3:-1]
    (x_ref, sh_ref, sc_ref, win_ref, sgun_ref, wsgu_ref, bsgu_ref, fdft_ref, qln_ref, wuq_ref, qn_ref,
     kvn_ref, wukv_ref, kn_ref, cos_ref, sin_ref, ya_ref, a_ref, b_ref, q_ref, k_ref, v_ref) = refs
    x = x_ref[rg, :]
    if fuse_residual:
        nrows = rg.stop - rg.start
        moe = jnp.concatenate(
            [moe_ref[pl.ds(rg.start * SUBLANES + j, nrows, stride=SUBLANES), :]
             for j in range(x.shape[-1] // LANES)], axis=1)
        x = x + g2_ref[...] * moe
        xo_ref[rg, :] = x
    tm = x.shape[0]
    d = x.shape[-1]
    h = x * _rms_scale(x, d) * (1.0 + sc_ref[...]) + sh_ref[...]
    p = jnp.dot(h.astype(BF16), win_ref[...], preferred_element_type=F32)

    u = _gelu_tanh(p[:, 0:OFF_V])
    gv = _gelu_tanh(p[:, OFF_V:OFF_F])
    vn = gv * _rms_scale(gv, SGU_W) * sgun_ref[...]
    head_of_lane = lax.broadcasted_iota(jnp.int32, (CHUNK, SGU_W), 1) // SGU_HEAD_DIM
    for c in range(tm // CHUNK):
        rows = slice(c * CHUNK, (c + 1) * CHUNK)
        vc = vn[rows]
        vstack = jnp.concatenate(
            [jnp.where(head_of_lane == hh, vc, 0.0) for hh in range(SGU_HEADS)], axis=0).astype(BF16)
        z = jnp.dot(wsgu_ref[...], vstack, preferred_element_type=F32) + bsgu_ref[...]
        ya_ref[rg.start + c * CHUNK:rg.start + (c + 1) * CHUNK, :] = (u[rows] * z).astype(ya_ref.dtype)

    ab = jnp.dot(p[:, OFF_F:OFF_Q].astype(BF16), fdft_ref[...], preferred_element_type=F32)
    a_ref[rg, :] = ab[:, 0:FNET_W].astype(a_ref.dtype)
    b_ref[rg, :] = ab[:, FNET_W:2 * FNET_W].astype(b_ref.dtype)

    lane = lax.broadcasted_iota(jnp.int32, (1, LANES), 1)
    lo = lane < QK_ROPE
    if use_rope:
        cos = cos_ref[rg, :]
        sin = sin_ref[rg, :]

    def rope(t):
        return t * cos + _swap16(t) * sin if use_rope else t

    pq = p[:, OFF_Q:OFF_KV]
    cq = pq * _rms_scale(pq, Q_LORA) * qln_ref[...]
    q = jnp.dot(cq.astype(BF16), wuq_ref[...], preferred_element_type=F32)
    qnorm = qn_ref[...]
    for hh in range(MLA_HEADS):
        qn = q[:, LANES * hh:LANES * (hh + 1)]
        pair = hh // 2
        qr = q[:, MLA_HEADS * LANES + LANES * pair:MLA_HEADS * LANES + LANES * (pair + 1)]
        qr = jnp.where(lo if hh % 2 == 0 else jnp.logical_not(lo), qr, 0.0)
        ssq = jnp.sum(qn * qn, axis=-1, keepdims=True) + jnp.sum(qr * qr, axis=-1, keepdims=True)
        rinv = lax.rsqrt(ssq * (1.0 / QK_DIM) + EPS) * (QK_DIM ** -0.5 * LOG2_E)
        q_ref[hh, rg, 0:LANES] = (qn * rinv * qnorm[:, 0:LANES]).astype(q_ref.dtype)
        q_ref[hh, rg, LANES:QK_PAD] = rope(qr * rinv * qnorm[:, LANES:QK_PAD]).astype(q_ref.dtype)

    pkv = p[:, OFF_KV:OFF_KR]
    ckv = pkv * _rms_scale(pkv, KV_LORA) * kvn_ref[...]
    kv = jnp.dot(ckv.astype(BF16), wukv_ref[...], preferred_element_type=F32)
    krr = p[:, OFF_KR:OFF_KR + LANES]
    ssq_kr = 0.5 * jnp.sum(krr * krr, axis=-1, keepdims=True)
    knorm = kn_ref[...]
    rinvs = []
    for hh in range(MLA_HEADS):
        kn = kv[:, 2 * LANES * hh:2 * LANES * hh + LANES]
        ssq = jnp.sum(kn * kn, axis=-1, keepdims=True) + ssq_kr
        rinv = lax.rsqrt(ssq * (1.0 / QK_DIM) + EPS)
        rinvs.append(rinv)
        k_ref[hh, rg, 0:LANES] = (kn * rinv * knorm[:, 0:LANES]).astype(k_ref.dtype)
        v_ref[hh, rg, :] = kv[:, 2 * LANES * hh + LANES:2 * LANES * (hh + 1)].astype(v_ref.dtype)
    for pair in range(MLA_HEADS // 2):
        r2 = jnp.where(lo, rinvs[2 * pair], rinvs[2 * pair + 1])
        kr = rope(krr * r2 * knorm[:, LANES:QK_PAD]).astype(k_ref.dtype)
        k_ref[2 * pair, rg, LANES:QK_PAD] = kr
        k_ref[2 * pair + 1, rg, LANES:QK_PAD] = kr


def _mix_in(x, sh, sc, lw, rope_tabs, tm, pending=None):
    bsz, nt, d = x.shape
    use_rope = rope_tabs is not None
    if use_rope:
        cos_t, sin_t = rope_tabs
    else:
        cos_t = sin_t = jnp.zeros((nt, LANES), F32)
    full = lambda arr: pl.BlockSpec(arr.shape, lambda b, i: (0,) * arr.ndim)
    vec = pl.BlockSpec((None, 1, d), lambda b, i: (b, 0, 0))
    in_w = lw["w_in"].shape[1]
    fuse = pending is not None
    kern = functools.partial(_mix_in_kernel, tm=tm, use_rope=use_rope, fuse_residual=fuse)
    row_blk = pl.BlockSpec((None, tm, d), lambda b, i: (b, i, 0))
    out_shape = [
        jax.ShapeDtypeStruct((bsz, nt, SGU_W), BF16),
        jax.ShapeDtypeStruct((bsz, nt, FNET_W), F32),
        jax.ShapeDtypeStruct((bsz, nt, FNET_W), F32),
        jax.ShapeDtypeStruct((bsz, MLA_HEADS, nt, QK_PAD), BF16),
        jax.ShapeDtypeStruct((bsz, MLA_HEADS, nt, QK_PAD), BF16),
        jax.ShapeDtypeStruct((bsz, MLA_HEADS, nt, V_DIM), BF16),
    ]
    out_specs = [
        pl.BlockSpec((None, tm, SGU_W), lambda b, i: (b, i, 0)),
        pl.BlockSpec((None, tm, FNET_W), lambda b, i: (b, i, 0)),
        pl.BlockSpec((None, tm, FNET_W), lambda b, i: (b, i, 0)),
        pl.BlockSpec((None, MLA_HEADS, tm, QK_PAD), lambda b, i: (b, 0, i, 0)),
        pl.BlockSpec((None, MLA_HEADS, tm, QK_PAD), lambda b, i: (b, 0, i, 0)),
        pl.BlockSpec((None, MLA_HEADS, tm, V_DIM), lambda b, i: (b, 0, i, 0)),
    ]
    in_specs = [row_blk]
    args = [x]
    if fuse:
        in_specs += [pl.BlockSpec((None, tm * SUBLANES, LANES), lambda b, i: (b, i, 0)), vec]
        args += list(pending)
        out_shape.append(jax.ShapeDtypeStruct((bsz, nt, d), F32))
        out_specs.append(row_blk)
    in_specs += [
        vec, vec,
        full(lw["w_in"]), full(lw["sgu_norm"]), full(lw["w_sgu"]), full(lw["b_sgu"]), full(lw["fdft"]),
        full(lw["q_lora_norm"]), full(lw["w_uq"]), full(lw["q_norm"]),
        full(lw["kv_lora_norm"]), full(lw["w_ukv"]), full(lw["k_norm"]),
        pl.BlockSpec((tm, LANES), lambda b, i: (i, 0)),
        pl.BlockSpec((tm, LANES), lambda b, i: (i, 0)),
    ]
    args += [sh, sc, lw["w_in"], lw["sgu_norm"], lw["w_sgu"], lw["b_sgu"], lw["fdft"],
             lw["q_lora_norm"], lw["w_uq"], lw["q_norm"], lw["kv_lora_norm"], lw["w_ukv"], lw["k_norm"],
             cos_t, sin_t]
    return pl.pallas_call(
        kern,
        out_shape=tuple(out_shape),
        grid=(bsz, nt // tm),
        in_specs=in_specs,
        out_specs=tuple(out_specs),
        compiler_params=_cparams(("parallel", "parallel")),
        name="mix_in",
    )(*args)


DFT_UNROLL = 16


def _dft_kernel(a_ref, b_ref, w_ref, m_ref, o_ref, yr_ref, yi_ref, *, n1, n2):
    w1 = w_ref[...]
    c = a_ref.shape[-1]

    u1 = math.gcd(n2, DFT_UNROLL)
    u2 = math.gcd(n1, DFT_UNROLL)

    def step1(g, _):
        cols = []
        for u in range(u1):
            rows = pl.ds(g * u1 + u, n1, stride=n2)
            cols.append(jnp.concatenate([a_ref[rows, :], b_ref[rows, :]], axis=0))
        r = jnp.dot(w1, jnp.concatenate(cols, axis=1).astype(BF16), preferred_element_type=F32)
        for u in range(u1):
            out_rows = pl.ds(pl.multiple_of((g * u1 + u) * n1, n1), n1)
            yr_ref[out_rows, :] = r[0:n1, u * c:(u + 1) * c]
            yi_ref[out_rows, :] = r[n1:2 * n1, u * c:(u + 1) * c]
        return 0

    lax.fori_loop(0, n2 // u1, step1, 0)

    def step2(g, _):
        for u in range(u2):
            k1 = g * u2 + u
            rows = pl.ds(k1, n2, stride=n1)
            ycat = jnp.concatenate([yr_ref[rows, :], yi_ref[rows, :]], axis=0).astype(BF16)
            o_ref[rows, :] = jnp.dot(m_ref[k1], ycat, preferred_element_type=F32)
        return 0

    lax.fori_loop(0, n1 // u2, step2, 0)


def _dft_factors(n):
    n1 = 64 if n >= 1024 else 32
    return n1, n // n1


@functools.lru_cache(maxsize=None)
def _dft_tables(n):
    n1, n2 = _dft_factors(n)
    j = np.arange(n1, dtype=np.float64)
    ang1 = 2.0 * np.pi * np.outer(j, j) / n1
    c1 = np.cos(ang1) / math.sqrt(n1)
    s1 = np.sin(ang1) / math.sqrt(n1)
    w1 = np.block([[c1, -s1], [s1, c1]])
    k1 = np.arange(n1, dtype=np.float64)[:, None, None]
    k2 = np.arange(n2, dtype=np.float64)[None, :, None]
    m2 = np.arange(n2, dtype=np.float64)[None, None, :]
    ang2 = 2.0 * np.pi * m2 * (k1 + n1 * k2) / n
    m2cat = np.concatenate([np.cos(ang2), -np.sin(ang2)], axis=-1) / math.sqrt(n2)
    return w1.astype(np.float32), m2cat.astype(np.float32)


@functools.lru_cache(maxsize=None)
def _channel_dft_table():
    j = np.arange(FNET_GROUP_DIM, dtype=np.float64)
    ang = 2.0 * np.pi * np.outer(j, j) / FNET_GROUP_DIM
    groups = FNET_W // FNET_GROUP_DIM
    cg = np.kron(np.eye(groups), np.cos(ang)) / math.sqrt(FNET_GROUP_DIM)
    sg = np.kron(np.eye(groups), np.sin(ang)) / math.sqrt(FNET_GROUP_DIM)
    return np.concatenate([cg, sg], axis=1).astype(np.float32)


def _fourier_positions(a, b):
    bsz, n, c = a.shape
    n1, n2 = _dft_factors(n)
    w1_np, m2_np = _dft_tables(n)
    w1 = jnp.asarray(w1_np).astype(BF16)
    m2cat = jnp.asarray(m2_np).astype(BF16)
    cb = LANES
    blk = pl.BlockSpec((None, n, cb), lambda bi, j: (bi, 0, j))
    return pl.pallas_call(
        functools.partial(_dft_kernel, n1=n1, n2=n2),
        out_shape=jax.ShapeDtypeStruct((bsz, n, c), F32),
        grid=(bsz, c // cb),
        in_specs=[blk, blk,
                  pl.BlockSpec((2 * n1, 2 * n1), lambda bi, j: (0, 0)),
                  pl.BlockSpec((n1, n2, 2 * n2), lambda bi, j: (0, 0, 0))],
        out_specs=blk,
        scratch_shapes=[pltpu.VMEM((n, cb), F32)] * 2,
        compiler_params=_cparams(("parallel", "parallel")),
        name="fourier",
    )(a, b, w1, m2cat)


NEG_BIG = -1e30


def _attn_kernel(*refs, tk, n_lat_blocks):
    if n_lat_blocks:
        qt_ref, kc_ref, vtc_ref, kl_ref, vtl_ref, o_ref, acc_ref, sa_ref, sb_ref = refs
    else:
        qt_ref, kc_ref, vtc_ref, o_ref, acc_ref = refs
    qt = qt_ref[...]
    tq = qt.shape[1]

    def scores(kj):
        return jnp.dot(kj, qt, preferred_element_type=F32)

    def consume(s, vtj, m, l, first):
        m_new = jnp.maximum(m, jnp.max(s, axis=0, keepdims=True))
        alpha = jnp.exp2(m - m_new)
        p = jnp.exp2(s - m_new)
        l_new = alpha * l + jnp.sum(p, axis=0, keepdims=True)
        pv = jnp.dot(vtj, p.astype(BF16), preferred_element_type=F32)
        acc_ref[...] = pv if first else alpha * acc_ref[...] + pv
        return m_new, l_new

    def lat_keys(j):
        return kl_ref[pl.ds(pl.multiple_of(j * tk, tk), tk), :]

    m0 = jnp.full((1, tq), NEG_BIG, F32)
    l0 = jnp.zeros((1, tq), F32)
    if n_lat_blocks:
        sa_ref[...] = scores(lat_keys(0))
    m, l = consume(scores(kc_ref[...]), vtc_ref[...], m0, l0, True)
    if n_lat_blocks:
        def body(i, carry):
            sb_ref[...] = scores(lat_keys(2 * i + 1))
            m, l = consume(sa_ref[...], vtl_ref[2 * i], carry[0], carry[1], False)
            sa_ref[...] = scores(lat_keys(2 * i + 2))
            return consume(sb_ref[...], vtl_ref[2 * i + 1], m, l, False)

        m, l = lax.fori_loop(0, n_lat_blocks // 2 - 1, body, (m, l))
        sb_ref[...] = scores(lat_keys(n_lat_blocks - 1))
        m, l = consume(sa_ref[...], vtl_ref[n_lat_blocks - 2], m, l, False)
        m, l = consume(sb_ref[...], vtl_ref[n_lat_blocks - 1], m, l, False)
    o_ref[...] = jnp.transpose(acc_ref[...] / l).astype(o_ref.dtype)


def _attention(qt, k_ctx, vt_ctx, k_lat, vt_lat, tq, tk):
    bsz, heads, _, nq = qt.shape
    m_ctx = k_ctx.shape[2]
    n_lat_blocks = 0 if k_lat is None else k_lat.shape[2] // tk
    in_specs = [
        pl.BlockSpec((None, None, QK_PAD, tq), lambda b, h, i: (b, h, 0, i)),
        pl.BlockSpec((None, None, m_ctx, QK_PAD), lambda b, h, i: (b, h, 0, 0)),
        pl.BlockSpec((None, None, V_DIM, m_ctx), lambda b, h, i: (b, h, 0, 0)),
    ]
    args = [qt, k_ctx, vt_ctx]
    if n_lat_blocks:
        in_specs += [
            pl.BlockSpec((None, None, k_lat.shape[2], QK_PAD), lambda b, h, i: (b, h, 0, 0)),
            pl.BlockSpec((None, None, n_lat_blocks, V_DIM, tk), lambda b, h, i: (b, h, 0, 0, 0)),
        ]
        args += [k_lat, vt_lat]
    return pl.pallas_call(
        functools.partial(_attn_kernel, tk=tk, n_lat_blocks=n_lat_blocks),
        out_shape=jax.ShapeDtypeStruct((bsz, nq, heads * V_DIM), BF16),
        grid=(bsz, heads, nq // tq),
        in_specs=in_specs,
        out_specs=pl.BlockSpec((None, tq, V_DIM), lambda b, h, i: (b, i, h)),
        scratch_shapes=[pltpu.VMEM((V_DIM, tq), F32)] + [pltpu.VMEM((tk, tq), F32)] * (2 if n_lat_blocks else 0),
        compiler_params=_cparams(("parallel", "parallel", "arbitrary")),
        name="attention",
    )(*args)


def _mix_out_kernel(x_ref, ya_ref, yb_ref, yc_ref, wo_ref, g1_ref, sh_ref, sc_ref, wrh_ref, wrl_ref,
                    xn_ref, h2_ref, aff_ref, *, tm):
    rows_per_group = min(tm, MIX_OUT_ROWS)
    for g in range(tm // rows_per_group):
        _mix_out_rows(g * rows_per_group, rows_per_group, x_ref, ya_ref, yb_ref, yc_ref, wo_ref, g1_ref, sh_ref,
                      sc_ref, wrh_ref, wrl_ref, xn_ref, h2_ref, aff_ref)


def _mix_out_rows(r0, nr, x_ref, ya_ref, yb_ref, yc_ref, wo_ref, g1_ref, sh_ref, sc_ref, wrh_ref, wrl_ref,
                  xn_ref, h2_ref, aff_ref):
    rg = slice(r0, r0 + nr)
    mix = jnp.dot(ya_ref[rg, :], wo_ref[0:SGU_W, :], preferred_element_type=F32)
    mix += jnp.dot(yb_ref[rg, :].astype(BF16), wo_ref[SGU_W:SGU_W + FNET_W, :], preferred_element_type=F32)
    mix += jnp.dot(yc_ref[rg, :], wo_ref[SGU_W + FNET_W:, :], preferred_element_type=F32)
    xn = x_ref[rg, :] + g1_ref[...] * mix
    xn_ref[rg, :] = xn
    d = xn.shape[-1]
    h2 = xn * _rms_scale(xn, d) * (1.0 + sc_ref[...]) + sh_ref[...]
    for j in range(d // LANES):
        h2_ref[pl.ds(r0 * SUBLANES + j, nr, stride=SUBLANES), :] = h2[:, j * LANES:(j + 1) * LANES]
    h_hi = h2.astype(BF16)
    h_lo = (h2 - h_hi.astype(F32)).astype(BF16)
    nt_dims = (((1,), (1,)), ((), ()))
    logits = lax.dot_general(wrh_ref[...], h_hi, nt_dims, preferred_element_type=F32)
    logits += lax.dot_general(wrh_ref[...], h_lo, nt_dims, preferred_element_type=F32)
    logits += lax.dot_general(wrl_ref[...], h_hi, nt_dims, preferred_element_type=F32)
    e = jnp.exp(logits - jnp.max(logits, axis=0, keepdims=True))
    aff_ref[:, rg] = e / jnp.sum(e, axis=0, keepdims=True)


def _mix_out(x, ya, yb, yc, w_out, g1, sh2, sc2, wr_hi, wr_lo, tm):
    bsz, nt, d = x.shape
    vec = pl.BlockSpec((None, 1, d), lambda b, i: (b, 0, 0))
    full = lambda arr: pl.BlockSpec(arr.shape, lambda b, i: (0,) * arr.ndim)
    return pl.pallas_call(
        functools.partial(_mix_out_kernel, tm=tm),
        out_shape=(
            jax.ShapeDtypeStruct((bsz, nt, d), F32),
            jax.ShapeDtypeStruct((bsz, nt * SUBLANES, LANES), F32),
            jax.ShapeDtypeStruct((bsz, N_EXPERTS, nt), F32),
        ),
        grid=(bsz, nt // tm),
        in_specs=[
            pl.BlockSpec((None, tm, d), lambda b, i: (b, i, 0)),
            pl.BlockSpec((None, tm, SGU_W), lambda b, i: (b, i, 0)),
            pl.BlockSpec((None, tm, FNET_W), lambda b, i: (b, i, 0)),
            pl.BlockSpec((None, tm, MLA_HEADS * V_DIM), lambda b, i: (b, i, 0)),
            full(w_out), vec, vec, vec, full(wr_hi), full(wr_lo),
        ],
        out_specs=(
            pl.BlockSpec((None, tm, d), lambda b, i: (b, i, 0)),
            pl.BlockSpec((None, tm * SUBLANES, LANES), lambda b, i: (b, i, 0)),
            pl.BlockSpec((None, N_EXPERTS, tm), lambda b, i: (b, 0, i)),
        ),
        compiler_params=_cparams(("parallel", "parallel")),
        name="mix_out",
    )(x, ya, yb, yc, w_out, g1, sh2, sc2, wr_hi, wr_lo)


def _prefix_count(mask, tri):
    e, n = mask.shape
    w = tri.shape[0]
    carry = jnp.zeros((e, 1), F32)
    outs = []
    ends = []
    for c in range(n // w):
        local = jnp.dot(mask[:, c * w:(c + 1) * w].astype(BF16), tri, preferred_element_type=F32)
        outs.append(local + carry)
        carry = carry + local[:, w - 1:w]
        ends.append(carry)
    return jnp.concatenate(outs, axis=1), ends


ROUTE_CHUNK = 2 * LANES


def _route_select_kernel(aff_ref, key_ref, cend_ref, *, cap):
    n_exp, n = aff_ref.shape
    w = min(n, ROUTE_CHUNK)
    aff = aff_ref[...]

    def step(i, thr_bits):
        cand = thr_bits | jnp.left_shift(jnp.int32(1), 30 - i)
        cnt = jnp.sum((aff >= pltpu.bitcast(cand, F32)).astype(jnp.int32), axis=-1, keepdims=True)
        return jnp.where(cnt >= cap, cand, thr_bits)

    thr = pltpu.bitcast(lax.fori_loop(0, 31, step, jnp.zeros((n_exp, 1), jnp.int32)), F32)
    gt = aff > thr
    eq = aff == thr
    need = cap - jnp.sum(gt.astype(jnp.int32), axis=-1, keepdims=True)
    r = lax.broadcasted_iota(jnp.int32, (w, w), 0)
    c = lax.broadcasted_iota(jnp.int32, (w, w), 1)
    tri = (r <= c).astype(BF16)
    eq_rank, _ = _prefix_count(eq.astype(F32), tri)
    sel = jnp.logical_or(gt, jnp.logical_and(eq, eq_rank <= need.astype(F32)))
    cum, ends = _prefix_count(sel.astype(F32), tri)
    key_ref[...] = jnp.where(sel, cum, 0.0)
    lane = lax.broadcasted_iota(jnp.int32, (n_exp, LANES), 1)
    cend = jnp.zeros((n_exp, LANES), F32)
    for ci, end in enumerate(ends):
        cend = jnp.where(lane == ci, end, cend)
    cend_ref[...] = cend.astype(jnp.int32)


def _route_compact_kernel(cend_ref, key_ref, aff_ref, idx_ref, gate_ref, acc_i_ref, acc_g_ref, *, cap, jb):
    b = pl.program_id(0)
    e = pl.program_id(1)
    n_chunks, w = key_ref.shape
    n_sb = cap // jb
    base = (b * pl.num_programs(1) + e) * LANES
    shift = jb.bit_length() - 1
    acc_i_ref[...] = jnp.zeros_like(acc_i_ref)
    acc_g_ref[...] = jnp.zeros_like(acc_g_ref)

    def chunk(c, _):
        c_start = jnp.where(c > 0, cend_ref[base + jnp.maximum(c - 1, 0)], 0)
        c_end = cend_ref[base + c]

        @pl.when(c_end > c_start)
        def _any_selected():
            krow = key_ref[pl.ds(c, 1), :]
            arow = aff_ref[pl.ds(c, 1), :]
            tpos = (lax.broadcasted_iota(jnp.int32, (1, w), 1) + c * w).astype(F32)
            sb_lo = lax.shift_right_logical(c_start, shift)
            sb_hi = jnp.minimum(lax.shift_right_logical(c_end - 1, shift), n_sb - 1)

            def slot_block(sb, _):
                slot = (lax.broadcasted_iota(jnp.int32, (jb, 1), 0) + (sb * jb + 1)).astype(F32)
                hit = krow == slot
                pi = jnp.where(hit, tpos, 0.0)
                pg = jnp.where(hit, arow, 0.0)
                fi = pi[:, 0:LANES]
                fg = pg[:, 0:LANES]
                for k in range(1, w // LANES):
                    fi = fi + pi[:, k * LANES:(k + 1) * LANES]
                    fg = fg + pg[:, k * LANES:(k + 1) * LANES]
                acc_i_ref[sb] = acc_i_ref[sb] + fi
                acc_g_ref[sb] = acc_g_ref[sb] + fg
                return 0

            lax.fori_loop(sb_lo, sb_hi + 1, slot_block, 0)

        return 0

    lax.fori_loop(0, n_chunks, chunk, 0)
    for sb in range(n_sb):
        idx_ref[sb * jb:(sb + 1) * jb, :] = jnp.sum(acc_i_ref[sb], axis=-1, keepdims=True).astype(jnp.int32)
        gate_ref[sb * jb:(sb + 1) * jb, :] = jnp.sum(acc_g_ref[sb], axis=-1, keepdims=True)


def _route(aff_t, cap):
    bsz, n_exp, n = aff_t.shape
    jb = min(cap, LANES)
    w = min(n, ROUTE_CHUNK)
    key, cend = pl.pallas_call(
        functools.partial(_route_select_kernel, cap=cap),
        out_shape=(jax.ShapeDtypeStruct((bsz, n_exp, n), F32),
                   jax.ShapeDtypeStruct((bsz, n_exp, LANES), jnp.int32)),
        grid=(bsz,),
        in_specs=[pl.BlockSpec((None, n_exp, n), lambda b: (b, 0, 0))],
        out_specs=(pl.BlockSpec((None, n_exp, n), lambda b: (b, 0, 0)),
                   pl.BlockSpec((None, n_exp, LANES), lambda b: (b, 0, 0))),
        compiler_params=_cparams(("parallel",)),
        name="route_select",
    )(aff_t)
    chunked = pl.BlockSpec((None, None, n // w, w), lambda b, e, cend: (b, e, 0, 0))
    out_blk = pl.BlockSpec((None, None, cap, 1), lambda b, e, cend: (b, e, 0, 0))
    grid_spec = pltpu.PrefetchScalarGridSpec(
        num_scalar_prefetch=1,
        grid=(bsz, n_exp),
        in_specs=[chunked, chunked],
        out_specs=(out_blk, out_blk),
        scratch_shapes=[pltpu.VMEM((cap // jb, jb, LANES), F32)] * 2,
    )
    return pl.pallas_call(
        functools.partial(_route_compact_kernel, cap=cap, jb=jb),
        out_shape=(jax.ShapeDtypeStruct((bsz, n_exp, cap, 1), jnp.int32),
                   jax.ShapeDtypeStruct((bsz, n_exp, cap, 1), F32)),
        grid_spec=grid_spec,
        compiler_params=_cparams(("parallel", "parallel")),
        name="route_compact",
    )(cend.reshape(-1), key.reshape(bsz, n_exp, n // w, w), aff_t.reshape(bsz, n_exp, n // w, w))


GATHER_UNROLL = 8


def _gather_start(idx_refs, h_refs, caps, xs_ref, sem, slot, e, b, n_exp):
    row0 = 0
    for idx_ref, h_hbm, cap in zip(idx_refs, h_refs, caps):
        base = (b * n_exp + e) * cap
        unroll = math.gcd(cap, GATHER_UNROLL)

        def issue(g, _, idx_ref=idx_ref, h_hbm=h_hbm, base=base, unroll=unroll, row0=row0):
            for u in range(unroll):
                i = g * unroll + u
                tok = idx_ref[base + i]
                pltpu.make_async_copy(
                    h_hbm.at[b, pl.ds(pl.multiple_of(tok * SUBLANES, SUBLANES), SUBLANES), :],
                    xs_ref.at[slot, pl.ds(pl.multiple_of((row0 + i) * SUBLANES, SUBLANES), SUBLANES), :],
                    sem.at[slot]).start()
            return 0

        lax.fori_loop(0, cap // unroll, issue, 0)
        row0 += cap


def _expert_kernel(*refs, caps, d):
    n_sets = len(caps)
    idx_refs = refs[:n_sets]
    h_refs = refs[n_sets:2 * n_sets]
    wg_ref, wu_ref, wd_ref = refs[2 * n_sets:2 * n_sets + 3]
    gate_refs = refs[2 * n_sets + 3:3 * n_sets + 3]
    ys_refs = refs[3 * n_sets + 3:4 * n_sets + 3]
    xs_ref, sem = refs[4 * n_sets + 3:]
    rows = sum(caps)
    e = pl.program_id(0)
    b = pl.program_id(1)
    n_exp = pl.num_programs(0)
    n_b = pl.num_programs(1)
    step = e * n_b + b
    slot = lax.rem(step, 2)

    @pl.when(step == 0)
    def _first():
        _gather_start(idx_refs, h_refs, caps, xs_ref, sem, 0, e, b, n_exp)

    nxt = step + 1

    @pl.when(nxt < n_exp * n_b)
    def _prefetch():
        _gather_start(idx_refs, h_refs, caps, xs_ref, sem, 1 - slot, lax.div(nxt, n_b), lax.rem(nxt, n_b), n_exp)

    pltpu.make_async_copy(h_refs[0].at[b, pl.ds(0, rows * SUBLANES), :], xs_ref.at[slot], sem.at[slot]).wait()

    x = jnp.concatenate(
        [xs_ref[slot, pl.ds(j, rows, stride=SUBLANES), :] for j in range(d // LANES)], axis=1).astype(BF16)
    hid = _silu(jnp.dot(x, wg_ref[...].astype(BF16), preferred_element_type=F32))
    hid = hid * jnp.dot(x, wu_ref[...].astype(BF16), preferred_element_type=F32)
    gate = gate_refs[0][...] if n_sets == 1 else jnp.concatenate([g[...] for g in gate_refs], axis=0)
    y = jnp.dot(hid.astype(BF16), wd_ref[...].astype(BF16), preferred_element_type=F32) * gate
    row0 = 0
    for ys_ref, cap in zip(ys_refs, caps):
        for j in range(d // LANES):
            ys_ref[pl.ds(j, cap, stride=SUBLANES), :] = y[row0:row0 + cap, j * LANES:(j + 1) * LANES]
        row0 += cap


def _experts(token_sets, wg, wu, wd, layer):
    caps = tuple(ts[3] for ts in token_sets)
    n_sets = len(token_sets)
    bsz = token_sets[0][1].shape[0]
    _, n_exp, d, f = wg.shape
    assert token_sets[0][1].shape[1] >= sum(caps) * SUBLANES
    wspec = lambda rows, cols: pl.BlockSpec((None, None, rows, cols), lambda e, b, *idx: (layer, e, 0, 0))
    grid_spec = pltpu.PrefetchScalarGridSpec(
        num_scalar_prefetch=n_sets,
        grid=(n_exp, bsz),
        in_specs=[pl.BlockSpec(memory_space=pl.ANY)] * n_sets + [wspec(d, f), wspec(d, f), wspec(f, d)] + [
            pl.BlockSpec((None, None, cap, 1), lambda e, b, *idx: (b, e, 0, 0)) for cap in caps],
        out_specs=tuple(pl.BlockSpec((None, None, cap * SUBLANES, LANES), lambda e, b, *idx: (b, e, 0, 0))
                        for cap in caps),
        scratch_shapes=[pltpu.VMEM((2, sum(caps) * SUBLANES, LANES), F32), pltpu.SemaphoreType.DMA((2,))],
    )
    return pl.pallas_call(
        functools.partial(_expert_kernel, caps=caps, d=d),
        out_shape=tuple(jax.ShapeDtypeStruct((bsz, n_exp, cap * SUBLANES, LANES), F32) for cap in caps),
        grid_spec=grid_spec,
        compiler_params=_cparams(("arbitrary", "arbitrary")),
        name="experts",
    )(*[ts[0] for ts in token_sets], *[ts[1] for ts in token_sets], wg, wu, wd, *[ts[2] for ts in token_sets])


def _combine_kernel(idx_ref, ys_ref, o_hbm, acc_ref, sem, *, cap, unroll):
    b = pl.program_id(0)
    e = pl.program_id(1)
    n_exp = pl.num_programs(1)
    base = (b * n_exp + e) * cap

    @pl.when(e == 0)
    def _zero():
        acc_ref[...] = jnp.zeros_like(acc_ref)

    def rows(g, _):
        i0 = g * unroll
        toks = [pl.multiple_of(idx_ref[base + i0 + u] * SUBLANES, SUBLANES) for u in range(unroll)]
        sums = [acc_ref[pl.ds(toks[u], SUBLANES), :]
                + ys_ref[pl.ds(pl.multiple_of((i0 + u) * SUBLANES, SUBLANES), SUBLANES), :]
                for u in range(unroll)]
        for u in range(unroll):
            acc_ref[pl.ds(toks[u], SUBLANES), :] = sums[u]
        return 0

    lax.fori_loop(0, cap // unroll, rows, 0)

    @pl.when(e == n_exp - 1)
    def _flush():
        cp = pltpu.make_async_copy(acc_ref, o_hbm.at[b], sem)
        cp.start()
        cp.wait()


def _combine(idx_flat, ys_tiles, nt, cap):
    bsz, n_exp = ys_tiles.shape[:2]
    grid_spec = pltpu.PrefetchScalarGridSpec(
        num_scalar_prefetch=1,
        grid=(bsz, n_exp),
        in_specs=[pl.BlockSpec((None, None, cap * SUBLANES, LANES), lambda b, e, idx: (b, e, 0, 0))],
        out_specs=pl.BlockSpec(memory_space=pl.ANY),
        scratch_shapes=[pltpu.VMEM((nt * SUBLANES, LANES), F32), pltpu.SemaphoreType.DMA(())],
    )
    return pl.pallas_call(
        functools.partial(_combine_kernel, cap=cap, unroll=4),
        out_shape=jax.ShapeDtypeStruct((bsz, nt * SUBLANES, LANES), F32),
        grid_spec=grid_spec,
        compiler_params=_cparams(("arbitrary", "arbitrary")),
        name="combine",
    )(idx_flat, ys_tiles)


def _residual_kernel(x_ref, m_ref, g_ref, o_ref, *, tm):
    d = x_ref.shape[-1]
    moe = jnp.concatenate(
        [m_ref[pl.ds(j, tm, stride=SUBLANES), :] for j in range(d // LANES)], axis=1)
    o_ref[...] = x_ref[...] + g_ref[...] * moe


def _residual(x, moe_tiles, g2, tm):
    bsz, nt, d = x.shape
    return pl.pallas_call(
        functools.partial(_residual_kernel, tm=tm),
        out_shape=jax.ShapeDtypeStruct((bsz, nt, d), F32),
        grid=(bsz, nt // tm),
        in_specs=[
            pl.BlockSpec((None, tm, d), lambda b, i: (b, i, 0)),
            pl.BlockSpec((None, tm * SUBLANES, LANES), lambda b, i: (b, i, 0)),
            pl.BlockSpec((None, 1, d), lambda b, i: (b, 0, 0)),
        ],
        out_specs=pl.BlockSpec((None, tm, d), lambda b, i: (b, i, 0)),
        compiler_params=_cparams(("parallel", "parallel")),
        name="residual",
    )(x, moe_tiles, g2)


def _prepare_layer(l, w_in, sgu_norm, w_sgu, b_sgu, q_lora_norm, w_uq, kv_lora_norm, w_ukv, q_norm, k_norm,
                   w_out, w_router, w_gate, w_up, w_down):
    w_in_l = w_in[l]
    kr_cols = w_in_l[:, OFF_KR:OFF_KR + QK_ROPE]
    w_in_ext = jnp.concatenate([w_in_l[:, :OFF_KR], kr_cols, kr_cols], axis=1).astype(BF16)
    uq = w_uq[l].reshape(Q_LORA, MLA_HEADS, QK_DIM)
    w_uq_p = jnp.concatenate(
        [uq[:, :, :QK_NOPE].reshape(Q_LORA, MLA_HEADS * QK_NOPE),
         uq[:, :, QK_NOPE:].reshape(Q_LORA, MLA_HEADS * QK_ROPE)], axis=1).astype(BF16)

    def norm_pad(v):
        return jnp.concatenate([v[:QK_NOPE], v[QK_NOPE:], v[QK_NOPE:]])[None, :]

    wr_t = w_router[l].T
    wr_hi = wr_t.astype(BF16)
    wr_lo = (wr_t - wr_hi.astype(F32)).astype(BF16)
    return dict(
        w_in=w_in_ext,
        sgu_norm=sgu_norm[l][None, :],
        w_sgu=jnp.concatenate([w_sgu[l][hh] for hh in range(SGU_HEADS)], axis=1).astype(BF16),
        b_sgu=jnp.repeat(b_sgu[l].T, SGU_HEAD_DIM, axis=1),
        fdft=jnp.asarray(_channel_dft_table()).astype(BF16),
        q_lora_norm=q_lora_norm[l][None, :],
        w_uq=w_uq_p,
        q_norm=norm_pad(q_norm[l]),
        kv_lora_norm=kv_lora_norm[l][None, :],
        w_ukv=w_ukv[l].astype(BF16),
        k_norm=norm_pad(k_norm[l]),
        w_out=w_out[l].astype(BF16),
        wr_hi=wr_hi, wr_lo=wr_lo,
        w_gate=w_gate, w_up=w_up, w_down=w_down, layer=l,
    )


def _rope_tables(n):
    pos = np.arange(n)
    half = QK_ROPE // 4
    freqs = ROPE_THETA ** (-np.arange(half, dtype=np.float64) / half)
    ang_r = (pos // GRID_WIDTH)[:, None] * freqs
    ang_c = (pos % GRID_WIDTH)[:, None] * freqs
    cos64 = np.concatenate([np.cos(ang_r), np.cos(ang_r), np.cos(ang_c), np.cos(ang_c)], axis=1)
    sin64 = np.concatenate([-np.sin(ang_r), np.sin(ang_r), -np.sin(ang_c), np.sin(ang_c)], axis=1)
    return (jnp.asarray(np.tile(cos64, (1, 2)), F32), jnp.asarray(np.tile(sin64, (1, 2)), F32))


def _token_tile(nt, pref):
    return pref if nt % pref == 0 else nt


def _moe(populations, lw):
    token_sets = []
    for h2_tiles, aff_t in populations:
        nt = aff_t.shape[-1]
        cap = EC_FACTOR * nt // N_EXPERTS
        idx, gate = _route(aff_t, cap)
        token_sets.append((idx.reshape(-1), h2_tiles, gate, cap))
    ys = _experts(token_sets, lw["w_gate"], lw["w_up"], lw["w_down"], lw["layer"])
    return [_combine(ts[0], y, pop[1].shape[-1], ts[3]) for ts, y, pop in zip(token_sets, ys, populations)]


def kernel(x, c, ctx, c_ctx, w_ada, b_ada, w_in, sgu_norm, w_sgu, b_sgu, q_lora_norm, w_uq, kv_lora_norm, w_ukv,
           q_norm, k_norm, w_out, w_router, w_gate, w_up, w_down):
    bsz, n, d = x.shape
    n_ctx = ctx.shape[1]
    depth = w_ada.shape[0]
    assert bsz + 1 <= SUBLANES

    cond = jnp.zeros((SUBLANES, d), F32).at[:bsz].set(c).at[bsz].set(c_ctx)
    mod = _ada_modulation(cond, w_ada, b_ada)
    rope_tabs = _rope_tables(n)
    tm = _token_tile(n, 512)
    tm_in = _token_tile(n, 1024)
    tm_c = _token_tile(n_ctx, 256)

    xc = ctx
    pending = pending_c = None
    for l in range(depth):
        last = l == depth - 1
        lw = _prepare_layer(l, w_in, sgu_norm, w_sgu, b_sgu, q_lora_norm, w_uq, kv_lora_norm, w_ukv,
                            q_norm, k_norm, w_out, w_router, w_gate, w_up, w_down)
        parts = [mod[l, :, i * d:(i + 1) * d] for i in range(6)]
        lat = [p[:bsz, None, :] for p in parts]
        cx = [jnp.broadcast_to(p[bsz][None, None, :], (bsz, 1, d)) for p in parts]

        outs_c = _mix_in(xc, cx[0], cx[1], lw, None, tm_c, pending_c)
        outs = _mix_in(x, lat[0], lat[1], lw, rope_tabs, tm_in, pending)
        ya_c, a_c, b_c, q_c, k_c, v_c = outs_c[:6]
        ya, a, b, q, k, v = outs[:6]
        if pending is not None:
            xc, x = outs_c[6], outs[6]
        yb = _fourier_positions(a, b)
        tk = _token_tile(n, ATTN_TK)
        vt_c = jnp.swapaxes(v_c, 2, 3)
        vt = jnp.swapaxes(v.reshape(bsz, MLA_HEADS, n // tk, tk, V_DIM), 3, 4)
        yc = _attention(jnp.swapaxes(q, 2, 3), k_c, vt_c, k, vt, tq=_token_tile(n, ATTN_TQ), tk=tk)
        x_new, h2_tiles, aff_t = _mix_out(x, ya, yb, yc, lw["w_out"], lat[2], lat[3], lat[4],
                                          lw["wr_hi"], lw["wr_lo"], tm)
        if not last:
            yb_c = _fourier_positions(a_c, b_c)
            yc_c = _attention(jnp.swapaxes(q_c, 2, 3), k_c, vt_c, None, None,
                              tq=_token_tile(n_ctx, ATTN_TQ), tk=n_ctx)
            xc_new, h2c_tiles, affc_t = _mix_out(xc, ya_c, yb_c, yc_c, lw["w_out"], cx[2], cx[3], cx[4],
                                                 lw["wr_hi"], lw["wr_lo"], tm_c)
            moe, moe_c = _moe([(h2_tiles, aff_t), (h2c_tiles, affc_t)], lw)
            xc, pending_c = xc_new, (moe_c, cx[5])
        else:
            moe, = _moe([(h2_tiles, aff_t)], lw)
        x, pending = x_new, (moe, lat[5])
    return _residual(x, pending[0], pending[1], tm)
```

```python
import functools
import math

import numpy as np
import jax
import jax.numpy as jnp
from jax import lax
from jax.experimental import pallas as pl
from jax.experimental.pallas import tpu as pltpu

F32 = jnp.float32
BF16 = jnp.bfloat16

GRID_WIDTH = 64
SGU_HEADS = 4
SGU_HEAD_DIM = 64
SGU_W = SGU_HEADS * SGU_HEAD_DIM
CHUNK = 128
FNET_GROUP_DIM = 64
FNET_W = 256
MLA_HEADS = 4
QK_NOPE = 128
QK_ROPE = 64
QK_DIM = QK_NOPE + QK_ROPE
V_DIM = 128
Q_LORA = 256
KV_LORA = 128
OFF_V = 256
OFF_F = 512
OFF_Q = 768
OFF_KV = 1024
OFF_KR = 1152
N_EXPERTS = 16
EC_FACTOR = 2
ROPE_THETA = 10000.0
EPS = 1e-6

LANES = 128
SUBLANES = 8
VMEM_LIMIT_BYTES = 56 * 1024 * 1024

QK_PAD = 2 * LANES
ATTN_TQ = 2048
ATTN_TK = 512
LOG2_E = 1.4426950408889634
MIX_ROWS = 128
MIX_OUT_ROWS = 512


def _cparams(sem, vmem=None):
    return pltpu.CompilerParams(dimension_semantics=sem, vmem_limit_bytes=vmem or VMEM_LIMIT_BYTES)


def _rms_scale(v, width):
    return lax.rsqrt(jnp.sum(v * v, axis=-1, keepdims=True) * (1.0 / width) + EPS)


def _gelu_tanh(v):
    return 0.5 * v * (1.0 + jnp.tanh(0.7978845608028654 * (v + 0.044715 * v * v * v)))


def _silu(v):
    return v / (1.0 + jnp.exp(-v))


def _ada_kernel(cond_ref, w_ref, b_ref, o_ref):
    s = _silu(cond_ref[...])
    o_ref[...] = jnp.dot(s, w_ref[...], preferred_element_type=F32,
                         precision=lax.Precision.HIGHEST) + b_ref[...]


def _ada_modulation(cond, w_ada, b_ada):
    depth, d, six_d = w_ada.shape
    tn = 1536
    return pl.pallas_call(
        _ada_kernel,
        out_shape=jax.ShapeDtypeStruct((depth, SUBLANES, six_d), F32),
        grid=(depth, six_d // tn),
        in_specs=[
            pl.BlockSpec((SUBLANES, d), lambda l, j: (0, 0)),
            pl.BlockSpec((None, d, tn), lambda l, j: (l, 0, j)),
            pl.BlockSpec((None, 1, tn), lambda l, j: (l, 0, j)),
        ],
        out_specs=pl.BlockSpec((None, SUBLANES, tn), lambda l, j: (l, 0, j)),
        compiler_params=_cparams(("arbitrary", "arbitrary")),
        name="ada_modulation",
    )(cond, w_ada, b_ada.reshape(depth, 1, six_d))


def _swap16(v):
    lane = lax.broadcasted_iota(jnp.int32, v.shape, 1)
    up = pltpu.roll(v, LANES - 16, axis=1)
    dn = pltpu.roll(v, 16, axis=1)
    return jnp.where((lane % 32) < 16, up, dn)


def _mix_in_kernel(*refs, tm, use_rope, fuse_residual):
    rows_per_group = min(tm, MIX_ROWS)
    for g in range(tm // rows_per_group):
        _mix_in_rows(slice(g * rows_per_group, (g + 1) * rows_per_group), refs, use_rope, fuse_residual)


def _mix_in_rows(rg, refs, use_rope, fuse_residual):
    if fuse_residual:
        x_ref, moe_ref, g2_ref = refs[:3]
        xo_ref = refs[-1]
        refs = refs[:1] + refs[3:-1]
    (x_ref, sh_ref, sc_ref, win_ref, sgun_ref, wsgu_ref, bsgu_ref, fdft_ref, qln_ref, wuq_ref, qn_ref,
     kvn_ref, wukv_ref, kn_ref, cos_ref, sin_ref, ya_ref, a_ref, b_ref, q_ref, k_ref, v_ref) = refs
    x = x_ref[rg, :]
    if fuse_residual:
        nrows = rg.stop - rg.start
        moe = jnp.concatenate(
            [moe_ref[pl.ds(rg.start * SUBLANES + j, nrows, stride=SUBLANES), :]
             for j in range(x.shape[-1] // LANES)], axis=1)
        x = x + g2_ref[...] * moe
        xo_ref[rg, :] = x
    tm = x.shape[0]
    d = x.shape[-1]
    h = x * _rms_scale(x, d) * (1.0 + sc_ref[...]) + sh_ref[...]
    p = jnp.dot(h.astype(BF16), win_ref[...], preferred_element_type=F32)

    u = _gelu_tanh(p[:, 0:OFF_V])
    gv = _gelu_tanh(p[:, OFF_V:OFF_F])
    vn = gv * _rms_scale(gv, SGU_W) * sgun_ref[...]
    head_of_lane = lax.broadcasted_iota(jnp.int32, (CHUNK, SGU_W), 1) // SGU_HEAD_DIM
    for c in range(tm // CHUNK):
        rows = slice(c * CHUNK, (c + 1) * CHUNK)
        vc = vn[rows]
        vstack = jnp.concatenate(
            [jnp.where(head_of_lane == hh, vc, 0.0) for hh in range(SGU_HEADS)], axis=0).astype(BF16)
        z = jnp.dot(wsgu_ref[...], vstack, preferred_element_type=F32) + bsgu_ref[...]
        ya_ref[rg.start + c * CHUNK:rg.start + (c + 1) * CHUNK, :] = (u[rows] * z).astype(ya_ref.dtype)

    ab = jnp.dot(p[:, OFF_F:OFF_Q].astype(BF16), fdft_ref[...], preferred_element_type=F32)
    a_ref[rg, :] = ab[:, 0:FNET_W].astype(a_ref.dtype)
    b_ref[rg, :] = ab[:, FNET_W:2 * FNET_W].astype(b_ref.dtype)

    lane = lax.broadcasted_iota(jnp.int32, (1, LANES), 1)
    lo = lane < QK_ROPE
    if use_rope:
        cos = cos_ref[rg, :]
        sin = sin_ref[rg, :]

    def rope(t):
        return t * cos + _swap16(t) * sin if use_rope else t

    pq = p[:, OFF_Q:OFF_KV]
    cq = pq * _rms_scale(pq, Q_LORA) * qln_ref[...]
    q = jnp.dot(cq.astype(BF16), wuq_ref[...], preferred_element_type=F32)
    qnorm = qn_ref[...]
    for hh in range(MLA_HEADS):
        qn = q[:, LANES * hh:LANES * (hh + 1)]
        pair = hh // 2
        qr = q[:, MLA_HEADS * LANES + LANES * pair:MLA_HEADS * LANES + LANES * (pair + 1)]
        qr = jnp.where(lo if hh % 2 == 0 else jnp.logical_not(lo), qr, 0.0)
        ssq = jnp.sum(qn * qn, axis=-1, keepdims=True) + jnp.sum(qr * qr, axis=-1, keepdims=True)
        rinv = lax.rsqrt(ssq * (1.0 / QK_DIM) + EPS) * (QK_DIM ** -0.5 * LOG2_E)
        q_ref[hh, rg, 0:LANES] = (qn * rinv * qnorm[:, 0:LANES]).astype(q_ref.dtype)
        q_ref[hh, rg, LANES:QK_PAD] = rope(qr * rinv * qnorm[:, LANES:QK_PAD]).astype(q_ref.dtype)

    pkv = p[:, OFF_KV:OFF_KR]
    ckv = pkv * _rms_scale(pkv, KV_LORA) * kvn_ref[...]
    kv = jnp.dot(ckv.astype(BF16), wukv_ref[...], preferred_element_type=F32)
    krr = p[:, OFF_KR:OFF_KR + LANES]
    ssq_kr = 0.5 * jnp.sum(krr * krr, axis=-1, keepdims=True)
    knorm = kn_ref[...]
    rinvs = []
    for hh in range(MLA_HEADS):
        kn = kv[:, 2 * LANES * hh:2 * LANES * hh + LANES]
        ssq = jnp.sum(kn * kn, axis=-1, keepdims=True) + ssq_kr
        rinv = lax.rsqrt(ssq * (1.0 / QK_DIM) + EPS)
        rinvs.append(rinv)
        k_ref[hh, rg, 0:LANES] = (kn * rinv * knorm[:, 0:LANES]).astype(k_ref.dtype)
        v_ref[hh, rg, :] = kv[:, 2 * LANES * hh + LANES:2 * LANES * (hh + 1)].astype(v_ref.dtype)
    for pair in range(MLA_HEADS // 2):
        r2 = jnp.where(lo, rinvs[2 * pair], rinvs[2 * pair + 1])
        kr = rope(krr * r2 * knorm[:, LANES:QK_PAD]).astype(k_ref.dtype)
        k_ref[2 * pair, rg, LANES:QK_PAD] = kr
        k_ref[2 * pair + 1, rg, LANES:QK_PAD] = kr


def _mix_in(x, sh, sc, lw, rope_tabs, tm, pending=None):
    bsz, nt, d = x.shape
    use_rope = rope_tabs is not None
    if use_rope:
        cos_t, sin_t = rope_tabs
    else:
        cos_t = sin_t = jnp.zeros((nt, LANES), F32)
    full = lambda arr: pl.BlockSpec(arr.shape, lambda b, i: (0,) * arr.ndim)
    vec = pl.BlockSpec((None, 1, d), lambda b, i: (b, 0, 0))
    in_w = lw["w_in"].shape[1]
    fuse = pending is not None
    kern = functools.partial(_mix_in_kernel, tm=tm, use_rope=use_rope, fuse_residual=fuse)
    row_blk = pl.BlockSpec((None, tm, d), lambda b, i: (b, i, 0))
    out_shape = [
        jax.ShapeDtypeStruct((bsz, nt, SGU_W), BF16),
        jax.ShapeDtypeStruct((bsz, nt, FNET_W), F32),
        jax.ShapeDtypeStruct((bsz, nt, FNET_W), F32),
        jax.ShapeDtypeStruct((bsz, MLA_HEADS, nt, QK_PAD), BF16),
        jax.ShapeDtypeStruct((bsz, MLA_HEADS, nt, QK_PAD), BF16),
        jax.ShapeDtypeStruct((bsz, MLA_HEADS, nt, V_DIM), BF16),
    ]
    out_specs = [
        pl.BlockSpec((None, tm, SGU_W), lambda b, i: (b, i, 0)),
        pl.BlockSpec((None, tm, FNET_W), lambda b, i: (b, i, 0)),
        pl.BlockSpec((None, tm, FNET_W), lambda b, i: (b, i, 0)),
        pl.BlockSpec((None, MLA_HEADS, tm, QK_PAD), lambda b, i: (b, 0, i, 0)),
        pl.BlockSpec((None, MLA_HEADS, tm, QK_PAD), lambda b, i: (b, 0, i, 0)),
        pl.BlockSpec((None, MLA_HEADS, tm, V_DIM), lambda b, i: (b, 0, i, 0)),
    ]
    in_specs = [row_blk]
    args = [x]
    if fuse:
        in_specs += [pl.BlockSpec((None, tm * SUBLANES, LANES), lambda b, i: (b, i, 0)), vec]
        args += list(pending)
        out_shape.append(jax.ShapeDtypeStruct((bsz, nt, d), F32))
        out_specs.append(row_blk)
    in_specs += [
        vec, vec,
        full(lw["w_in"]), full(lw["sgu_norm"]), full(lw["w_sgu"]), full(lw["b_sgu"]), full(lw["fdft"]),
        full(lw["q_lora_norm"]), full(lw["w_uq"]), full(lw["q_norm"]),
        full(lw["kv_lora_norm"]), full(lw["w_ukv"]), full(lw["k_norm"]),
        pl.BlockSpec((tm, LANES), lambda b, i: (i, 0)),
        pl.BlockSpec((tm, LANES), lambda b, i: (i, 0)),
    ]
    args += [sh, sc, lw["w_in"], lw["sgu_norm"], lw["w_sgu"], lw["b_sgu"], lw["fdft"],
             lw["q_lora_norm"], lw["w_uq"], lw["q_norm"], lw["kv_lora_norm"], lw["w_ukv"], lw["k_norm"],
             cos_t, sin_t]
    return pl.pallas_call(
        kern,
        out_shape=tuple(out_shape),
        grid=(bsz, nt // tm),
        in_specs=in_specs,
        out_specs=tuple(out_specs),
        compiler_params=_cparams(("parallel", "parallel")),
        name="mix_in",
    )(*args)


DFT_UNROLL = 16


def _dft_kernel(a_ref, b_ref, w_ref, m_ref, o_ref, yr_ref, yi_ref, *, n1, n2):
    w1 = w_ref[...]
    c = a_ref.shape[-1]

    u1 = math.gcd(n2, DFT_UNROLL)
    u2 = math.gcd(n1, DFT_UNROLL)

    def step1(g, _):
        cols = []
        for u in range(u1):
            rows = pl.ds(g * u1 + u, n1, stride=n2)
            cols.append(jnp.concatenate([a_ref[rows, :], b_ref[rows, :]], axis=0))
        r = jnp.dot(w1, jnp.concatenate(cols, axis=1).astype(BF16), preferred_element_type=F32)
        for u in range(u1):
            out_rows = pl.ds(pl.multiple_of((g * u1 + u) * n1, n1), n1)
            yr_ref[out_rows, :] = r[0:n1, u * c:(u + 1) * c]
            yi_ref[out_rows, :] = r[n1:2 * n1, u * c:(u + 1) * c]
        return 0

    lax.fori_loop(0, n2 // u1, step1, 0)

    def step2(g, _):
        for u in range(u2):
            k1 = g * u2 + u
            rows = pl.ds(k1, n2, stride=n1)
            ycat = jnp.concatenate([yr_ref[rows, :], yi_ref[rows, :]], axis=0).astype(BF16)
            o_ref[rows, :] = jnp.dot(m_ref[k1], ycat, preferred_element_type=F32)
        return 0

    lax.fori_loop(0, n1 // u2, step2, 0)


def _dft_factors(n):
    n1 = 64 if n >= 1024 else 32
    return n1, n // n1


@functools.lru_cache(maxsize=None)
def _dft_tables(n):
    n1, n2 = _dft_factors(n)
    j = np.arange(n1, dtype=np.float64)
    ang1 = 2.0 * np.pi * np.outer(j, j) / n1
    c1 = np.cos(ang1) / math.sqrt(n1)
    s1 = np.sin(ang1) / math.sqrt(n1)
    w1 = np.block([[c1, -s1], [s1, c1]])
    k1 = np.arange(n1, dtype=np.float64)[:, None, None]
    k2 = np.arange(n2, dtype=np.float64)[None, :, None]
    m2 = np.arange(n2, dtype=np.float64)[None, None, :]
    ang2 = 2.0 * np.pi * m2 * (k1 + n1 * k2) / n
    m2cat = np.concatenate([np.cos(ang2), -np.sin(ang2)], axis=-1) / math.sqrt(n2)
    return w1.astype(np.float32), m2cat.astype(np.float32)


@functools.lru_cache(maxsize=None)
def _channel_dft_table():
    j = np.arange(FNET_GROUP_DIM, dtype=np.float64)
    ang = 2.0 * np.pi * np.outer(j, j) / FNET_GROUP_DIM
    groups = FNET_W // FNET_GROUP_DIM
    cg = np.kron(np.eye(groups), np.cos(ang)) / math.sqrt(FNET_GROUP_DIM)
    sg = np.kron(np.eye(groups), np.sin(ang)) / math.sqrt(FNET_GROUP_DIM)
    return np.concatenate([cg, sg], axis=1).astype(np.float32)


def _fourier_positions(a, b):
    bsz, n, c = a.shape
    n1, n2 = _dft_factors(n)
    w1_np, m2_np = _dft_tables(n)
    w1 = jnp.asarray(w1_np).astype(BF16)
    m2cat = jnp.asarray(m2_np).astype(BF16)
    cb = LANES
    blk = pl.BlockSpec((None, n, cb), lambda bi, j: (bi, 0, j))
    return pl.pallas_call(
        functools.partial(_dft_kernel, n1=n1, n2=n2),
        out_shape=jax.ShapeDtypeStruct((bsz, n, c), F32),
        grid=(bsz, c // cb),
        in_specs=[blk, blk,
                  pl.BlockSpec((2 * n1, 2 * n1), lambda bi, j: (0, 0)),
                  pl.BlockSpec((n1, n2, 2 * n2), lambda bi, j: (0, 0, 0))],
        out_specs=blk,
        scratch_shapes=[pltpu.VMEM((n, cb), F32)] * 2,
        compiler_params=_cparams(("parallel", "parallel")),
        name="fourier",
    )(a, b, w1, m2cat)


NEG_BIG = -1e30


def _attn_kernel(*refs, tk, n_lat_blocks):
    if n_lat_blocks:
        qt_ref, kc_ref, vtc_ref, kl_ref, vtl_ref, o_ref, acc_ref, sa_ref, sb_ref = refs
    else:
        qt_ref, kc_ref, vtc_ref, o_ref, acc_ref = refs
    qt = qt_ref[...]
    tq = qt.shape[1]

    def scores(kj):
        return jnp.dot(kj, qt, preferred_element_type=F32)

    def consume(s, vtj, m, l, first):
        m_new = jnp.maximum(m, jnp.max(s, axis=0, keepdims=True))
        alpha = jnp.exp2(m - m_new)
        p = jnp.exp2(s - m_new)
        l_new = alpha * l + jnp.sum(p, axis=0, keepdims=True)
        pv = jnp.dot(vtj, p.astype(BF16), preferred_element_type=F32)
        acc_ref[...] = pv if first else alpha * acc_ref[...] + pv
        return m_new, l_new

    def lat_keys(j):
        return kl_ref[pl.ds(pl.multiple_of(j * tk, tk), tk), :]

    m0 = jnp.full((1, tq), NEG_BIG, F32)
    l0 = jnp.zeros((1, tq), F32)
    if n_lat_blocks:
        sa_ref[...] = scores(lat_keys(0))
    m, l = consume(scores(kc_ref[...]), vtc_ref[...], m0, l0, True)
    if n_lat_blocks:
        def body(i, carry):
            sb_ref[...] = scores(lat_keys(2 * i + 1))
            m, l = consume(sa_ref[...], vtl_ref[2 * i], carry[0], carry[1], False)
            sa_ref[...] = scores(lat_keys(2 * i + 2))
            return consume(sb_ref[...], vtl_ref[2 * i + 1], m, l, False)

        m, l = lax.fori_loop(0, n_lat_blocks // 2 - 1, body, (m, l))
        sb_ref[...] = scores(lat_keys(n_lat_blocks - 1))
        m, l = consume(sa_ref[...], vtl_ref[n_lat_blocks - 2], m, l, False)
        m, l = consume(sb_ref[...], vtl_ref[n_lat_blocks - 1], m, l, False)
    o_ref[...] = jnp.transpose(acc_ref[...] / l).astype(o_ref.dtype)


def _attention(qt, k_ctx, vt_ctx, k_lat, vt_lat, tq, tk):
    bsz, heads, _, nq = qt.shape
    m_ctx = k_ctx.shape[2]
    n_lat_blocks = 0 if k_lat is None else k_lat.shape[2] // tk
    in_specs = [
        pl.BlockSpec((None, None, QK_PAD, tq), lambda b, h, i: (b, h, 0, i)),
        pl.BlockSpec((None, None, m_ctx, QK_PAD), lambda b, h, i: (b, h, 0, 0)),
        pl.BlockSpec((None, None, V_DIM, m_ctx), lambda b, h, i: (b, h, 0, 0)),
    ]
    args = [qt, k_ctx, vt_ctx]
    if n_lat_blocks:
        in_specs += [
            pl.BlockSpec((None, None, k_lat.shape[2], QK_PAD), lambda b, h, i: (b, h, 0, 0)),
            pl.BlockSpec((None, None, n_lat_blocks, V_DIM, tk), lambda b, h, i: (b, h, 0, 0, 0)),
        ]
        args += [k_lat, vt_lat]
    return pl.pallas_call(
        functools.partial(_attn_kernel, tk=tk, n_lat_blocks=n_lat_blocks),
        out_shape=jax.ShapeDtypeStruct((bsz, nq, heads * V_DIM), BF16),
        grid=(bsz, heads, nq // tq),
        in_specs=in_specs,
        out_specs=pl.BlockSpec((None, tq, V_DIM), lambda b, h, i: (b, i, h)),
        scratch_shapes=[pltpu.VMEM((V_DIM, tq), F32)] + [pltpu.VMEM((tk, tq), F32)] * (2 if n_lat_blocks else 0),
        compiler_params=_cparams(("parallel", "parallel", "arbitrary")),
        name="attention",
    )(*args)


def _mix_out_kernel(x_ref, ya_ref, yb_ref, yc_ref, wo_ref, g1_ref, sh_ref, sc_ref, wrh_ref, wrl_ref,
                    xn_ref, h2_ref, aff_ref, *, tm):
    rows_per_group = min(tm, MIX_OUT_ROWS)
    for g in range(tm // rows_per_group):
        _mix_out_rows(g * rows_per_group, rows_per_group, x_ref, ya_ref, yb_ref, yc_ref, wo_ref, g1_ref, sh_ref,
                      sc_ref, wrh_ref, wrl_ref, xn_ref, h2_ref, aff_ref)


def _mix_out_rows(r0, nr, x_ref, ya_ref, yb_ref, yc_ref, wo_ref, g1_ref, sh_ref, sc_ref, wrh_ref, wrl_ref,
                  xn_ref, h2_ref, aff_ref):
    rg = slice(r0, r0 + nr)
    mix = jnp.dot(ya_ref[rg, :], wo_ref[0:SGU_W, :], preferred_element_type=F32)
    mix += jnp.dot(yb_ref[rg, :].astype(BF16), wo_ref[SGU_W:SGU_W + FNET_W, :], preferred_element_type=F32)
    mix += jnp.dot(yc_ref[rg, :], wo_ref[SGU_W + FNET_W:, :], preferred_element_type=F32)
    xn = x_ref[rg, :] + g1_ref[...] * mix
    xn_ref[rg, :] = xn
    d = xn.shape[-1]
    h2 = xn * _rms_scale(xn, d) * (1.0 + sc_ref[...]) + sh_ref[...]
    for j in range(d // LANES):
        h2_ref[pl.ds(r0 * SUBLANES + j, nr, stride=SUBLANES), :] = h2[:, j * LANES:(j + 1) * LANES]
    h_hi = h2.astype(BF16)
    h_lo = (h2 - h_hi.astype(F32)).astype(BF16)
    nt_dims = (((1,), (1,)), ((), ()))
    logits = lax.dot_general(wrh_ref[...], h_hi, nt_dims, preferred_element_type=F32)
    logits += lax.dot_general(wrh_ref[...], h_lo, nt_dims, preferred_element_type=F32)
    logits += lax.dot_general(wrl_ref[...], h_hi, nt_dims, preferred_element_type=F32)
    e = jnp.exp(logits - jnp.max(logits, axis=0, keepdims=True))
    aff_ref[:, rg] = e / jnp.sum(e, axis=0, keepdims=True)


def _mix_out(x, ya, yb, yc, w_out, g1, sh2, sc2, wr_hi, wr_lo, tm):
    bsz, nt, d = x.shape
    vec = pl.BlockSpec((None, 1, d), lambda b, i: (b, 0, 0))
    full = lambda arr: pl.BlockSpec(arr.shape, lambda b, i: (0,) * arr.ndim)
    return pl.pallas_call(
        functools.partial(_mix_out_kernel, tm=tm),
        out_shape=(
            jax.ShapeDtypeStruct((bsz, nt, d), F32),
            jax.ShapeDtypeStruct((bsz, nt * SUBLANES, LANES), F32),
            jax.ShapeDtypeStruct((bsz, N_EXPERTS, nt), F32),
        ),
        grid=(bsz, nt // tm),
        in_specs=[
            pl.BlockSpec((None, tm, d), lambda b, i: (b, i, 0)),
            pl.BlockSpec((None, tm, SGU_W), lambda b, i: (b, i, 0)),
            pl.BlockSpec((None, tm, FNET_W), lambda b, i: (b, i, 0)),
            pl.BlockSpec((None, tm, MLA_HEADS * V_DIM), lambda b, i: (b, i, 0)),
            full(w_out), vec, vec, vec, full(wr_hi), full(wr_lo),
        ],
        out_specs=(
            pl.BlockSpec((None, tm, d), lambda b, i: (b, i, 0)),
            pl.BlockSpec((None, tm * SUBLANES, LANES), lambda b, i: (b, i, 0)),
            pl.BlockSpec((None, N_EXPERTS, tm), lambda b, i: (b, 0, i)),
        ),
        compiler_params=_cparams(("parallel", "parallel")),
        name="mix_out",
    )(x, ya, yb, yc, w_out, g1, sh2, sc2, wr_hi, wr_lo)


def _prefix_count(mask, tri):
    e, n = mask.shape
    w = tri.shape[0]
    carry = jnp.zeros((e, 1), F32)
    outs = []
    ends = []
    for c in range(n // w):
        local = jnp.dot(mask[:, c * w:(c + 1) * w].astype(BF16), tri, preferred_element_type=F32)
        outs.append(local + carry)
        carry = carry + local[:, w - 1:w]
        ends.append(carry)
    return jnp.concatenate(outs, axis=1), ends


ROUTE_CHUNK = 2 * LANES


def _route_select_kernel(aff_ref, key_ref, cend_ref, *, cap):
    n_exp, n = aff_ref.shape
    w = min(n, ROUTE_CHUNK)
    aff = aff_ref[...]

    def step(i, thr_bits):
        cand = thr_bits | jnp.left_shift(jnp.int32(1), 30 - i)
        cnt = jnp.sum((aff >= pltpu.bitcast(cand, F32)).astype(jnp.int32), axis=-1, keepdims=True)
        return jnp.where(cnt >= cap, cand, thr_bits)

    thr = pltpu.bitcast(lax.fori_loop(0, 31, step, jnp.zeros((n_exp, 1), jnp.int32)), F32)
    gt = aff > thr
    eq = aff == thr
    need = cap - jnp.sum(gt.astype(jnp.int32), axis=-1, keepdims=True)
    r = lax.broadcasted_iota(jnp.int32, (w, w), 0)
    c = lax.broadcasted_iota(jnp.int32, (w, w), 1)
    tri = (r <= c).astype(BF16)
    eq_rank, _ = _prefix_count(eq.astype(F32), tri)
    sel = jnp.logical_or(gt, jnp.logical_and(eq, eq_rank <= need.astype(F32)))
    cum, ends = _prefix_count(sel.astype(F32), tri)
    key_ref[...] = jnp.where(sel, cum, 0.0)
    lane = lax.broadcasted_iota(jnp.int32, (n_exp, LANES), 1)
    cend = jnp.zeros((n_exp, LANES), F32)
    for ci, end in enumerate(ends):
        cend = jnp.where(lane == ci, end, cend)
    cend_ref[...] = cend.astype(jnp.int32)


def _route_compact_kernel(cend_ref, key_ref, aff_ref, idx_ref, gate_ref, acc_i_ref, acc_g_ref, *, cap, jb):
    b = pl.program_id(0)
    e = pl.program_id(1)
    n_chunks, w = key_ref.shape
    n_sb = cap // jb
    base = (b * pl.num_programs(1) + e) * LANES
    shift = jb.bit_length() - 1
    acc_i_ref[...] = jnp.zeros_like(acc_i_ref)
    acc_g_ref[...] = jnp.zeros_like(acc_g_ref)

    def chunk(c, _):
        c_start = jnp.where(c > 0, cend_ref[base + jnp.maximum(c - 1, 0)], 0)
        c_end = cend_ref[base + c]

        @pl.when(c_end > c_start)
        def _any_selected():
            krow = key_ref[pl.ds(c, 1), :]
            arow = aff_ref[pl.ds(c, 1), :]
            tpos = (lax.broadcasted_iota(jnp.int32, (1, w), 1) + c * w).astype(F32)
            sb_lo = lax.shift_right_logical(c_start, shift)
            sb_hi = jnp.minimum(lax.shift_right_logical(c_end - 1, shift), n_sb - 1)

            def slot_block(sb, _):
                slot = (lax.broadcasted_iota(jnp.int32, (jb, 1), 0) + (sb * jb + 1)).astype(F32)
                hit = krow == slot
                pi = jnp.where(hit, tpos, 0.0)
                pg = jnp.where(hit, arow, 0.0)
                fi = pi[:, 0:LANES]
                fg = pg[:, 0:LANES]
                for k in range(1, w // LANES):
                    fi = fi + pi[:, k * LANES:(k + 1) * LANES]
                    fg = fg + pg[:, k * LANES:(k + 1) * LANES]
                acc_i_ref[sb] = acc_i_ref[sb] + fi
                acc_g_ref[sb] = acc_g_ref[sb] + fg
                return 0

            lax.fori_loop(sb_lo, sb_hi + 1, slot_block, 0)

        return 0

    lax.fori_loop(0, n_chunks, chunk, 0)
    for sb in range(n_sb):
        idx_ref[sb * jb:(sb + 1) * jb, :] = jnp.sum(acc_i_ref[sb], axis=-1, keepdims=True).astype(jnp.int32)
        gate_ref[sb * jb:(sb + 1) * jb, :] = jnp.sum(acc_g_ref[sb], axis=-1, keepdims=True)


def _route(aff_t, cap):
    bsz, n_exp, n = aff_t.shape
    jb = min(cap, LANES)
    w = min(n, ROUTE_CHUNK)
    key, cend = pl.pallas_call(
        functools.partial(_route_select_kernel, cap=cap),
        out_shape=(jax.ShapeDtypeStruct((bsz, n_exp, n), F32),
                   jax.ShapeDtypeStruct((bsz, n_exp, LANES), jnp.int32)),
        grid=(bsz,),
        in_specs=[pl.BlockSpec((None, n_exp, n), lambda b: (b, 0, 0))],
        out_specs=(pl.BlockSpec((None, n_exp, n), lambda b: (b, 0, 0)),
                   pl.BlockSpec((None, n_exp, LANES), lambda b: (b, 0, 0))),
        compiler_params=_cparams(("parallel",)),
        name="route_select",
    )(aff_t)
    chunked = pl.BlockSpec((None, None, n // w, w), lambda b, e, cend: (b, e, 0, 0))
    out_blk = pl.BlockSpec((None, None, cap, 1), lambda b, e, cend: (b, e, 0, 0))
    grid_spec = pltpu.PrefetchScalarGridSpec(
        num_scalar_prefetch=1,
        grid=(bsz, n_exp),
        in_specs=[chunked, chunked],
        out_specs=(out_blk, out_blk),
        scratch_shapes=[pltpu.VMEM((cap // jb, jb, LANES), F32)] * 2,
    )
    return pl.pallas_call(
        functools.partial(_route_compact_kernel, cap=cap, jb=jb),
        out_shape=(jax.ShapeDtypeStruct((bsz, n_exp, cap, 1), jnp.int32),
                   jax.ShapeDtypeStruct((bsz, n_exp, cap, 1), F32)),
        grid_spec=grid_spec,
        compiler_params=_cparams(("parallel", "parallel")),
        name="route_compact",
    )(cend.reshape(-1), key.reshape(bsz, n_exp, n // w, w), aff_t.reshape(bsz, n_exp, n // w, w))


GATHER_UNROLL = 8


def _gather_start(idx_refs, h_refs, caps, xs_ref, sem, slot, e, b, n_exp):
    row0 = 0
    for idx_ref, h_hbm, cap in zip(idx_refs, h_refs, caps):
        base = (b * n_exp + e) * cap
        unroll = math.gcd(cap, GATHER_UNROLL)

        def issue(g, _, idx_ref=idx_ref, h_hbm=h_hbm, base=base, unroll=unroll, row0=row0):
            for u in range(unroll):
                i = g * unroll + u
                tok = idx_ref[base + i]
                pltpu.make_async_copy(
                    h_hbm.at[b, pl.ds(pl.multiple_of(tok * SUBLANES, SUBLANES), SUBLANES), :],
                    xs_ref.at[slot, pl.ds(pl.multiple_of((row0 + i) * SUBLANES, SUBLANES), SUBLANES), :],
                    sem.at[slot]).start()
            return 0

        lax.fori_loop(0, cap // unroll, issue, 0)
        row0 += cap


def _expert_kernel(*refs, caps, d):
    n_sets = len(caps)
    idx_refs = refs[:n_sets]
    h_refs = refs[n_sets:2 * n_sets]
    wg_ref, wu_ref, wd_ref = refs[2 * n_sets:2 * n_sets + 3]
    gate_refs = refs[2 * n_sets + 3:3 * n_sets + 3]
    ys_refs = refs[3 * n_sets + 3:4 * n_sets + 3]
    xs_ref, sem = refs[4 * n_sets + 3:]
    rows = sum(caps)
    e = pl.program_id(0)
    b = pl.program_id(1)
    n_exp = pl.num_programs(0)
    n_b = pl.num_programs(1)
    step = e * n_b + b
    slot = lax.rem(step, 2)

    @pl.when(step == 0)
    def _first():
        _gather_start(idx_refs, h_refs, caps, xs_ref, sem, 0, e, b, n_exp)

    nxt = step + 1

    @pl.when(nxt < n_exp * n_b)
    def _prefetch():
        _gather_start(idx_refs, h_refs, caps, xs_ref, sem, 1 - slot, lax.div(nxt, n_b), lax.rem(nxt, n_b), n_exp)

    pltpu.make_async_copy(h_refs[0].at[b, pl.ds(0, rows * SUBLANES), :], xs_ref.at[slot], sem.at[slot]).wait()

    x = jnp.concatenate(
        [xs_ref[slot, pl.ds(j, rows, stride=SUBLANES), :] for j in range(d // LANES)], axis=1).astype(BF16)
    hid = _silu(jnp.dot(x, wg_ref[...].astype(BF16), preferred_element_type=F32))
    hid = hid * jnp.dot(x, wu_ref[...].astype(BF16), preferred_element_type=F32)
    gate = gate_refs[0][...] if n_sets == 1 else jnp.concatenate([g[...] for g in gate_refs], axis=0)
    y = jnp.dot(hid.astype(BF16), wd_ref[...].astype(BF16), preferred_element_type=F32) * gate
    row0 = 0
    for ys_ref, cap in zip(ys_refs, caps):
        for j in range(d // LANES):
            ys_ref[pl.ds(j, cap, stride=SUBLANES), :] = y[row0:row0 + cap, j * LANES:(j + 1) * LANES]
        row0 += cap


def _experts(token_sets, wg, wu, wd, layer):
    caps = tuple(ts[3] for ts in token_sets)
    n_sets = len(token_sets)
    bsz = token_sets[0][1].shape[0]
    _, n_exp, d, f = wg.shape
    assert token_sets[0][1].shape[1] >= sum(caps) * SUBLANES
    wspec = lambda rows, cols: pl.BlockSpec((None, None, rows, cols), lambda e, b, *idx: (layer, e, 0, 0))
    grid_spec = pltpu.PrefetchScalarGridSpec(
        num_scalar_prefetch=n_sets,
        grid=(n_exp, bsz),
        in_specs=[pl.BlockSpec(memory_space=pl.ANY)] * n_sets + [wspec(d, f), wspec(d, f), wspec(f, d)] + [
            pl.BlockSpec((None, None, cap, 1), lambda e, b, *idx: (b, e, 0, 0)) for cap in caps],
        out_specs=tuple(pl.BlockSpec((None, None, cap * SUBLANES, LANES), lambda e, b, *idx: (b, e, 0, 0))
                        for cap in caps),
        scratch_shapes=[pltpu.VMEM((2, sum(caps) * SUBLANES, LANES), F32), pltpu.SemaphoreType.DMA((2,))],
    )
    return pl.pallas_call(
        functools.partial(_expert_kernel, caps=caps, d=d),
        out_shape=tuple(jax.ShapeDtypeStruct((bsz, n_exp, cap * SUBLANES, LANES), F32) for cap in caps),
        grid_spec=grid_spec,
        compiler_params=_cparams(("arbitrary", "arbitrary")),
        name="experts",
    )(*[ts[0] for ts in token_sets], *[ts[1] for ts in token_sets], wg, wu, wd, *[ts[2] for ts in token_sets])


def _combine_kernel(idx_ref, ys_ref, o_hbm, acc_ref, sem, *, cap, unroll):
    b = pl.program_id(0)
    e = pl.program_id(1)
    n_exp = pl.num_programs(1)
    base = (b * n_exp + e) * cap

    @pl.when(e == 0)
    def _zero():
        acc_ref[...] = jnp.zeros_like(acc_ref)

    def rows(g, _):
        i0 = g * unroll
        toks = [pl.multiple_of(idx_ref[base + i0 + u] * SUBLANES, SUBLANES) for u in range(unroll)]
        sums = [acc_ref[pl.ds(toks[u], SUBLANES), :]
                + ys_ref[pl.ds(pl.multiple_of((i0 + u) * SUBLANES, SUBLANES), SUBLANES), :]
                for u in range(unroll)]
        for u in range(unroll):
            acc_ref[pl.ds(toks[u], SUBLANES), :] = sums[u]
        return 0

    lax.fori_loop(0, cap // unroll, rows, 0)

    @pl.when(e == n_exp - 1)
    def _flush():
        cp = pltpu.make_async_copy(acc_ref, o_hbm.at[b], sem)
        cp.start()
        cp.wait()


def _combine(idx_flat, ys_tiles, nt, cap):
    bsz, n_exp = ys_tiles.shape[:2]
    grid_spec = pltpu.PrefetchScalarGridSpec(
        num_scalar_prefetch=1,
        grid=(bsz, n_exp),
        in_specs=[pl.BlockSpec((None, None, cap * SUBLANES, LANES), lambda b, e, idx: (b, e, 0, 0))],
        out_specs=pl.BlockSpec(memory_space=pl.ANY),
        scratch_shapes=[pltpu.VMEM((nt * SUBLANES, LANES), F32), pltpu.SemaphoreType.DMA(())],
    )
    return pl.pallas_call(
        functools.partial(_combine_kernel, cap=cap, unroll=4),
        out_shape=jax.ShapeDtypeStruct((bsz, nt * SUBLANES, LANES), F32),
        grid_spec=grid_spec,
        compiler_params=_cparams(("arbitrary", "arbitrary")),
        name="combine",
    )(idx_flat, ys_tiles)


def _residual_kernel(x_ref, m_ref, g_ref, o_ref, *, tm):
    d = x_ref.shape[-1]
    moe = jnp.concatenate(
        [m_ref[pl.ds(j, tm, stride=SUBLANES), :] for j in range(d // LANES)], axis=1)
    o_ref[...] = x_ref[...] + g_ref[...] * moe


def _residual(x, moe_tiles, g2, tm):
    bsz, nt, d = x.shape
    return pl.pallas_call(
        functools.partial(_residual_kernel, tm=tm),
        out_shape=jax.ShapeDtypeStruct((bsz, nt, d), F32),
        grid=(bsz, nt // tm),
        in_specs=[
            pl.BlockSpec((None, tm, d), lambda b, i: (b, i, 0)),
            pl.BlockSpec((None, tm * SUBLANES, LANES), lambda b, i: (b, i, 0)),
            pl.BlockSpec((None, 1, d), lambda b, i: (b, 0, 0)),
        ],
        out_specs=pl.BlockSpec((None, tm, d), lambda b, i: (b, i, 0)),
        compiler_params=_cparams(("parallel", "parallel")),
        name="residual",
    )(x, moe_tiles, g2)


def _prepare_layer(l, w_in, sgu_norm, w_sgu, b_sgu, q_lora_norm, w_uq, kv_lora_norm, w_ukv, q_norm, k_norm,
                   w_out, w_router, w_gate, w_up, w_down):
    w_in_l = w_in[l]
    kr_cols = w_in_l[:, OFF_KR:OFF_KR + QK_ROPE]
    w_in_ext = jnp.concatenate([w_in_l[:, :OFF_KR], kr_cols, kr_cols], axis=1).astype(BF16)
    uq = w_uq[l].reshape(Q_LORA, MLA_HEADS, QK_DIM)
    w_uq_p = jnp.concatenate(
        [uq[:, :, :QK_NOPE].reshape(Q_LORA, MLA_HEADS * QK_NOPE),
         uq[:, :, QK_NOPE:].reshape(Q_LORA, MLA_HEADS * QK_ROPE)], axis=1).astype(BF16)

    def norm_pad(v):
        return jnp.concatenate([v[:QK_NOPE], v[QK_NOPE:], v[QK_NOPE:]])[None, :]

    wr_t = w_router[l].T
    wr_hi = wr_t.astype(BF16)
    wr_lo = (wr_t - wr_hi.astype(F32)).astype(BF16)
    return dict(
        w_in=w_in_ext,
        sgu_norm=sgu_norm[l][None, :],
        w_sgu=jnp.concatenate([w_sgu[l][hh] for hh in range(SGU_HEADS)], axis=1).astype(BF16),
        b_sgu=jnp.repeat(b_sgu[l].T, SGU_HEAD_DIM, axis=1),
        fdft=jnp.asarray(_channel_dft_table()).astype(BF16),
        q_lora_norm=q_lora_norm[l][None, :],
        w_uq=w_uq_p,
        q_norm=norm_pad(q_norm[l]),
        kv_lora_norm=kv_lora_norm[l][None, :],
        w_ukv=w_ukv[l].astype(BF16),
        k_norm=norm_pad(k_norm[l]),
        w_out=w_out[l].astype(BF16),
        wr_hi=wr_hi, wr_lo=wr_lo,
        w_gate=w_gate, w_up=w_up, w_down=w_down, layer=l,
    )


def _rope_tables(n):
    pos = np.arange(n)
    half = QK_ROPE // 4
    freqs = ROPE_THETA ** (-np.arange(half, dtype=np.float64) / half)
    ang_r = (pos // GRID_WIDTH)[:, None] * freqs
    ang_c = (pos % GRID_WIDTH)[:, None] * freqs
    cos64 = np.concatenate([np.cos(ang_r), np.cos(ang_r), np.cos(ang_c), np.cos(ang_c)], axis=1)
    sin64 = np.concatenate([-np.sin(ang_r), np.sin(ang_r), -np.sin(ang_c), np.sin(ang_c)], axis=1)
    return (jnp.asarray(np.tile(cos64, (1, 2)), F32), jnp.asarray(np.tile(sin64, (1, 2)), F32))


def _token_tile(nt, pref):
    return pref if nt % pref == 0 else nt


def _moe(populations, lw):
    token_sets = []
    for h2_tiles, aff_t in populations:
        nt = aff_t.shape[-1]
        cap = EC_FACTOR * nt // N_EXPERTS
        idx, gate = _route(aff_t, cap)
        token_sets.append((idx.reshape(-1), h2_tiles, gate, cap))
    ys = _experts(token_sets, lw["w_gate"], lw["w_up"], lw["w_down"], lw["layer"])
    return [_combine(ts[0], y, pop[1].shape[-1], ts[3]) for ts, y, pop in zip(token_sets, ys, populations)]


def kernel(x, c, ctx, c_ctx, w_ada, b_ada, w_in, sgu_norm, w_sgu, b_sgu, q_lora_norm, w_uq, kv_lora_norm, w_ukv,
           q_norm, k_norm, w_out, w_router, w_gate, w_up, w_down):
    bsz, n, d = x.shape
    n_ctx = ctx.shape[1]
    depth = w_ada.shape[0]
    assert bsz + 1 <= SUBLANES

    cond = jnp.zeros((SUBLANES, d), F32).at[:bsz].set(c).at[bsz].set(c_ctx)
    mod = _ada_modulation(cond, w_ada, b_ada)
    rope_tabs = _rope_tables(n)
    tm = _token_tile(n, 1024)
    tm_in = _token_tile(n, 1024)
    tm_c = _token_tile(n_ctx, 256)

    xc = ctx
    pending = pending_c = None
    for l in range(depth):
        last = l == depth - 1
        lw = _prepare_layer(l, w_in, sgu_norm, w_sgu, b_sgu, q_lora_norm, w_uq, kv_lora_norm, w_ukv,
                            q_norm, k_norm, w_out, w_router, w_gate, w_up, w_down)
        parts = [mod[l, :, i * d:(i + 1) * d] for i in range(6)]
        lat = [p[:bsz, None, :] for p in parts]
        cx = [jnp.broadcast_to(p[bsz][None, None, :], (bsz, 1, d)) for p in parts]

        outs_c = _mix_in(xc, cx[0], cx[1], lw, None, tm_c, pending_c)
        outs = _mix_in(x, lat[0], lat[1], lw, rope_tabs, tm_in, pending)
        ya_c, a_c, b_c, q_c, k_c, v_c = outs_c[:6]
        ya, a, b, q, k, v = outs[:6]
        if pending is not None:
            xc, x = outs_c[6], outs[6]
        yb = _fourier_positions(a, b)
        tk = _token_tile(n, ATTN_TK)
        vt_c = jnp.swapaxes(v_c, 2, 3)
        vt = jnp.swapaxes(v.reshape(bsz, MLA_HEADS, n // tk, tk, V_DIM), 3, 4)
        yc = _attention(jnp.swapaxes(q, 2, 3), k_c, vt_c, k, vt, tq=_token_tile(n, ATTN_TQ), tk=tk)
        x_new, h2_tiles, aff_t = _mix_out(x, ya, yb, yc, lw["w_out"], lat[2], lat[3], lat[4],
                                          lw["wr_hi"], lw["wr_lo"], tm)
        if not last:
            yb_c = _fourier_positions(a_c, b_c)
            yc_c = _attention(jnp.swapaxes(q_c, 2, 3), k_c, vt_c, None, None,
                              tq=_token_tile(n_ctx, ATTN_TQ), tk=n_ctx)
            xc_new, h2c_tiles, affc_t = _mix_out(xc, ya_c, yb_c, yc_c, lw["w_out"], cx[2], cx[3], cx[4],
                                                 lw["wr_hi"], lw["wr_lo"], tm_c)
            moe, moe_c = _moe([(h2_tiles, aff_t), (h2c_tiles, affc_t)], lw)
            xc, pending_c = xc_new, (moe_c, cx[5])
        else:
            moe, = _moe([(h2_tiles, aff_t)], lw)
        x, pending = x_new, (moe, lat[5])
    return _residual(x, pending[0], pending[1], tm)
```

```python
import functools
import math

import numpy as np
import jax
import jax.numpy as jnp
from jax import lax
from jax.experimental import pallas as pl
from jax.experimental.pallas import tpu as pltpu

F32 = jnp.float32
BF16 = jnp.bfloat16

GRID_WIDTH = 64
SGU_HEADS = 4
SGU_HEAD_DIM = 64
SGU_W = SGU_HEADS * SGU_HEAD_DIM
CHUNK = 128
FNET_GROUP_DIM = 64
FNET_W = 256
MLA_HEADS = 4
QK_NOPE = 128
QK_ROPE = 64
QK_DIM = QK_NOPE + QK_ROPE
V_DIM = 128
Q_LORA = 256
KV_LORA = 128
OFF_V = 256
OFF_F = 512
OFF_Q = 768
OFF_KV = 1024
OFF_KR = 1152
N_EXPERTS = 16
EC_FACTOR = 2
ROPE_THETA = 10000.0
EPS = 1e-6

LANES = 128
SUBLANES = 8
VMEM_LIMIT_BYTES = 56 * 1024 * 1024

QK_PAD = 2 * LANES
ATTN_TQ = 2048
ATTN_TK = 512
LOG2_E = 1.4426950408889634
MIX_ROWS = 128
MIX_OUT_ROWS = 512


def _cparams(sem, vmem=None):
    return pltpu.CompilerParams(dimension_semantics=sem, vmem_limit_bytes=vmem or VMEM_LIMIT_BYTES)


def _rms_scale(v, width):
    return lax.rsqrt(jnp.sum(v * v, axis=-1, keepdims=True) * (1.0 / width) + EPS)


def _gelu_tanh(v):
    return 0.5 * v * (1.0 + jnp.tanh(0.7978845608028654 * (v + 0.044715 * v * v * v)))


def _silu(v):
    return v / (1.0 + jnp.exp(-v))


def _ada_kernel(cond_ref, w_ref, b_ref, o_ref):
    s = _silu(cond_ref[...])
    o_ref[...] = jnp.dot(s, w_ref[...], preferred_element_type=F32,
                         precision=lax.Precision.HIGHEST) + b_ref[...]


def _ada_modulation(cond, w_ada, b_ada):
    depth, d, six_d = w_ada.shape
    tn = 1536
    return pl.pallas_call(
        _ada_kernel,
        out_shape=jax.ShapeDtypeStruct((depth, SUBLANES, six_d), F32),
        grid=(depth, six_d // tn),
        in_specs=[
            pl.BlockSpec((SUBLANES, d), lambda l, j: (0, 0)),
            pl.BlockSpec((None, d, tn), lambda l, j: (l, 0, j)),
            pl.BlockSpec((None, 1, tn), lambda l, j: (l, 0, j)),
        ],
        out_specs=pl.BlockSpec((None, SUBLANES, tn), lambda l, j: (l, 0, j)),
        compiler_params=_cparams(("arbitrary", "arbitrary")),
        name="ada_modulation",
    )(cond, w_ada, b_ada.reshape(depth, 1, six_d))


def _swap16(v):
    lane = lax.broadcasted_iota(jnp.int32, v.shape, 1)
    up = pltpu.roll(v, LANES - 16, axis=1)
    dn = pltpu.roll(v, 16, axis=1)
    return jnp.where((lane % 32) < 16, up, dn)


def _mix_in_kernel(*refs, tm, use_rope, fuse_residual):
    rows_per_group = min(tm, MIX_ROWS)
    for g in range(tm // rows_per_group):
        _mix_in_rows(slice(g * rows_per_group, (g + 1) * rows_per_group), refs, use_rope, fuse_residual)


def _mix_in_rows(rg, refs, use_rope, fuse_residual):
    if fuse_residual:
        x_ref, moe_ref, g2_ref = refs[:3]
        xo_ref = refs[-1]
        refs = refs[:1] + refs[3:-1]
    (x_ref, sh_ref, sc_ref, win_ref, sgun_ref, wsgu_ref, bsgu_ref, fdft_ref, qln_ref, wuq_ref, qn_ref,
     kvn_ref, wukv_ref, kn_ref, cos_ref, sin_ref, ya_ref, a_ref, b_ref, q_ref, k_ref, v_ref) = refs
    x = x_ref[rg, :]
    if fuse_residual:
        nrows = rg.stop - rg.start
        moe = jnp.concatenate(
            [moe_ref[pl.ds(rg.start * SUBLANES + j, nrows, stride=SUBLANES), :]
             for j in range(x.shape[-1] // LANES)], axis=1)
        x = x + g2_ref[...] * moe
        xo_ref[rg, :] = x
    tm = x.shape[0]
    d = x.shape[-1]
    h = x * _rms_scale(x, d) * (1.0 + sc_ref[...]) + sh_ref[...]
    p = jnp.dot(h.astype(BF16), win_ref[...], preferred_element_type=F32)

    u = _gelu_tanh(p[:, 0:OFF_V])
    gv = _gelu_tanh(p[:, OFF_V:OFF_F])
    vn = gv * _rms_scale(gv, SGU_W) * sgun_ref[...]
    head_of_lane = lax.broadcasted_iota(jnp.int32, (CHUNK, SGU_W), 1) // SGU_HEAD_DIM
    for c in range(tm // CHUNK):
        rows = slice(c * CHUNK, (c + 1) * CHUNK)
        vc = vn[rows]
        vstack = jnp.concatenate(
            [jnp.where(head_of_lane == hh, vc, 0.0) for hh in range(SGU_HEADS)], axis=0).astype(BF16)
        z = jnp.dot(wsgu_ref[...], vstack, preferred_element_type=F32) + bsgu_ref[...]
        ya_ref[rg.start + c * CHUNK:rg.start + (c + 1) * CHUNK, :] = (u[rows] * z).astype(ya_ref.dtype)

    ab = jnp.dot(p[:, OFF_F:OFF_Q].astype(BF16), fdft_ref[...], preferred_element_type=F32)
    a_ref[rg, :] = ab[:, 0:FNET_W].astype(a_ref.dtype)
    b_ref[rg, :] = ab[:, FNET_W:2 * FNET_W].astype(b_ref.dtype)

    lane = lax.broadcasted_iota(jnp.int32, (1, LANES), 1)
    lo = lane < QK_ROPE
    if use_rope:
        cos = cos_ref[rg, :]
        sin = sin_ref[rg, :]

    def rope(t):
        return t * cos + _swap16(t) * sin if use_rope else t

    pq = p[:, OFF_Q:OFF_KV]
    cq = pq * _rms_scale(pq, Q_LORA) * qln_ref[...]
    q = jnp.dot(cq.astype(BF16), wuq_ref[...], preferred_element_type=F32)
    qnorm = qn_ref[...]
    for hh in range(MLA_HEADS):
        qn = q[:, LANES * hh:LANES * (hh + 1)]
        pair = hh // 2
        qr = q[:, MLA_HEADS * LANES + LANES * pair:MLA_HEADS * LANES + LANES * (pair + 1)]
        qr = jnp.where(lo if hh % 2 == 0 else jnp.logical_not(lo), qr, 0.0)
        ssq = jnp.sum(qn * qn, axis=-1, keepdims=True) + jnp.sum(qr * qr, axis=-1, keepdims=True)
        rinv = lax.rsqrt(ssq * (1.0 / QK_DIM) + EPS) * (QK_DIM ** -0.5 * LOG2_E)
        q_ref[hh, rg, 0:LANES] = (qn * rinv * qnorm[:, 0:LANES]).astype(q_ref.dtype)
        q_ref[hh, rg, LANES:QK_PAD] = rope(qr * rinv * qnorm[:, LANES:QK_PAD]).astype(q_ref.dtype)

    pkv = p[:, OFF_KV:OFF_KR]
    ckv = pkv * _rms_scale(pkv, KV_LORA) * kvn_ref[...]
    kv = jnp.dot(ckv.astype(BF16), wukv_ref[...], preferred_element_type=F32)
    krr = p[:, OFF_KR:OFF_KR + LANES]
    ssq_kr = 0.5 * jnp.sum(krr * krr, axis=-1, keepdims=True)
    knorm = kn_ref[...]
    rinvs = []
    for hh in range(MLA_HEADS):
        kn = kv[:, 2 * LANES * hh:2 * LANES * hh + LANES]
        ssq = jnp.sum(kn * kn, axis=-1, keepdims=True) + ssq_kr
        rinv = lax.rsqrt(ssq * (1.0 / QK_DIM) + EPS)
        rinvs.append(rinv)
        k_ref[hh, rg, 0:LANES] = (kn * rinv * knorm[:, 0:LANES]).astype(k_ref.dtype)
        v_ref[hh, rg, :] = kv[:, 2 * LANES * hh + LANES:2 * LANES * (hh + 1)].astype(v_ref.dtype)
    for pair in range(MLA_HEADS // 2):
        r2 = jnp.where(lo, rinvs[2 * pair], rinvs[2 * pair + 1])
        kr = rope(krr * r2 * knorm[:, LANES:QK_PAD]).astype(k_ref.dtype)
        k_ref[2 * pair, rg, LANES:QK_PAD] = kr
        k_ref[2 * pair + 1, rg, LANES:QK_PAD] = kr


def _mix_in(x, sh, sc, lw, rope_tabs, tm, pending=None):
    bsz, nt, d = x.shape
    use_rope = rope_tabs is not None
    if use_rope:
        cos_t, sin_t = rope_tabs
    else:
        cos_t = sin_t = jnp.zeros((nt, LANES), F32)
    full = lambda arr: pl.BlockSpec(arr.shape, lambda b, i: (0,) * arr.ndim)
    vec = pl.BlockSpec((None, 1, d), lambda b, i: (b, 0, 0))
    in_w = lw["w_in"].shape[1]
    fuse = pending is not None
    kern = functools.partial(_mix_in_kernel, tm=tm, use_rope=use_rope, fuse_residual=fuse)
    row_blk = pl.BlockSpec((None, tm, d), lambda b, i: (b, i, 0))
    out_shape = [
        jax.ShapeDtypeStruct((bsz, nt, SGU_W), BF16),
        jax.ShapeDtypeStruct((bsz, nt, FNET_W), F32),
        jax.ShapeDtypeStruct((bsz, nt, FNET_W), F32),
        jax.ShapeDtypeStruct((bsz, MLA_HEADS, nt, QK_PAD), BF16),
        jax.ShapeDtypeStruct((bsz, MLA_HEADS, nt, QK_PAD), BF16),
        jax.ShapeDtypeStruct((bsz, MLA_HEADS, nt, V_DIM), BF16),
    ]
    out_specs = [
        pl.BlockSpec((None, tm, SGU_W), lambda b, i: (b, i, 0)),
        pl.BlockSpec((None, tm, FNET_W), lambda b, i: (b, i, 0)),
        pl.BlockSpec((None, tm, FNET_W), lambda b, i: (b, i, 0)),
        pl.BlockSpec((None, MLA_HEADS, tm, QK_PAD), lambda b, i: (b, 0, i, 0)),
        pl.BlockSpec((None, MLA_HEADS, tm, QK_PAD), lambda b, i: (b, 0, i, 0)),
        pl.BlockSpec((None, MLA_HEADS, tm, V_DIM), lambda b, i: (b, 0, i, 0)),
    ]
    in_specs = [row_blk]
    args = [x]
    if fuse:
        in_specs += [pl.BlockSpec((None, tm * SUBLANES, LANES), lambda b, i: (b, i, 0)), vec]
        args += list(pending)
        out_shape.append(jax.ShapeDtypeStruct((bsz, nt, d), F32))
        out_specs.append(row_blk)
    in_specs += [
        vec, vec,
        full(lw["w_in"]), full(lw["sgu_norm"]), full(lw["w_sgu"]), full(lw["b_sgu"]), full(lw["fdft"]),
        full(lw["q_lora_norm"]), full(lw["w_uq"]), full(lw["q_norm"]),
        full(lw["kv_lora_norm"]), full(lw["w_ukv"]), full(lw["k_norm"]),
        pl.BlockSpec((tm, LANES), lambda b, i: (i, 0)),
        pl.BlockSpec((tm, LANES), lambda b, i: (i, 0)),
    ]
    args += [sh, sc, lw["w_in"], lw["sgu_norm"], lw["w_sgu"], lw["b_sgu"], lw["fdft"],
             lw["q_lora_norm"], lw["w_uq"], lw["q_norm"], lw["kv_lora_norm"], lw["w_ukv"], lw["k_norm"],
             cos_t, sin_t]
    return pl.pallas_call(
        kern,
        out_shape=tuple(out_shape),
        grid=(bsz, nt // tm),
        in_specs=in_specs,
        out_specs=tuple(out_specs),
        compiler_params=_cparams(("parallel", "parallel")),
        name="mix_in",
    )(*args)


DFT_UNROLL = 16


def _dft_kernel(a_ref, b_ref, w_ref, m_ref, o_ref, yr_ref, yi_ref, *, n1, n2):
    w1 = w_ref[...]
    c = a_ref.shape[-1]

    u1 = math.gcd(n2, DFT_UNROLL)
    u2 = math.gcd(n1, DFT_UNROLL)

    def step1(g, _):
        cols = []
        for u in range(u1):
            rows = pl.ds(g * u1 + u, n1, stride=n2)
            cols.append(jnp.concatenate([a_ref[rows, :], b_ref[rows, :]], axis=0))
        r = jnp.dot(w1, jnp.concatenate(cols, axis=1).astype(BF16), preferred_element_type=F32)
        for u in range(u1):
            out_rows = pl.ds(pl.multiple_of((g * u1 + u) * n1, n1), n1)
            yr_ref[out_rows, :] = r[0:n1, u * c:(u + 1) * c]
            yi_ref[out_rows, :] = r[n1:2 * n1, u * c:(u + 1) * c]
        return 0

    lax.fori_loop(0, n2 // u1, step1, 0)

    def step2(g, _):
        for u in range(u2):
            k1 = g * u2 + u
            rows = pl.ds(k1, n2, stride=n1)
            ycat = jnp.concatenate([yr_ref[rows, :], yi_ref[rows, :]], axis=0).astype(BF16)
            o_ref[rows, :] = jnp.dot(m_ref[k1], ycat, preferred_element_type=F32)
        return 0

    lax.fori_loop(0, n1 // u2, step2, 0)


def _dft_factors(n):
    n1 = 64 if n >= 1024 else 32
    return n1, n // n1


@functools.lru_cache(maxsize=None)
def _dft_tables(n):
    n1, n2 = _dft_factors(n)
    j = np.arange(n1, dtype=np.float64)
    ang1 = 2.0 * np.pi * np.outer(j, j) / n1
    c1 = np.cos(ang1) / math.sqrt(n1)
    s1 = np.sin(ang1) / math.sqrt(n1)
    w1 = np.block([[c1, -s1], [s1, c1]])
    k1 = np.arange(n1, dtype=np.float64)[:, None, None]
    k2 = np.arange(n2, dtype=np.float64)[None, :, None]
    m2 = np.arange(n2, dtype=np.float64)[None, None, :]
    ang2 = 2.0 * np.pi * m2 * (k1 + n1 * k2) / n
    m2cat = np.concatenate([np.cos(ang2), -np.sin(ang2)], axis=-1) / math.sqrt(n2)
    return w1.astype(np.float32), m2cat.astype(np.float32)


@functools.lru_cache(maxsize=None)
def _channel_dft_table():
    j = np.arange(FNET_GROUP_DIM, dtype=np.float64)
    ang = 2.0 * np.pi * np.outer(j, j) / FNET_GROUP_DIM
    groups = FNET_W // FNET_GROUP_DIM
    cg = np.kron(np.eye(groups), np.cos(ang)) / math.sqrt(FNET_GROUP_DIM)
    sg = np.kron(np.eye(groups), np.sin(ang)) / math.sqrt(FNET_GROUP_DIM)
    return np.concatenate([cg, sg], axis=1).astype(np.float32)


def _fourier_positions(a, b):
    bsz, n, c = a.shape
    n1, n2 = _dft_factors(n)
    w1_np, m2_np = _dft_tables(n)
    w1 = jnp.asarray(w1_np).astype(BF16)
    m2cat = jnp.asarray(m2_np).astype(BF16)
    cb = LANES
    blk = pl.BlockSpec((None, n, cb), lambda bi, j: (bi, 0, j))
    return pl.pallas_call(
        functools.partial(_dft_kernel, n1=n1, n2=n2),
        out_shape=jax.ShapeDtypeStruct((bsz, n, c), F32),
        grid=(bsz, c // cb),
        in_specs=[blk, blk,
                  pl.BlockSpec((2 * n1, 2 * n1), lambda bi, j: (0, 0)),
                  pl.BlockSpec((n1, n2, 2 * n2), lambda bi, j: (0, 0, 0))],
        out_specs=blk,
        scratch_shapes=[pltpu.VMEM((n, cb), F32)] * 2,
        compiler_params=_cparams(("parallel", "parallel")),
        name="fourier",
    )(a, b, w1, m2cat)


NEG_BIG = -1e30


def _attn_kernel(*refs, tk, n_lat_blocks):
    if n_lat_blocks:
        qt_ref, kc_ref, vtc_ref, kl_ref, vtl_ref, o_ref, acc_ref, sa_ref, sb_ref = refs
    else:
        qt_ref, kc_ref, vtc_ref, o_ref, acc_ref = refs
    qt = qt_ref[...]
    tq = qt.shape[1]

    def scores(kj):
        return jnp.dot(kj, qt, preferred_element_type=F32)

    def consume(s, vtj, m, l, first):
        m_new = jnp.maximum(m, jnp.max(s, axis=0, keepdims=True))
        alpha = jnp.exp2(m - m_new)
        p = jnp.exp2(s - m_new)
        l_new = alpha * l + jnp.sum(p, axis=0, keepdims=True)
        pv = jnp.dot(vtj, p.astype(BF16), preferred_element_type=F32)
        acc_ref[...] = pv if first else alpha * acc_ref[...] + pv
        return m_new, l_new

    def lat_keys(j):
        return kl_ref[pl.ds(pl.multiple_of(j * tk, tk), tk), :]

    m0 = jnp.full((1, tq), NEG_BIG, F32)
    l0 = jnp.zeros((1, tq), F32)
    if n_lat_blocks:
        sa_ref[...] = scores(lat_keys(0))
    m, l = consume(scores(kc_ref[...]), vtc_ref[...], m0, l0, True)
    if n_lat_blocks:
        def body(i, carry):
            sb_ref[...] = scores(lat_keys(2 * i + 1))
            m, l = consume(sa_ref[...], vtl_ref[2 * i], carry[0], carry[1], False)
            sa_ref[...] = scores(lat_keys(2 * i + 2))
            return consume(sb_ref[...], vtl_ref[2 * i + 1], m, l, False)

        m, l = lax.fori_loop(0, n_lat_blocks // 2 - 1, body, (m, l))
        sb_ref[...] = scores(lat_keys(n_lat_blocks - 1))
        m, l = consume(sa_ref[...], vtl_ref[n_lat_blocks - 2], m, l, False)
        m, l = consume(sb_ref[...], vtl_ref[n_lat_blocks - 1], m, l, False)
    o_ref[...] = jnp.transpose(acc_ref[...] / l).astype(o_ref.dtype)


def _attention(qt, k_ctx, vt_ctx, k_lat, vt_lat, tq, tk):
    bsz, heads, _, nq = qt.shape
    m_ctx = k_ctx.shape[2]
    n_lat_blocks = 0 if k_lat is None else k_lat.shape[2] // tk
    in_specs = [
        pl.BlockSpec((None, None, QK_PAD, tq), lambda b, h, i: (b, h, 0, i)),
        pl.BlockSpec((None, None, m_ctx, QK_PAD), lambda b, h, i: (b, h, 0, 0)),
        pl.BlockSpec((None, None, V_DIM, m_ctx), lambda b, h, i: (b, h, 0, 0)),
    ]
    args = [qt, k_ctx, vt_ctx]
    if n_lat_blocks:
        in_specs += [
            pl.BlockSpec((None, None, k_lat.shape[2], QK_PAD), lambda b, h, i: (b, h, 0, 0)),
            pl.BlockSpec((None, None, n_lat_blocks, V_DIM, tk), lambda b, h, i: (b, h, 0, 0, 0)),
        ]
        args += [k_lat, vt_lat]
    return pl.pallas_call(
        functools.partial(_attn_kernel, tk=tk, n_lat_blocks=n_lat_blocks),
        out_shape=jax.ShapeDtypeStruct((bsz, nq, heads * V_DIM), BF16),
        grid=(bsz, heads, nq // tq),
        in_specs=in_specs,
        out_specs=pl.BlockSpec((None, tq, V_DIM), lambda b, h, i: (b, i, h)),
        scratch_shapes=[pltpu.VMEM((V_DIM, tq), F32)] + [pltpu.VMEM((tk, tq), F32)] * (2 if n_lat_blocks else 0),
        compiler_params=_cparams(("parallel", "parallel", "arbitrary")),
        name="attention",
    )(*args)


def _mix_out_kernel(x_ref, ya_ref, yb_ref, yc_ref, wo_ref, g1_ref, sh_ref, sc_ref, wrh_ref, wrl_ref,
                    xn_ref, h2_ref, aff_ref, *, tm):
    rows_per_group = min(tm, MIX_OUT_ROWS)
    for g in range(tm // rows_per_group):
        _mix_out_rows(g * rows_per_group, rows_per_group, x_ref, ya_ref, yb_ref, yc_ref, wo_ref, g1_ref, sh_ref,
                      sc_ref, wrh_ref, wrl_ref, xn_ref, h2_ref, aff_ref)


def _mix_out_rows(r0, nr, x_ref, ya_ref, yb_ref, yc_ref, wo_ref, g1_ref, sh_ref, sc_ref, wrh_ref, wrl_ref,
                  xn_ref, h2_ref, aff_ref):
    rg = slice(r0, r0 + nr)
    mix = jnp.dot(ya_ref[rg, :], wo_ref[0:SGU_W, :], preferred_element_type=F32)
    mix += jnp.dot(yb_ref[rg, :].astype(BF16), wo_ref[SGU_W:SGU_W + FNET_W, :], preferred_element_type=F32)
    mix += jnp.dot(yc_ref[rg, :], wo_ref[SGU_W + FNET_W:, :], preferred_element_type=F32)
    xn = x_ref[rg, :] + g1_ref[...] * mix
    xn_ref[rg, :] = xn
    d = xn.shape[-1]
    h2 = xn * _rms_scale(xn, d) * (1.0 + sc_ref[...]) + sh_ref[...]
    for j in range(d // LANES):
        h2_ref[pl.ds(r0 * SUBLANES + j, nr, stride=SUBLANES), :] = h2[:, j * LANES:(j + 1) * LANES]
    h_hi = h2.astype(BF16)
    h_lo = (h2 - h_hi.astype(F32)).astype(BF16)
    nt_dims = (((1,), (1,)), ((), ()))
    logits = lax.dot_general(wrh_ref[...], h_hi, nt_dims, preferred_element_type=F32)
    logits += lax.dot_general(wrh_ref[...], h_lo, nt_dims, preferred_element_type=F32)
    logits += lax.dot_general(wrl_ref[...], h_hi, nt_dims, preferred_element_type=F32)
    e = jnp.exp(logits - jnp.max(logits, axis=0, keepdims=True))
    aff_ref[:, rg] = e / jnp.sum(e, axis=0, keepdims=True)


def _mix_out(x, ya, yb, yc, w_out, g1, sh2, sc2, wr_hi, wr_lo, tm):
    bsz, nt, d = x.shape
    vec = pl.BlockSpec((None, 1, d), lambda b, i: (b, 0, 0))
    full = lambda arr: pl.BlockSpec(arr.shape, lambda b, i: (0,) * arr.ndim)
    return pl.pallas_call(
        functools.partial(_mix_out_kernel, tm=tm),
        out_shape=(
            jax.ShapeDtypeStruct((bsz, nt, d), F32),
            jax.ShapeDtypeStruct((bsz, nt * SUBLANES, LANES), F32),
            jax.ShapeDtypeStruct((bsz, N_EXPERTS, nt), F32),
        ),
        grid=(bsz, nt // tm),
        in_specs=[
            pl.BlockSpec((None, tm, d), lambda b, i: (b, i, 0)),
            pl.BlockSpec((None, tm, SGU_W), lambda b, i: (b, i, 0)),
            pl.BlockSpec((None, tm, FNET_W), lambda b, i: (b, i, 0)),
            pl.BlockSpec((None, tm, MLA_HEADS * V_DIM), lambda b, i: (b, i, 0)),
            full(w_out), vec, vec, vec, full(wr_hi), full(wr_lo),
        ],
        out_specs=(
            pl.BlockSpec((None, tm, d), lambda b, i: (b, i, 0)),
            pl.BlockSpec((None, tm * SUBLANES, LANES), lambda b, i: (b, i, 0)),
            pl.BlockSpec((None, N_EXPERTS, tm), lambda b, i: (b, 0, i)),
        ),
        compiler_params=_cparams(("parallel", "parallel")),
        name="mix_out",
    )(x, ya, yb, yc, w_out, g1, sh2, sc2, wr_hi, wr_lo)


def _prefix_count(mask, tri):
    e, n = mask.shape
    w = tri.shape[0]
    carry = jnp.zeros((e, 1), F32)
    outs = []
    ends = []
    for c in range(n // w):
        local = jnp.dot(mask[:, c * w:(c + 1) * w].astype(BF16), tri, preferred_element_type=F32)
        outs.append(local + carry)
        carry = carry + local[:, w - 1:w]
        ends.append(carry)
    return jnp.concatenate(outs, axis=1), ends


ROUTE_CHUNK = 2 * LANES


def _route_select_kernel(aff_ref, key_ref, cend_ref, *, cap):
    n_exp, n = aff_ref.shape
    w = min(n, ROUTE_CHUNK)
    aff = aff_ref[...]

    def step(i, thr_bits):
        cand = thr_bits | jnp.left_shift(jnp.int32(1), 30 - i)
        cnt = jnp.sum((aff >= pltpu.bitcast(cand, F32)).astype(jnp.int32), axis=-1, keepdims=True)
        return jnp.where(cnt >= cap, cand, thr_bits)

    thr = pltpu.bitcast(lax.fori_loop(0, 31, step, jnp.zeros((n_exp, 1), jnp.int32)), F32)
    gt = aff > thr
    eq = aff == thr
    need = cap - jnp.sum(gt.astype(jnp.int32), axis=-1, keepdims=True)
    r = lax.broadcasted_iota(jnp.int32, (w, w), 0)
    c = lax.broadcasted_iota(jnp.int32, (w, w), 1)
    tri = (r <= c).astype(BF16)
    eq_rank, _ = _prefix_count(eq.astype(F32), tri)
    sel = jnp.logical_or(gt, jnp.logical_and(eq, eq_rank <= need.astype(F32)))
    cum, ends = _prefix_count(sel.astype(F32), tri)
    key_ref[...] = jnp.where(sel, cum, 0.0)
    lane = lax.broadcasted_iota(jnp.int32, (n_exp, LANES), 1)
    cend = jnp.zeros((n_exp, LANES), F32)
    for ci, end in enumerate(ends):
        cend = jnp.where(lane == ci, end, cend)
    cend_ref[...] = cend.astype(jnp.int32)


def _route_compact_kernel(cend_ref, key_ref, aff_ref, idx_ref, gate_ref, acc_i_ref, acc_g_ref, *, cap, jb):
    b = pl.program_id(0)
    e = pl.program_id(1)
    n_chunks, w = key_ref.shape
    n_sb = cap // jb
    base = (b * pl.num_programs(1) + e) * LANES
    shift = jb.bit_length() - 1
    acc_i_ref[...] = jnp.zeros_like(acc_i_ref)
    acc_g_ref[...] = jnp.zeros_like(acc_g_ref)

    def chunk(c, _):
        c_start = jnp.where(c > 0, cend_ref[base + jnp.maximum(c - 1, 0)], 0)
        c_end = cend_ref[base + c]

        @pl.when(c_end > c_start)
        def _any_selected():
            krow = key_ref[pl.ds(c, 1), :]
            arow = aff_ref[pl.ds(c, 1), :]
            tpos = (lax.broadcasted_iota(jnp.int32, (1, w), 1) + c * w).astype(F32)
            sb_lo = lax.shift_right_logical(c_start, shift)
            sb_hi = jnp.minimum(lax.shift_right_logical(c_end - 1, shift), n_sb - 1)

            def slot_block(sb, _):
                slot = (lax.broadcasted_iota(jnp.int32, (jb, 1), 0) + (sb * jb + 1)).astype(F32)
                hit = krow == slot
                pi = jnp.where(hit, tpos, 0.0)
                pg = jnp.where(hit, arow, 0.0)
                fi = pi[:, 0:LANES]
                fg = pg[:, 0:LANES]
                for k in range(1, w // LANES):
                    fi = fi + pi[:, k * LANES:(k + 1) * LANES]
                    fg = fg + pg[:, k * LANES:(k + 1) * LANES]
                acc_i_ref[sb] = acc_i_ref[sb] + fi
                acc_g_ref[sb] = acc_g_ref[sb] + fg
                return 0

            lax.fori_loop(sb_lo, sb_hi + 1, slot_block, 0)

        return 0

    lax.fori_loop(0, n_chunks, chunk, 0)
    for sb in range(n_sb):
        idx_ref[sb * jb:(sb + 1) * jb, :] = jnp.sum(acc_i_ref[sb], axis=-1, keepdims=True).astype(jnp.int32)
        gate_ref[sb * jb:(sb + 1) * jb, :] = jnp.sum(acc_g_ref[sb], axis=-1, keepdims=True)


def _route(aff_t, cap):
    bsz, n_exp, n = aff_t.shape
    jb = min(cap, LANES)
    w = min(n, ROUTE_CHUNK)
    key, cend = pl.pallas_call(
        functools.partial(_route_select_kernel, cap=cap),
        out_shape=(jax.ShapeDtypeStruct((bsz, n_exp, n), F32),
                   jax.ShapeDtypeStruct((bsz, n_exp, LANES), jnp.int32)),
        grid=(bsz,),
        in_specs=[pl.BlockSpec((None, n_exp, n), lambda b: (b, 0, 0))],
        out_specs=(pl.BlockSpec((None, n_exp, n), lambda b: (b, 0, 0)),
                   pl.BlockSpec((None, n_exp, LANES), lambda b: (b, 0, 0))),
        compiler_params=_cparams(("parallel",)),
        name="route_select",
    )(aff_t)
    chunked = pl.BlockSpec((None, None, n // w, w), lambda b, e, cend: (b, e, 0, 0))
    out_blk = pl.BlockSpec((None, None, cap, 1), lambda b, e, cend: (b, e, 0, 0))
    grid_spec = pltpu.PrefetchScalarGridSpec(
        num_scalar_prefetch=1,
        grid=(bsz, n_exp),
        in_specs=[chunked, chunked],
        out_specs=(out_blk, out_blk),
        scratch_shapes=[pltpu.VMEM((cap // jb, jb, LANES), F32)] * 2,
    )
    return pl.pallas_call(
        functools.partial(_route_compact_kernel, cap=cap, jb=jb),
        out_shape=(jax.ShapeDtypeStruct((bsz, n_exp, cap, 1), jnp.int32),
                   jax.ShapeDtypeStruct((bsz, n_exp, cap, 1), F32)),
        grid_spec=grid_spec,
        compiler_params=_cparams(("parallel", "parallel")),
        name="route_compact",
    )(cend.reshape(-1), key.reshape(bsz, n_exp, n // w, w), aff_t.reshape(bsz, n_exp, n // w, w))


GATHER_UNROLL = 8
EXPERT_ROW_CHUNKS = 2
GATHER_PARTS = 4 * EXPERT_ROW_CHUNKS


def _tile_copy(h_hbm, xs_ref, sem, b, tok, row):
    return pltpu.make_async_copy(
        h_hbm.at[b, pl.ds(pl.multiple_of(tok * SUBLANES, SUBLANES), SUBLANES), :],
        xs_ref.at[pl.ds(pl.multiple_of(row * SUBLANES, SUBLANES), SUBLANES), :],
        sem)


def _gather_start(idx_refs, h_refs, caps, xs_ref, sem, e, b, n_exp):
    row0 = 0
    for idx_ref, h_hbm, cap in zip(idx_refs, h_refs, caps):
        base = (b * n_exp + e) * cap
        unroll = math.gcd(cap, GATHER_UNROLL)

        def issue(g, _, idx_ref=idx_ref, h_hbm=h_hbm, base=base, unroll=unroll, row0=row0):
            for u in range(unroll):
                i = g * unroll + u
                _tile_copy(h_hbm, xs_ref, sem, b, idx_ref[base + i], row0 + i).start()
            return 0

        lax.fori_loop(0, cap // unroll, issue, 0)
        row0 += cap


def _gather_inline(idx_refs, h_refs, caps, xs_ref, sem, e, b, n_exp, part):
    row0 = 0
    for idx_ref, h_hbm, cap in zip(idx_refs, h_refs, caps):
        base = (b * n_exp + e) * cap
        for i in range(part, cap, GATHER_PARTS):
            _tile_copy(h_hbm, xs_ref, sem, b, idx_ref[base + i], row0 + i).start()
        row0 += cap


def _expert_kernel(*refs, caps, d):
    n_sets = len(caps)
    idx_refs = refs[:n_sets]
    h_refs = refs[n_sets:2 * n_sets]
    wg_ref, wu_ref, wd_ref = refs[2 * n_sets:2 * n_sets + 3]
    gate_refs = refs[2 * n_sets + 3:3 * n_sets + 3]
    ys_refs = refs[3 * n_sets + 3:4 * n_sets + 3]
    xs_a, xs_b, sem = refs[4 * n_sets + 3:]
    rows = sum(caps)
    e = pl.program_id(0)
    b = pl.program_id(1)
    n_exp = pl.num_programs(0)
    n_b = pl.num_programs(1)
    total = n_exp * n_b
    step = e * n_b + b
    slot = lax.rem(step, 2)

    def wait_all(buf, s):
        pltpu.make_async_copy(h_refs[0].at[b, pl.ds(0, rows * SUBLANES), :], buf, sem.at[s]).wait()

    @pl.when(step == 0)
    def _first():
        _gather_start(idx_refs, h_refs, caps, xs_a, sem.at[0], e, b, n_exp)

    nxt = jnp.where(step + 1 < total, step + 1, 0)
    e1 = lax.div(nxt, n_b)
    b1 = lax.rem(nxt, n_b)

    def body(cur, nxt_buf, s_cur, s_nxt):
        def start_part(part):
            _gather_inline(idx_refs, h_refs, caps, nxt_buf, sem.at[s_nxt], e1, b1, n_exp, part)

        wait_all(cur, s_cur)
        wg = wg_ref[...].astype(BF16)
        wu = wu_ref[...].astype(BF16)
        wd = wd_ref[...].astype(BF16)
        bounds = [0]
        for cap in caps:
            bounds.append(bounds[-1] + cap)
        chunk = rows // EXPERT_ROW_CHUNKS
        for c in range(EXPERT_ROW_CHUNKS):
            r0, r1 = c * chunk, (c + 1) * chunk
            x = jnp.concatenate(
                [cur[pl.ds(r0 * SUBLANES + j, chunk, stride=SUBLANES), :] for j in range(d // LANES)],
                axis=1).astype(BF16)
            start_part(4 * c)
            hid = _silu(jnp.dot(x, wg, preferred_element_type=F32))
            start_part(4 * c + 1)
            hid = hid * jnp.dot(x, wu, preferred_element_type=F32)
            start_part(4 * c + 2)
            y = jnp.dot(hid.astype(BF16), wd, preferred_element_type=F32)
            start_part(4 * c + 3)
            for s in range(n_sets):
                lo, hi = max(r0, bounds[s]), min(r1, bounds[s + 1])
                if lo < hi:
                    ys = y[lo - r0:hi - r0] * gate_refs[s][lo - bounds[s]:hi - bounds[s], :]
                    for j in range(d // LANES):
                        ys_refs[s][pl.ds((lo - bounds[s]) * SUBLANES + j, hi - lo, stride=SUBLANES), :] = (
                            ys[:, j * LANES:(j + 1) * LANES])

        @pl.when(step == total - 1)
        def _drain():
            wait_all(nxt_buf, s_nxt)

    @pl.when(slot == 0)
    def _even():
        body(xs_a, xs_b, 0, 1)

    @pl.when(slot == 1)
    def _odd():
        body(xs_b, xs_a, 1, 0)


def _experts(token_sets, wg, wu, wd, layer):
    caps = tuple(ts[3] for ts in token_sets)
    n_sets = len(token_sets)
    bsz = token_sets[0][1].shape[0]
    _, n_exp, d, f = wg.shape
    assert token_sets[0][1].shape[1] >= sum(caps) * SUBLANES
    wspec = lambda rows, cols: pl.BlockSpec((None, None, rows, cols), lambda e, b, *idx: (layer, e, 0, 0))
    grid_spec = pltpu.PrefetchScalarGridSpec(
        num_scalar_prefetch=n_sets,
        grid=(n_exp, bsz),
        in_specs=[pl.BlockSpec(memory_space=pl.ANY)] * n_sets + [wspec(d, f), wspec(d, f), wspec(f, d)] + [
            pl.BlockSpec((None, None, cap, 1), lambda e, b, *idx: (b, e, 0, 0)) for cap in caps],
        out_specs=tuple(pl.BlockSpec((None, None, cap * SUBLANES, LANES), lambda e, b, *idx: (b, e, 0, 0))
                        for cap in caps),
        scratch_shapes=[pltpu.VMEM((sum(caps) * SUBLANES, LANES), F32)] * 2 + [pltpu.SemaphoreType.DMA((2,))],
    )
    return pl.pallas_call(
        functools.partial(_expert_kernel, caps=caps, d=d),
        out_shape=tuple(jax.ShapeDtypeStruct((bsz, n_exp, cap * SUBLANES, LANES), F32) for cap in caps),
        grid_spec=grid_spec,
        compiler_params=_cparams(("arbitrary", "arbitrary")),
        name="experts",
    )(*[ts[0] for ts in token_sets], *[ts[1] for ts in token_sets], wg, wu, wd, *[ts[2] for ts in token_sets])


def _combine_kernel(idx_ref, ys_ref, o_hbm, acc_ref, sem, *, cap, unroll):
    b = pl.program_id(0)
    e = pl.program_id(1)
    n_exp = pl.num_programs(1)
    base = (b * n_exp + e) * cap

    @pl.when(e == 0)
    def _zero():
        acc_ref[...] = jnp.zeros_like(acc_ref)

    def rows(g, _):
        i0 = g * unroll
        toks = [pl.multiple_of(idx_ref[base + i0 + u] * SUBLANES, SUBLANES) for u in range(unroll)]
        sums = [acc_ref[pl.ds(toks[u], SUBLANES), :]
                + ys_ref[pl.ds(pl.multiple_of((i0 + u) * SUBLANES, SUBLANES), SUBLANES), :]
                for u in range(unroll)]
        for u in range(unroll):
            acc_ref[pl.ds(toks[u], SUBLANES), :] = sums[u]
        return 0

    lax.fori_loop(0, cap // unroll, rows, 0)

    @pl.when(e == n_exp - 1)
    def _flush():
        cp = pltpu.make_async_copy(acc_ref, o_hbm.at[b], sem)
        cp.start()
        cp.wait()


def _combine(idx_flat, ys_tiles, nt, cap):
    bsz, n_exp = ys_tiles.shape[:2]
    grid_spec = pltpu.PrefetchScalarGridSpec(
        num_scalar_prefetch=1,
        grid=(bsz, n_exp),
        in_specs=[pl.BlockSpec((None, None, cap * SUBLANES, LANES), lambda b, e, idx: (b, e, 0, 0))],
        out_specs=pl.BlockSpec(memory_space=pl.ANY),
        scratch_shapes=[pltpu.VMEM((nt * SUBLANES, LANES), F32), pltpu.SemaphoreType.DMA(())],
    )
    return pl.pallas_call(
        functools.partial(_combine_kernel, cap=cap, unroll=4),
        out_shape=jax.ShapeDtypeStruct((bsz, nt * SUBLANES, LANES), F32),
        grid_spec=grid_spec,
        compiler_params=_cparams(("arbitrary", "arbitrary")),
        name="combine",
    )(idx_flat, ys_tiles)


def _residual_kernel(x_ref, m_ref, g_ref, o_ref, *, tm):
    d = x_ref.shape[-1]
    moe = jnp.concatenate(
        [m_ref[pl.ds(j, tm, stride=SUBLANES), :] for j in range(d // LANES)], axis=1)
    o_ref[...] = x_ref[...] + g_ref[...] * moe


def _residual(x, moe_tiles, g2, tm):
    bsz, nt, d = x.shape
    return pl.pallas_call(
        functools.partial(_residual_kernel, tm=tm),
        out_shape=jax.ShapeDtypeStruct((bsz, nt, d), F32),
        grid=(bsz, nt // tm),
        in_specs=[
            pl.BlockSpec((None, tm, d), lambda b, i: (b, i, 0)),
            pl.BlockSpec((None, tm * SUBLANES, LANES), lambda b, i: (b, i, 0)),
            pl.BlockSpec((None, 1, d), lambda b, i: (b, 0, 0)),
        ],
        out_specs=pl.BlockSpec((None, tm, d), lambda b, i: (b, i, 0)),
        compiler_params=_cparams(("parallel", "parallel")),
        name="residual",
    )(x, moe_tiles, g2)


def _prepare_layer(l, w_in, sgu_norm, w_sgu, b_sgu, q_lora_norm, w_uq, kv_lora_norm, w_ukv, q_norm, k_norm,
                   w_out, w_router, w_gate, w_up, w_down):
    w_in_l = w_in[l]
    kr_cols = w_in_l[:, OFF_KR:OFF_KR + QK_ROPE]
    w_in_ext = jnp.concatenate([w_in_l[:, :OFF_KR], kr_cols, kr_cols], axis=1).astype(BF16)
    uq = w_uq[l].reshape(Q_LORA, MLA_HEADS, QK_DIM)
    w_uq_p = jnp.concatenate(
        [uq[:, :, :QK_NOPE].reshape(Q_LORA, MLA_HEADS * QK_NOPE),
         uq[:, :, QK_NOPE:].reshape(Q_LORA, MLA_HEADS * QK_ROPE)], axis=1).astype(BF16)

    def norm_pad(v):
        return jnp.concatenate([v[:QK_NOPE], v[QK_NOPE:], v[QK_NOPE:]])[None, :]

    wr_t = w_router[l].T
    wr_hi = wr_t.astype(BF16)
    wr_lo = (wr_t - wr_hi.astype(F32)).astype(BF16)
    return dict(
        w_in=w_in_ext,
        sgu_norm=sgu_norm[l][None, :],
        w_sgu=jnp.concatenate([w_sgu[l][hh] for hh in range(SGU_HEADS)], axis=1).astype(BF16),
        b_sgu=jnp.repeat(b_sgu[l].T, SGU_HEAD_DIM, axis=1),
        fdft=jnp.asarray(_channel_dft_table()).astype(BF16),
        q_lora_norm=q_lora_norm[l][None, :],
        w_uq=w_uq_p,
        q_norm=norm_pad(q_norm[l]),
        kv_lora_norm=kv_lora_norm[l][None, :],
        w_ukv=w_ukv[l].astype(BF16),
        k_norm=norm_pad(k_norm[l]),
        w_out=w_out[l].astype(BF16),
        wr_hi=wr_hi, wr_lo=wr_lo,
        w_gate=w_gate, w_up=w_up, w_down=w_down, layer=l,
    )


def _rope_tables(n):
    pos = np.arange(n)
    half = QK_ROPE // 4
    freqs = ROPE_THETA ** (-np.arange(half, dtype=np.float64) / half)
    ang_r = (pos // GRID_WIDTH)[:, None] * freqs
    ang_c = (pos % GRID_WIDTH)[:, None] * freqs
    cos64 = np.concatenate([np.cos(ang_r), np.cos(ang_r), np.cos(ang_c), np.cos(ang_c)], axis=1)
    sin64 = np.concatenate([-np.sin(ang_r), np.sin(ang_r), -np.sin(ang_c), np.sin(ang_c)], axis=1)
    return (jnp.asarray(np.tile(cos64, (1, 2)), F32), jnp.asarray(np.tile(sin64, (1, 2)), F32))


def _token_tile(nt, pref):
    return pref if nt % pref == 0 else nt


def _moe(populations, lw):
    token_sets = []
    for h2_tiles, aff_t in populations:
        nt = aff_t.shape[-1]
        cap = EC_FACTOR * nt // N_EXPERTS
        idx, gate = _route(aff_t, cap)
        token_sets.append((idx.reshape(-1), h2_tiles, gate, cap))
    ys = _experts(token_sets, lw["w_gate"], lw["w_up"], lw["w_down"], lw["layer"])
    return [_combine(ts[0], y, pop[1].shape[-1], ts[3]) for ts, y, pop in zip(token_sets, ys, populations)]


def kernel(x, c, ctx, c_ctx, w_ada, b_ada, w_in, sgu_norm, w_sgu, b_sgu, q_lora_norm, w_uq, kv_lora_norm, w_ukv,
           q_norm, k_norm, w_out, w_router, w_gate, w_up, w_down):
    bsz, n, d = x.shape
    n_ctx = ctx.shape[1]
    depth = w_ada.shape[0]
    assert bsz + 1 <= SUBLANES

    cond = jnp.zeros((SUBLANES, d), F32).at[:bsz].set(c).at[bsz].set(c_ctx)
    mod = _ada_modulation(cond, w_ada, b_ada)
    rope_tabs = _rope_tables(n)
    tm = _token_tile(n, 1024)
    tm_in = _token_tile(n, 1024)
    tm_c = _token_tile(n_ctx, 256)

    xc = ctx
    pending = pending_c = None
    for l in range(depth):
        last = l == depth - 1
        lw = _prepare_layer(l, w_in, sgu_norm, w_sgu, b_sgu, q_lora_norm, w_uq, kv_lora_norm, w_ukv,
                            q_norm, k_norm, w_out, w_router, w_gate, w_up, w_down)
        parts = [mod[l, :, i * d:(i + 1) * d] for i in range(6)]
        lat = [p[:bsz, None, :] for p in parts]
        cx = [jnp.broadcast_to(p[bsz][None, None, :], (bsz, 1, d)) for p in parts]

        outs_c = _mix_in(xc, cx[0], cx[1], lw, None, tm_c, pending_c)
        outs = _mix_in(x, lat[0], lat[1], lw, rope_tabs, tm_in, pending)
        ya_c, a_c, b_c, q_c, k_c, v_c = outs_c[:6]
        ya, a, b, q, k, v = outs[:6]
        if pending is not None:
            xc, x = outs_c[6], outs[6]
        yb = _fourier_positions(a, b)
        tk = _token_tile(n, ATTN_TK)
        vt_c = jnp.swapaxes(v_c, 2, 3)
        vt = jnp.swapaxes(v.reshape(bsz, MLA_HEADS, n // tk, tk, V_DIM), 3, 4)
        yc = _attention(jnp.swapaxes(q, 2, 3), k_c, vt_c, k, vt, tq=_token_tile(n, ATTN_TQ), tk=tk)
        x_new, h2_tiles, aff_t = _mix_out(x, ya, yb, yc, lw["w_out"], lat[2], lat[3], lat[4],
                                          lw["wr_hi"], lw["wr_lo"], tm)
        if not last:
            yb_c = _fourier_positions(a_c, b_c)
            yc_c = _attention(jnp.swapaxes(q_c, 2, 3), k_c, vt_c, None, None,
                              tq=_token_tile(n_ctx, ATTN_TQ), tk=n_ctx)
            xc_new, h2c_tiles, affc_t = _mix_out(xc, ya_c, yb_c, yc_c, lw["w_out"], cx[2], cx[3], cx[4],
                                                 lw["wr_hi"], lw["wr_lo"], tm_c)
            moe, moe_c = _moe([(h2_tiles, aff_t), (h2c_tiles, affc_t)], lw)
            xc, pending_c = xc_new, (moe_c, cx[5])
        else:
            moe, = _moe([(h2_tiles, aff_t)], lw)
        x, pending = x_new, (moe, lat[5])
    return _residual(x, pending[0], pending[1], tm)
```

```python
import functools
import math

import numpy as np
import jax
import jax.numpy as jnp
from jax import lax
from jax.experimental import pallas as pl
from jax.experimental.pallas import tpu as pltpu

F32 = jnp.float32
BF16 = jnp.bfloat16

GRID_WIDTH = 64
SGU_HEADS = 4
SGU_HEAD_DIM = 64
SGU_W = SGU_HEADS * SGU_HEAD_DIM
CHUNK = 128
FNET_GROUP_DIM = 64
FNET_W = 256
MLA_HEADS = 4
QK_NOPE = 128
QK_ROPE = 64
QK_DIM = QK_NOPE + QK_ROPE
V_DIM = 128
Q_LORA = 256
KV_LORA = 128
OFF_V = 256
OFF_F = 512
OFF_Q = 768
OFF_KV = 1024
OFF_KR = 1152
N_EXPERTS = 16
EC_FACTOR = 2
ROPE_THETA = 10000.0
EPS = 1e-6

LANES = 128
SUBLANES = 8
VMEM_LIMIT_BYTES = 56 * 1024 * 1024

QK_PAD = 2 * LANES
ATTN_TQ = 2048
ATTN_TK = 512
LOG2_E = 1.4426950408889634
MIX_ROWS = 128
MIX_OUT_ROWS = 512


def _cparams(sem, vmem=None):
    return pltpu.CompilerParams(dimension_semantics=sem, vmem_limit_bytes=vmem or VMEM_LIMIT_BYTES)


def _rms_scale(v, width):
    return lax.rsqrt(jnp.sum(v * v, axis=-1, keepdims=True) * (1.0 / width) + EPS)


def _gelu_tanh(v):
    return 0.5 * v * (1.0 + jnp.tanh(0.7978845608028654 * (v + 0.044715 * v * v * v)))


def _silu(v):
    return v / (1.0 + jnp.exp(-v))


def _ada_kernel(cond_ref, w_ref, b_ref, o_ref):
    s = _silu(cond_ref[...])
    o_ref[...] = jnp.dot(s, w_ref[...], preferred_element_type=F32,
                         precision=lax.Precision.HIGHEST) + b_ref[...]


def _ada_modulation(cond, w_ada, b_ada):
    depth, d, six_d = w_ada.shape
    tn = 1536
    return pl.pallas_call(
        _ada_kernel,
        out_shape=jax.ShapeDtypeStruct((depth, SUBLANES, six_d), F32),
        grid=(depth, six_d // tn),
        in_specs=[
            pl.BlockSpec((SUBLANES, d), lambda l, j: (0, 0)),
            pl.BlockSpec((None, d, tn), lambda l, j: (l, 0, j)),
            pl.BlockSpec((None, 1, tn), lambda l, j: (l, 0, j)),
        ],
        out_specs=pl.BlockSpec((None, SUBLANES, tn), lambda l, j: (l, 0, j)),
        compiler_params=_cparams(("arbitrary", "arbitrary")),
        name="ada_modulation",
    )(cond, w_ada, b_ada.reshape(depth, 1, six_d))


def _swap16(v):
    lane = lax.broadcasted_iota(jnp.int32, v.shape, 1)
    up = pltpu.roll(v, LANES - 16, axis=1)
    dn = pltpu.roll(v, 16, axis=1)
    return jnp.where((lane % 32) < 16, up, dn)


def _mix_in_kernel(*refs, tm, use_rope, fuse_residual):
    rows_per_group = min(tm, MIX_ROWS)
    for g in range(tm // rows_per_group):
        _mix_in_rows(slice(g * rows_per_group, (g + 1) * rows_per_group), refs, use_rope, fuse_residual)


def _mix_in_rows(rg, refs, use_rope, fuse_residual):
    if fuse_residual:
        x_ref, moe_ref, g2_ref = refs[:3]
        xo_ref = refs[-1]
        refs = refs[:1] + refs[3:-1]
    (x_ref, sh_ref, sc_ref, win_ref, sgun_ref, wsgu_ref, bsgu_ref, fdft_ref, qln_ref, wuq_ref, qn_ref,
     kvn_ref, wukv_ref, kn_ref, cos_ref, sin_ref, ya_ref, a_ref, b_ref, q_ref, k_ref, v_ref) = refs
    x = x_ref[rg, :]
    if fuse_residual:
        nrows = rg.stop - rg.start
        moe = jnp.concatenate(
            [moe_ref[pl.ds(rg.start * SUBLANES + j, nrows, stride=SUBLANES), :]
             for j in range(x.shape[-1] // LANES)], axis=1)
        x = x + g2_ref[...] * moe
        xo_ref[rg, :] = x
    tm = x.shape[0]
    d = x.shape[-1]
    h = x * _rms_scale(x, d) * (1.0 + sc_ref[...]) + sh_ref[...]
    p = jnp.dot(h.astype(BF16), win_ref[...], preferred_element_type=F32)

    u = _gelu_tanh(p[:, 0:OFF_V])
    gv = _gelu_tanh(p[:, OFF_V:OFF_F])
    vn = gv * _rms_scale(gv, SGU_W) * sgun_ref[...]
    head_of_lane = lax.broadcasted_iota(jnp.int32, (CHUNK, SGU_W), 1) // SGU_HEAD_DIM
    for c in range(tm // CHUNK):
        rows = slice(c * CHUNK, (c + 1) * CHUNK)
        vc = vn[rows]
        vstack = jnp.concatenate(
            [jnp.where(head_of_lane == hh, vc, 0.0) for hh in range(SGU_HEADS)], axis=0).astype(BF16)
        z = jnp.dot(wsgu_ref[...], vstack, preferred_element_type=F32) + bsgu_ref[...]
        ya_ref[rg.start + c * CHUNK:rg.start + (c + 1) * CHUNK, :] = (u[rows] * z).astype(ya_ref.dtype)

    ab = jnp.dot(p[:, OFF_F:OFF_Q].astype(BF16), fdft_ref[...], preferred_element_type=F32)
    a_ref[rg, :] = ab[:, 0:FNET_W].astype(a_ref.dtype)
    b_ref[rg, :] = ab[:, FNET_W:2 * FNET_W].astype(b_ref.dtype)

    lane = lax.broadcasted_iota(jnp.int32, (1, LANES), 1)
    lo = lane < QK_ROPE
    if use_rope:
        cos = cos_ref[rg, :]
        sin = sin_ref[rg, :]

    def rope(t):
        return t * cos + _swap16(t) * sin if use_rope else t

    pq = p[:, OFF_Q:OFF_KV]
    cq = pq * _rms_scale(pq, Q_LORA) * qln_ref[...]
    q = jnp.dot(cq.astype(BF16), wuq_ref[...], preferred_element_type=F32)
    qnorm = qn_ref[...]
    for hh in range(MLA_HEADS):
        qn = q[:, LANES * hh:LANES * (hh + 1)]
        pair = hh // 2
        qr = q[:, MLA_HEADS * LANES + LANES * pair:MLA_HEADS * LANES + LANES * (pair + 1)]
        qr = jnp.where(lo if hh % 2 == 0 else jnp.logical_not(lo), qr, 0.0)
        ssq = jnp.sum(qn * qn, axis=-1, keepdims=True) + jnp.sum(qr * qr, axis=-1, keepdims=True)
        rinv = lax.rsqrt(ssq * (1.0 / QK_DIM) + EPS) * (QK_DIM ** -0.5 * LOG2_E)
        q_ref[hh, rg, 0:LANES] = (qn * rinv * qnorm[:, 0:LANES]).astype(q_ref.dtype)
        q_ref[hh, rg, LANES:QK_PAD] = rope(qr * rinv * qnorm[:, LANES:QK_PAD]).astype(q_ref.dtype)

    pkv = p[:, OFF_KV:OFF_KR]
    ckv = pkv * _rms_scale(pkv, KV_LORA) * kvn_ref[...]
    kv = jnp.dot(ckv.astype(BF16), wukv_ref[...], preferred_element_type=F32)
    krr = p[:, OFF_KR:OFF_KR + LANES]
    ssq_kr = 0.5 * jnp.sum(krr * krr, axis=-1, keepdims=True)
    knorm = kn_ref[...]
    rinvs = []
    for hh in range(MLA_HEADS):
        kn = kv[:, 2 * LANES * hh:2 * LANES * hh + LANES]
        ssq = jnp.sum(kn * kn, axis=-1, keepdims=True) + ssq_kr
        rinv = lax.rsqrt(ssq * (1.0 / QK_DIM) + EPS)
        rinvs.append(rinv)
        k_ref[hh, rg, 0:LANES] = (kn * rinv * knorm[:, 0:LANES]).astype(k_ref.dtype)
        v_ref[hh, rg, :] = kv[:, 2 * LANES * hh + LANES:2 * LANES * (hh + 1)].astype(v_ref.dtype)
    for pair in range(MLA_HEADS // 2):
        r2 = jnp.where(lo, rinvs[2 * pair], rinvs[2 * pair + 1])
        kr = rope(krr * r2 * knorm[:, LANES:QK_PAD]).astype(k_ref.dtype)
        k_ref[2 * pair, rg, LANES:QK_PAD] = kr
        k_ref[2 * pair + 1, rg, LANES:QK_PAD] = kr


def _mix_in(x, sh, sc, lw, rope_tabs, tm, pending=None):
    bsz, nt, d = x.shape
    use_rope = rope_tabs is not None
    if use_rope:
        cos_t, sin_t = rope_tabs
    else:
        cos_t = sin_t = jnp.zeros((nt, LANES), F32)
    full = lambda arr: pl.BlockSpec(arr.shape, lambda b, i: (0,) * arr.ndim)
    vec = pl.BlockSpec((None, 1, d), lambda b, i: (b, 0, 0))
    in_w = lw["w_in"].shape[1]
    fuse = pending is not None
    kern = functools.partial(_mix_in_kernel, tm=tm, use_rope=use_rope, fuse_residual=fuse)
    row_blk = pl.BlockSpec((None, tm, d), lambda b, i: (b, i, 0))
    out_shape = [
        jax.ShapeDtypeStruct((bsz, nt, SGU_W), BF16),
        jax.ShapeDtypeStruct((bsz, nt, FNET_W), F32),
        jax.ShapeDtypeStruct((bsz, nt, FNET_W), F32),
        jax.ShapeDtypeStruct((bsz, MLA_HEADS, nt, QK_PAD), BF16),
        jax.ShapeDtypeStruct((bsz, MLA_HEADS, nt, QK_PAD), BF16),
        jax.ShapeDtypeStruct((bsz, MLA_HEADS, nt, V_DIM), BF16),
    ]
    out_specs = [
        pl.BlockSpec((None, tm, SGU_W), lambda b, i: (b, i, 0)),
        pl.BlockSpec((None, tm, FNET_W), lambda b, i: (b, i, 0)),
        pl.BlockSpec((None, tm, FNET_W), lambda b, i: (b, i, 0)),
        pl.BlockSpec((None, MLA_HEADS, tm, QK_PAD), lambda b, i: (b, 0, i, 0)),
        pl.BlockSpec((None, MLA_HEADS, tm, QK_PAD), lambda b, i: (b, 0, i, 0)),
        pl.BlockSpec((None, MLA_HEADS, tm, V_DIM), lambda b, i: (b, 0, i, 0)),
    ]
    in_specs = [row_blk]
    args = [x]
    if fuse:
        in_specs += [pl.BlockSpec((None, tm * SUBLANES, LANES), lambda b, i: (b, i, 0)), vec]
        args += list(pending)
        out_shape.append(jax.ShapeDtypeStruct((bsz, nt, d), F32))
        out_specs.append(row_blk)
    in_specs += [
        vec, vec,
        full(lw["w_in"]), full(lw["sgu_norm"]), full(lw["w_sgu"]), full(lw["b_sgu"]), full(lw["fdft"]),
        full(lw["q_lora_norm"]), full(lw["w_uq"]), full(lw["q_norm"]),
        full(lw["kv_lora_norm"]), full(lw["w_ukv"]), full(lw["k_norm"]),
        pl.BlockSpec((tm, LANES), lambda b, i: (i, 0)),
        pl.BlockSpec((tm, LANES), lambda b, i: (i, 0)),
    ]
    args += [sh, sc, lw["w_in"], lw["sgu_norm"], lw["w_sgu"], lw["b_sgu"], lw["fdft"],
             lw["q_lora_norm"], lw["w_uq"], lw["q_norm"], lw["kv_lora_norm"], lw["w_ukv"], lw["k_norm"],
             cos_t, sin_t]
    return pl.pallas_call(
        kern,
        out_shape=tuple(out_shape),
        grid=(bsz, nt // tm),
        in_specs=in_specs,
        out_specs=tuple(out_specs),
        compiler_params=_cparams(("parallel", "parallel")),
        name="mix_in",
    )(*args)


DFT_UNROLL = 16


def _dft_kernel(a_ref, b_ref, w_ref, m_ref, o_ref, yr_ref, yi_ref, *, n1, n2):
    w1 = w_ref[...]
    c = a_ref.shape[-1]

    u1 = math.gcd(n2, DFT_UNROLL)
    u2 = math.gcd(n1, DFT_UNROLL)

    def step1(g, _):
        cols = []
        for u in range(u1):
            rows = pl.ds(g * u1 + u, n1, stride=n2)
            cols.append(jnp.concatenate([a_ref[rows, :], b_ref[rows, :]], axis=0))
        r = jnp.dot(w1, jnp.concatenate(cols, axis=1).astype(BF16), preferred_element_type=F32)
        for u in range(u1):
            out_rows = pl.ds(pl.multiple_of((g * u1 + u) * n1, n1), n1)
            yr_ref[out_rows, :] = r[0:n1, u * c:(u + 1) * c]
            yi_ref[out_rows, :] = r[n1:2 * n1, u * c:(u + 1) * c]
        return 0

    lax.fori_loop(0, n2 // u1, step1, 0)

    def step2(g, _):
        for u in range(u2):
            k1 = g * u2 + u
            rows = pl.ds(k1, n2, stride=n1)
            ycat = jnp.concatenate([yr_ref[rows, :], yi_ref[rows, :]], axis=0).astype(BF16)
            o_ref[rows, :] = jnp.dot(m_ref[k1], ycat, preferred_element_type=F32)
        return 0

    lax.fori_loop(0, n1 // u2, step2, 0)


def _dft_factors(n):
    n1 = 64 if n >= 1024 else 32
    return n1, n // n1


@functools.lru_cache(maxsize=None)
def _dft_tables(n):
    n1, n2 = _dft_factors(n)
    j = np.arange(n1, dtype=np.float64)
    ang1 = 2.0 * np.pi * np.outer(j, j) / n1
    c1 = np.cos(ang1) / math.sqrt(n1)
    s1 = np.sin(ang1) / math.sqrt(n1)
    w1 = np.block([[c1, -s1], [s1, c1]])
    k1 = np.arange(n1, dtype=np.float64)[:, None, None]
    k2 = np.arange(n2, dtype=np.float64)[None, :, None]
    m2 = np.arange(n2, dtype=np.float64)[None, None, :]
    ang2 = 2.0 * np.pi * m2 * (k1 + n1 * k2) / n
    m2cat = np.concatenate([np.cos(ang2), -np.sin(ang2)], axis=-1) / math.sqrt(n2)
    return w1.astype(np.float32), m2cat.astype(np.float32)


@functools.lru_cache(maxsize=None)
def _channel_dft_table():
    j = np.arange(FNET_GROUP_DIM, dtype=np.float64)
    ang = 2.0 * np.pi * np.outer(j, j) / FNET_GROUP_DIM
    groups = FNET_W // FNET_GROUP_DIM
    cg = np.kron(np.eye(groups), np.cos(ang)) / math.sqrt(FNET_GROUP_DIM)
    sg = np.kron(np.eye(groups), np.sin(ang)) / math.sqrt(FNET_GROUP_DIM)
    return np.concatenate([cg, sg], axis=1).astype(np.float32)


def _fourier_positions(a, b):
    bsz, n, c = a.shape
    n1, n2 = _dft_factors(n)
    w1_np, m2_np = _dft_tables(n)
    w1 = jnp.asarray(w1_np).astype(BF16)
    m2cat = jnp.asarray(m2_np).astype(BF16)
    cb = LANES
    blk = pl.BlockSpec((None, n, cb), lambda bi, j: (bi, 0, j))
    return pl.pallas_call(
        functools.partial(_dft_kernel, n1=n1, n2=n2),
        out_shape=jax.ShapeDtypeStruct((bsz, n, c), F32),
        grid=(bsz, c // cb),
        in_specs=[blk, blk,
                  pl.BlockSpec((2 * n1, 2 * n1), lambda bi, j: (0, 0)),
                  pl.BlockSpec((n1, n2, 2 * n2), lambda bi, j: (0, 0, 0))],
        out_specs=blk,
        scratch_shapes=[pltpu.VMEM((n, cb), F32)] * 2,
        compiler_params=_cparams(("parallel", "parallel")),
        name="fourier",
    )(a, b, w1, m2cat)


NEG_BIG = -1e30


def _attn_kernel(*refs, tk, n_lat_blocks):
    if n_lat_blocks:
        qt_ref, kc_ref, vtc_ref, kl_ref, vtl_ref, o_ref, acc_ref, sa_ref, sb_ref = refs
    else:
        qt_ref, kc_ref, vtc_ref, o_ref, acc_ref = refs
    qt = qt_ref[...]
    tq = qt.shape[1]

    def scores(kj):
        return jnp.dot(kj, qt, preferred_element_type=F32)

    def consume(s, vtj, m, l, first):
        m_new = jnp.maximum(m, jnp.max(s, axis=0, keepdims=True))
        alpha = jnp.exp2(m - m_new)
        p = jnp.exp2(s - m_new)
        l_new = alpha * l + jnp.sum(p, axis=0, keepdims=True)
        pv = jnp.dot(vtj, p.astype(BF16), preferred_element_type=F32)
        acc_ref[...] = pv if first else alpha * acc_ref[...] + pv
        return m_new, l_new

    def lat_keys(j):
        return kl_ref[pl.ds(pl.multiple_of(j * tk, tk), tk), :]

    m0 = jnp.full((1, tq), NEG_BIG, F32)
    l0 = jnp.zeros((1, tq), F32)
    if n_lat_blocks:
        sa_ref[...] = scores(lat_keys(0))
    m, l = consume(scores(kc_ref[...]), vtc_ref[...], m0, l0, True)
    if n_lat_blocks:
        def body(i, carry):
            sb_ref[...] = scores(lat_keys(2 * i + 1))
            m, l = consume(sa_ref[...], vtl_ref[2 * i], carry[0], carry[1], False)
            sa_ref[...] = scores(lat_keys(2 * i + 2))
            return consume(sb_ref[...], vtl_ref[2 * i + 1], m, l, False)

        m, l = lax.fori_loop(0, n_lat_blocks // 2 - 1, body, (m, l))
        sb_ref[...] = scores(lat_keys(n_lat_blocks - 1))
        m, l = consume(sa_ref[...], vtl_ref[n_lat_blocks - 2], m, l, False)
        m, l = consume(sb_ref[...], vtl_ref[n_lat_blocks - 1], m, l, False)
    o_ref[...] = jnp.transpose(acc_ref[...] / l).astype(o_ref.dtype)


def _attention(qt, k_ctx, vt_ctx, k_lat, vt_lat, tq, tk):
    bsz, heads, _, nq = qt.shape
    m_ctx = k_ctx.shape[2]
    n_lat_blocks = 0 if k_lat is None else k_lat.shape[2] // tk
    in_specs = [
        pl.BlockSpec((None, None, QK_PAD, tq), lambda b, h, i: (b, h, 0, i)),
        pl.BlockSpec((None, None, m_ctx, QK_PAD), lambda b, h, i: (b, h, 0, 0)),
        pl.BlockSpec((None, None, V_DIM, m_ctx), lambda b, h, i: (b, h, 0, 0)),
    ]
    args = [qt, k_ctx, vt_ctx]
    if n_lat_blocks:
        in_specs += [
            pl.BlockSpec((None, None, k_lat.shape[2], QK_PAD), lambda b, h, i: (b, h, 0, 0)),
            pl.BlockSpec((None, None, n_lat_blocks, V_DIM, tk), lambda b, h, i: (b, h, 0, 0, 0)),
        ]
        args += [k_lat, vt_lat]
    return pl.pallas_call(
        functools.partial(_attn_kernel, tk=tk, n_lat_blocks=n_lat_blocks),
        out_shape=jax.ShapeDtypeStruct((bsz, nq, heads * V_DIM), BF16),
        grid=(bsz, heads, nq // tq),
        in_specs=in_specs,
        out_specs=pl.BlockSpec((None, tq, V_DIM), lambda b, h, i: (b, i, h)),
        scratch_shapes=[pltpu.VMEM((V_DIM, tq), F32)] + [pltpu.VMEM((tk, tq), F32)] * (2 if n_lat_blocks else 0),
        compiler_params=_cparams(("parallel", "parallel", "arbitrary")),
        name="attention",
    )(*args)


def _mix_out_kernel(x_ref, ya_ref, yb_ref, yc_ref, wo_ref, g1_ref, sh_ref, sc_ref, wrh_ref, wrl_ref,
                    xn_ref, h2_ref, aff_ref, *, tm):
    rows_per_group = min(tm, MIX_OUT_ROWS)
    for g in range(tm // rows_per_group):
        _mix_out_rows(g * rows_per_group, rows_per_group, x_ref, ya_ref, yb_ref, yc_ref, wo_ref, g1_ref, sh_ref,
                      sc_ref, wrh_ref, wrl_ref, xn_ref, h2_ref, aff_ref)


def _mix_out_rows(r0, nr, x_ref, ya_ref, yb_ref, yc_ref, wo_ref, g1_ref, sh_ref, sc_ref, wrh_ref, wrl_ref,
                  xn_ref, h2_ref, aff_ref):
    rg = slice(r0, r0 + nr)
    mix = jnp.dot(ya_ref[rg, :], wo_ref[0:SGU_W, :], preferred_element_type=F32)
    mix += jnp.dot(yb_ref[rg, :].astype(BF16), wo_ref[SGU_W:SGU_W + FNET_W, :], preferred_element_type=F32)
    mix += jnp.dot(yc_ref[rg, :], wo_ref[SGU_W + FNET_W:, :], preferred_element_type=F32)
    xn = x_ref[rg, :] + g1_ref[...] * mix
    xn_ref[rg, :] = xn
    d = xn.shape[-1]
    h2 = xn * _rms_scale(xn, d) * (1.0 + sc_ref[...]) + sh_ref[...]
    for j in range(d // LANES):
        h2_ref[pl.ds(r0 * SUBLANES + j, nr, stride=SUBLANES), :] = h2[:, j * LANES:(j + 1) * LANES]
    h_hi = h2.astype(BF16)
    h_lo = (h2 - h_hi.astype(F32)).astype(BF16)
    nt_dims = (((1,), (1,)), ((), ()))
    logits = lax.dot_general(wrh_ref[...], h_hi, nt_dims, preferred_element_type=F32)
    logits += lax.dot_general(wrh_ref[...], h_lo, nt_dims, preferred_element_type=F32)
    logits += lax.dot_general(wrl_ref[...], h_hi, nt_dims, preferred_element_type=F32)
    e = jnp.exp(logits - jnp.max(logits, axis=0, keepdims=True))
    aff_ref[:, rg] = e / jnp.sum(e, axis=0, keepdims=True)


def _mix_out(x, ya, yb, yc, w_out, g1, sh2, sc2, wr_hi, wr_lo, tm):
    bsz, nt, d = x.shape
    vec = pl.BlockSpec((None, 1, d), lambda b, i: (b, 0, 0))
    full = lambda arr: pl.BlockSpec(arr.shape, lambda b, i: (0,) * arr.ndim)
    return pl.pallas_call(
        functools.partial(_mix_out_kernel, tm=tm),
        out_shape=(
            jax.ShapeDtypeStruct((bsz, nt, d), F32),
            jax.ShapeDtypeStruct((bsz, nt * SUBLANES, LANES), F32),
            jax.ShapeDtypeStruct((bsz, N_EXPERTS, nt), F32),
        ),
        grid=(bsz, nt // tm),
        in_specs=[
            pl.BlockSpec((None, tm, d), lambda b, i: (b, i, 0)),
            pl.BlockSpec((None, tm, SGU_W), lambda b, i: (b, i, 0)),
            pl.BlockSpec((None, tm, FNET_W), lambda b, i: (b, i, 0)),
            pl.BlockSpec((None, tm, MLA_HEADS * V_DIM), lambda b, i: (b, i, 0)),
            full(w_out), vec, vec, vec, full(wr_hi), full(wr_lo),
        ],
        out_specs=(
            pl.BlockSpec((None, tm, d), lambda b, i: (b, i, 0)),
            pl.BlockSpec((None, tm * SUBLANES, LANES), lambda b, i: (b, i, 0)),
            pl.BlockSpec((None, N_EXPERTS, tm), lambda b, i: (b, 0, i)),
        ),
        compiler_params=_cparams(("parallel", "parallel")),
        name="mix_out",
    )(x, ya, yb, yc, w_out, g1, sh2, sc2, wr_hi, wr_lo)


def _prefix_count(mask, tri):
    e, n = mask.shape
    w = tri.shape[0]
    carry = jnp.zeros((e, 1), F32)
    outs = []
    ends = []
    for c in range(n // w):
        local = jnp.dot(mask[:, c * w:(c + 1) * w].astype(BF16), tri, preferred_element_type=F32)
        outs.append(local + carry)
        carry = carry + local[:, w - 1:w]
        ends.append(carry)
    return jnp.concatenate(outs, axis=1), ends


ROUTE_CHUNK = 2 * LANES


def _route_select_kernel(aff_ref, key_ref, cend_ref, *, cap):
    n_exp, n = aff_ref.shape
    w = min(n, ROUTE_CHUNK)
    aff = aff_ref[...]

    def step(i, thr_bits):
        cand = thr_bits | jnp.left_shift(jnp.int32(1), 30 - i)
        cnt = jnp.sum((aff >= pltpu.bitcast(cand, F32)).astype(jnp.int32), axis=-1, keepdims=True)
        return jnp.where(cnt >= cap, cand, thr_bits)

    thr = pltpu.bitcast(lax.fori_loop(0, 31, step, jnp.zeros((n_exp, 1), jnp.int32)), F32)
    gt = aff > thr
    eq = aff == thr
    need = cap - jnp.sum(gt.astype(jnp.int32), axis=-1, keepdims=True)
    r = lax.broadcasted_iota(jnp.int32, (w, w), 0)
    c = lax.broadcasted_iota(jnp.int32, (w, w), 1)
    tri = (r <= c).astype(BF16)
    eq_rank, _ = _prefix_count(eq.astype(F32), tri)
    sel = jnp.logical_or(gt, jnp.logical_and(eq, eq_rank <= need.astype(F32)))
    cum, ends = _prefix_count(sel.astype(F32), tri)
    key_ref[...] = jnp.where(sel, cum, 0.0)
    lane = lax.broadcasted_iota(jnp.int32, (n_exp, LANES), 1)
    cend = jnp.zeros((n_exp, LANES), F32)
    for ci, end in enumerate(ends):
        cend = jnp.where(lane == ci, end, cend)
    cend_ref[...] = cend.astype(jnp.int32)


def _route_compact_kernel(cend_ref, key_ref, aff_ref, idx_ref, gate_ref, acc_i_ref, acc_g_ref, *, cap, jb):
    b = pl.program_id(0)
    e = pl.program_id(1)
    n_chunks, w = key_ref.shape
    n_sb = cap // jb
    base = (b * pl.num_programs(1) + e) * LANES
    shift = jb.bit_length() - 1
    acc_i_ref[...] = jnp.zeros_like(acc_i_ref)
    acc_g_ref[...] = jnp.zeros_like(acc_g_ref)

    def chunk(c, _):
        c_start = jnp.where(c > 0, cend_ref[base + jnp.maximum(c - 1, 0)], 0)
        c_end = cend_ref[base + c]

        @pl.when(c_end > c_start)
        def _any_selected():
            krow = key_ref[pl.ds(c, 1), :]
            arow = aff_ref[pl.ds(c, 1), :]
            tpos = (lax.broadcasted_iota(jnp.int32, (1, w), 1) + c * w).astype(F32)
            sb_lo = lax.shift_right_logical(c_start, shift)
            sb_hi = jnp.minimum(lax.shift_right_logical(c_end - 1, shift), n_sb - 1)

            def slot_block(sb, _):
                slot = (lax.broadcasted_iota(jnp.int32, (jb, 1), 0) + (sb * jb + 1)).astype(F32)
                hit = krow == slot
                pi = jnp.where(hit, tpos, 0.0)
                pg = jnp.where(hit, arow, 0.0)
                fi = pi[:, 0:LANES]
                fg = pg[:, 0:LANES]
                for k in range(1, w // LANES):
                    fi = fi + pi[:, k * LANES:(k + 1) * LANES]
                    fg = fg + pg[:, k * LANES:(k + 1) * LANES]
                acc_i_ref[sb] = acc_i_ref[sb] + fi
                acc_g_ref[sb] = acc_g_ref[sb] + fg
                return 0

            lax.fori_loop(sb_lo, sb_hi + 1, slot_block, 0)

        return 0

    lax.fori_loop(0, n_chunks, chunk, 0)
    for sb in range(n_sb):
        idx_ref[sb * jb:(sb + 1) * jb, :] = jnp.sum(acc_i_ref[sb], axis=-1, keepdims=True).astype(jnp.int32)
        gate_ref[sb * jb:(sb + 1) * jb, :] = jnp.sum(acc_g_ref[sb], axis=-1, keepdims=True)


def _route(aff_t, cap):
    bsz, n_exp, n = aff_t.shape
    jb = min(cap, LANES)
    w = min(n, ROUTE_CHUNK)
    key, cend = pl.pallas_call(
        functools.partial(_route_select_kernel, cap=cap),
        out_shape=(jax.ShapeDtypeStruct((bsz, n_exp, n), F32),
                   jax.ShapeDtypeStruct((bsz, n_exp, LANES), jnp.int32)),
        grid=(bsz,),
        in_specs=[pl.BlockSpec((None, n_exp, n), lambda b: (b, 0, 0))],
        out_specs=(pl.BlockSpec((None, n_exp, n), lambda b: (b, 0, 0)),
                   pl.BlockSpec((None, n_exp, LANES), lambda b: (b, 0, 0))),
        compiler_params=_cparams(("parallel",)),
        name="route_select",
    )(aff_t)
    chunked = pl.BlockSpec((None, None, n // w, w), lambda b, e, cend: (b, e, 0, 0))
    out_blk = pl.BlockSpec((None, None, cap, 1), lambda b, e, cend: (b, e, 0, 0))
    grid_spec = pltpu.PrefetchScalarGridSpec(
        num_scalar_prefetch=1,
        grid=(bsz, n_exp),
        in_specs=[chunked, chunked],
        out_specs=(out_blk, out_blk),
        scratch_shapes=[pltpu.VMEM((cap // jb, jb, LANES), F32)] * 2,
    )
    return pl.pallas_call(
        functools.partial(_route_compact_kernel, cap=cap, jb=jb),
        out_shape=(jax.ShapeDtypeStruct((bsz, n_exp, cap, 1), jnp.int32),
                   jax.ShapeDtypeStruct((bsz, n_exp, cap, 1), F32)),
        grid_spec=grid_spec,
        compiler_params=_cparams(("parallel", "parallel")),
        name="route_compact",
    )(cend.reshape(-1), key.reshape(bsz, n_exp, n // w, w), aff_t.reshape(bsz, n_exp, n // w, w))


GATHER_UNROLL = 8
EXPERT_ROW_CHUNKS = 2
GATHER_PARTS = 4 * EXPERT_ROW_CHUNKS


def _tile_copy(h_hbm, xs_ref, sem, b, tok, row):
    return pltpu.make_async_copy(
        h_hbm.at[b, pl.ds(pl.multiple_of(tok * SUBLANES, SUBLANES), SUBLANES), :],
        xs_ref.at[pl.ds(pl.multiple_of(row * SUBLANES, SUBLANES), SUBLANES), :],
        sem)


def _gather_start(idx_refs, h_refs, caps, xs_ref, sem, e, b, n_exp):
    row0 = 0
    for idx_ref, h_hbm, cap in zip(idx_refs, h_refs, caps):
        base = (b * n_exp + e) * cap
        unroll = math.gcd(cap, GATHER_UNROLL)

        def issue(g, _, idx_ref=idx_ref, h_hbm=h_hbm, base=base, unroll=unroll, row0=row0):
            for u in range(unroll):
                i = g * unroll + u
                _tile_copy(h_hbm, xs_ref, sem, b, idx_ref[base + i], row0 + i).start(priority=u % 2)
            return 0

        lax.fori_loop(0, cap // unroll, issue, 0)
        row0 += cap


def _gather_inline(idx_refs, h_refs, caps, xs_ref, sem, e, b, n_exp, part):
    row0 = 0
    for idx_ref, h_hbm, cap in zip(idx_refs, h_refs, caps):
        base = (b * n_exp + e) * cap
        for k, i in enumerate(range(part, cap, GATHER_PARTS)):
            _tile_copy(h_hbm, xs_ref, sem, b, idx_ref[base + i], row0 + i).start(priority=k % 2)
        row0 += cap


def _expert_kernel(*refs, caps, d):
    n_sets = len(caps)
    idx_refs = refs[:n_sets]
    h_refs = refs[n_sets:2 * n_sets]
    wg_ref, wu_ref, wd_ref = refs[2 * n_sets:2 * n_sets + 3]
    gate_refs = refs[2 * n_sets + 3:3 * n_sets + 3]
    ys_refs = refs[3 * n_sets + 3:4 * n_sets + 3]
    xs_a, xs_b, sem = refs[4 * n_sets + 3:]
    rows = sum(caps)
    e = pl.program_id(0)
    b = pl.program_id(1)
    n_exp = pl.num_programs(0)
    n_b = pl.num_programs(1)
    total = n_exp * n_b
    step = e * n_b + b
    slot = lax.rem(step, 2)

    def wait_all(buf, s):
        pltpu.make_async_copy(h_refs[0].at[b, pl.ds(0, rows * SUBLANES), :], buf, sem.at[s]).wait()

    @pl.when(step == 0)
    def _first():
        _gather_start(idx_refs, h_refs, caps, xs_a, sem.at[0], e, b, n_exp)

    nxt = jnp.where(step + 1 < total, step + 1, 0)
    e1 = lax.div(nxt, n_b)
    b1 = lax.rem(nxt, n_b)

    def body(cur, nxt_buf, s_cur, s_nxt):
        def start_part(part):
            _gather_inline(idx_refs, h_refs, caps, nxt_buf, sem.at[s_nxt], e1, b1, n_exp, part)

        wait_all(cur, s_cur)
        wg = wg_ref[...].astype(BF16)
        wu = wu_ref[...].astype(BF16)
        wd = wd_ref[...].astype(BF16)
        bounds = [0]
        for cap in caps:
            bounds.append(bounds[-1] + cap)
        chunk = rows // EXPERT_ROW_CHUNKS
        for c in range(EXPERT_ROW_CHUNKS):
            r0, r1 = c * chunk, (c + 1) * chunk
            x = jnp.concatenate(
                [cur[pl.ds(r0 * SUBLANES + j, chunk, stride=SUBLANES), :] for j in range(d // LANES)],
                axis=1).astype(BF16)
            start_part(4 * c)
            hid = _silu(jnp.dot(x, wg, preferred_element_type=F32))
            start_part(4 * c + 1)
            hid = hid * jnp.dot(x, wu, preferred_element_type=F32)
            start_part(4 * c + 2)
            y = jnp.dot(hid.astype(BF16), wd, preferred_element_type=F32)
            start_part(4 * c + 3)
            for s in range(n_sets):
                lo, hi = max(r0, bounds[s]), min(r1, bounds[s + 1])
                if lo < hi:
                    ys = y[lo - r0:hi - r0] * gate_refs[s][lo - bounds[s]:hi - bounds[s], :]
                    for j in range(d // LANES):
                        ys_refs[s][pl.ds((lo - bounds[s]) * SUBLANES + j, hi - lo, stride=SUBLANES), :] = (
                            ys[:, j * LANES:(j + 1) * LANES])

        @pl.when(step == total - 1)
        def _drain():
            wait_all(nxt_buf, s_nxt)

    @pl.when(slot == 0)
    def _even():
        body(xs_a, xs_b, 0, 1)

    @pl.when(slot == 1)
    def _odd():
        body(xs_b, xs_a, 1, 0)


def _experts(token_sets, wg, wu, wd, layer):
    caps = tuple(ts[3] for ts in token_sets)
    n_sets = len(token_sets)
    bsz = token_sets[0][1].shape[0]
    _, n_exp, d, f = wg.shape
    assert token_sets[0][1].shape[1] >= sum(caps) * SUBLANES
    wspec = lambda rows, cols: pl.BlockSpec((None, None, rows, cols), lambda e, b, *idx: (layer, e, 0, 0))
    grid_spec = pltpu.PrefetchScalarGridSpec(
        num_scalar_prefetch=n_sets,
        grid=(n_exp, bsz),
        in_specs=[pl.BlockSpec(memory_space=pl.ANY)] * n_sets + [wspec(d, f), wspec(d, f), wspec(f, d)] + [
            pl.BlockSpec((None, None, cap, 1), lambda e, b, *idx: (b, e, 0, 0)) for cap in caps],
        out_specs=tuple(pl.BlockSpec((None, None, cap * SUBLANES, LANES), lambda e, b, *idx: (b, e, 0, 0))
                        for cap in caps),
        scratch_shapes=[pltpu.VMEM((sum(caps) * SUBLANES, LANES), F32)] * 2 + [pltpu.SemaphoreType.DMA((2,))],
    )
    return pl.pallas_call(
        functools.partial(_expert_kernel, caps=caps, d=d),
        out_shape=tuple(jax.ShapeDtypeStruct((bsz, n_exp, cap * SUBLANES, LANES), F32) for cap in caps),
        grid_spec=grid_spec,
        compiler_params=_cparams(("arbitrary", "arbitrary")),
        name="experts",
    )(*[ts[0] for ts in token_sets], *[ts[1] for ts in token_sets], wg, wu, wd, *[ts[2] for ts in token_sets])


def _combine_kernel(idx_ref, ys_ref, o_hbm, acc_ref, sem, *, cap, unroll):
    b = pl.program_id(0)
    e = pl.program_id(1)
    n_exp = pl.num_programs(1)
    base = (b * n_exp + e) * cap

    @pl.when(e == 0)
    def _zero():
        acc_ref[...] = jnp.zeros_like(acc_ref)

    def rows(g, _):
        i0 = g * unroll
        toks = [pl.multiple_of(idx_ref[base + i0 + u] * SUBLANES, SUBLANES) for u in range(unroll)]
        sums = [acc_ref[pl.ds(toks[u], SUBLANES), :]
                + ys_ref[pl.ds(pl.multiple_of((i0 + u) * SUBLANES, SUBLANES), SUBLANES), :]
                for u in range(unroll)]
        for u in range(unroll):
            acc_ref[pl.ds(toks[u], SUBLANES), :] = sums[u]
        return 0

    lax.fori_loop(0, cap // unroll, rows, 0)

    @pl.when(e == n_exp - 1)
    def _flush():
        cp = pltpu.make_async_copy(acc_ref, o_hbm.at[b], sem)
        cp.start()
        cp.wait()


def _combine(idx_flat, ys_tiles, nt, cap):
    bsz, n_exp = ys_tiles.shape[:2]
    grid_spec = pltpu.PrefetchScalarGridSpec(
        num_scalar_prefetch=1,
        grid=(bsz, n_exp),
        in_specs=[pl.BlockSpec((None, None, cap * SUBLANES, LANES), lambda b, e, idx: (b, e, 0, 0))],
        out_specs=pl.BlockSpec(memory_space=pl.ANY),
        scratch_shapes=[pltpu.VMEM((nt * SUBLANES, LANES), F32), pltpu.SemaphoreType.DMA(())],
    )
    return pl.pallas_call(
        functools.partial(_combine_kernel, cap=cap, unroll=4),
        out_shape=jax.ShapeDtypeStruct((bsz, nt * SUBLANES, LANES), F32),
        grid_spec=grid_spec,
        compiler_params=_cparams(("arbitrary", "arbitrary")),
        name="combine",
    )(idx_flat, ys_tiles)


def _residual_kernel(x_ref, m_ref, g_ref, o_ref, *, tm):
    d = x_ref.shape[-1]
    moe = jnp.concatenate(
        [m_ref[pl.ds(j, tm, stride=SUBLANES), :] for j in range(d // LANES)], axis=1)
    o_ref[...] = x_ref[...] + g_ref[...] * moe


def _residual(x, moe_tiles, g2, tm):
    bsz, nt, d = x.shape
    return pl.pallas_call(
        functools.partial(_residual_kernel, tm=tm),
        out_shape=jax.ShapeDtypeStruct((bsz, nt, d), F32),
        grid=(bsz, nt // tm),
        in_specs=[
            pl.BlockSpec((None, tm, d), lambda b, i: (b, i, 0)),
            pl.BlockSpec((None, tm * SUBLANES, LANES), lambda b, i: (b, i, 0)),
            pl.BlockSpec((None, 1, d), lambda b, i: (b, 0, 0)),
        ],
        out_specs=pl.BlockSpec((None, tm, d), lambda b, i: (b, i, 0)),
        compiler_params=_cparams(("parallel", "parallel")),
        name="residual",
    )(x, moe_tiles, g2)


def _prepare_layer(l, w_in, sgu_norm, w_sgu, b_sgu, q_lora_norm, w_uq, kv_lora_norm, w_ukv, q_norm, k_norm,
                   w_out, w_router, w_gate, w_up, w_down):
    w_in_l = w_in[l]
    kr_cols = w_in_l[:, OFF_KR:OFF_KR + QK_ROPE]
    w_in_ext = jnp.concatenate([w_in_l[:, :OFF_KR], kr_cols, kr_cols], axis=1).astype(BF16)
    uq = w_uq[l].reshape(Q_LORA, MLA_HEADS, QK_DIM)
    w_uq_p = jnp.concatenate(
        [uq[:, :, :QK_NOPE].reshape(Q_LORA, MLA_HEADS * QK_NOPE),
         uq[:, :, QK_NOPE:].reshape(Q_LORA, MLA_HEADS * QK_ROPE)], axis=1).astype(BF16)

    def norm_pad(v):
        return jnp.concatenate([v[:QK_NOPE], v[QK_NOPE:], v[QK_NOPE:]])[None, :]

    wr_t = w_router[l].T
    wr_hi = wr_t.astype(BF16)
    wr_lo = (wr_t - wr_hi.astype(F32)).astype(BF16)
    return dict(
        w_in=w_in_ext,
        sgu_norm=sgu_norm[l][None, :],
        w_sgu=jnp.concatenate([w_sgu[l][hh] for hh in range(SGU_HEADS)], axis=1).astype(BF16),
        b_sgu=jnp.repeat(b_sgu[l].T, SGU_HEAD_DIM, axis=1),
        fdft=jnp.asarray(_channel_dft_table()).astype(BF16),
        q_lora_norm=q_lora_norm[l][None, :],
        w_uq=w_uq_p,
        q_norm=norm_pad(q_norm[l]),
        kv_lora_norm=kv_lora_norm[l][None, :],
        w_ukv=w_ukv[l].astype(BF16),
        k_norm=norm_pad(k_norm[l]),
        w_out=w_out[l].astype(BF16),
        wr_hi=wr_hi, wr_lo=wr_lo,
        w_gate=w_gate, w_up=w_up, w_down=w_down, layer=l,
    )


def _rope_tables(n):
    pos = np.arange(n)
    half = QK_ROPE // 4
    freqs = ROPE_THETA ** (-np.arange(half, dtype=np.float64) / half)
    ang_r = (pos // GRID_WIDTH)[:, None] * freqs
    ang_c = (pos % GRID_WIDTH)[:, None] * freqs
    cos64 = np.concatenate([np.cos(ang_r), np.cos(ang_r), np.cos(ang_c), np.cos(ang_c)], axis=1)
    sin64 = np.concatenate([-np.sin(ang_r), np.sin(ang_r), -np.sin(ang_c), np.sin(ang_c)], axis=1)
    return (jnp.asarray(np.tile(cos64, (1, 2)), F32), jnp.asarray(np.tile(sin64, (1, 2)), F32))


def _token_tile(nt, pref):
    return pref if nt % pref == 0 else nt


def _moe(populations, lw):
    token_sets = []
    for h2_tiles, aff_t in populations:
        nt = aff_t.shape[-1]
        cap = EC_FACTOR * nt // N_EXPERTS
        idx, gate = _route(aff_t, cap)
        token_sets.append((idx.reshape(-1), h2_tiles, gate, cap))
    ys = _experts(token_sets, lw["w_gate"], lw["w_up"], lw["w_down"], lw["layer"])
    return [_combine(ts[0], y, pop[1].shape[-1], ts[3]) for ts, y, pop in zip(token_sets, ys, populations)]


def kernel(x, c, ctx, c_ctx, w_ada, b_ada, w_in, sgu_norm, w_sgu, b_sgu, q_lora_norm, w_uq, kv_lora_norm, w_ukv,
           q_norm, k_norm, w_out, w_router, w_gate, w_up, w_down):
    bsz, n, d = x.shape
    n_ctx = ctx.shape[1]
    depth = w_ada.shape[0]
    assert bsz + 1 <= SUBLANES

    cond = jnp.zeros((SUBLANES, d), F32).at[:bsz].set(c).at[bsz].set(c_ctx)
    mod = _ada_modulation(cond, w_ada, b_ada)
    rope_tabs = _rope_tables(n)
    tm = _token_tile(n, 1024)
    tm_in = _token_tile(n, 1024)
    tm_c = _token_tile(n_ctx, 256)

    xc = ctx
    pending = pending_c = None
    for l in range(depth):
        last = l == depth - 1
        lw = _prepare_layer(l, w_in, sgu_norm, w_sgu, b_sgu, q_lora_norm, w_uq, kv_lora_norm, w_ukv,
                            q_norm, k_norm, w_out, w_router, w_gate, w_up, w_down)
        parts = [mod[l, :, i * d:(i + 1) * d] for i in range(6)]
        lat = [p[:bsz, None, :] for p in parts]
        cx = [jnp.broadcast_to(p[bsz][None, None, :], (bsz, 1, d)) for p in parts]

        outs_c = _mix_in(xc, cx[0], cx[1], lw, None, tm_c, pending_c)
        outs = _mix_in(x, lat[0], lat[1], lw, rope_tabs, tm_in, pending)
        ya_c, a_c, b_c, q_c, k_c, v_c = outs_c[:6]
        ya, a, b, q, k, v = outs[:6]
        if pending is not None:
            xc, x = outs_c[6], outs[6]
        yb = _fourier_positions(a, b)
        tk = _token_tile(n, ATTN_TK)
        vt_c = jnp.swapaxes(v_c, 2, 3)
        vt = jnp.swapaxes(v.reshape(bsz, MLA_HEADS, n // tk, tk, V_DIM), 3, 4)
        yc = _attention(jnp.swapaxes(q, 2, 3), k_c, vt_c, k, vt, tq=_token_tile(n, ATTN_TQ), tk=tk)
        x_new, h2_tiles, aff_t = _mix_out(x, ya, yb, yc, lw["w_out"], lat[2], lat[3], lat[4],
                                          lw["wr_hi"], lw["wr_lo"], tm)
        if not last:
            yb_c = _fourier_positions(a_c, b_c)
            yc_c = _attention(jnp.swapaxes(q_c, 2, 3), k_c, vt_c, None, None,
                              tq=_token_tile(n_ctx, ATTN_TQ), tk=n_ctx)
            xc_new, h2c_tiles, affc_t = _mix_out(xc, ya_c, yb_c, yc_c, lw["w_out"], cx[2], cx[3], cx[4],
                                                 lw["wr_hi"], lw["wr_lo"], tm_c)
            moe, moe_c = _moe([(h2_tiles, aff_t), (h2c_tiles, affc_t)], lw)
            xc, pending_c = xc_new, (moe_c, cx[5])
        else:
            moe, = _moe([(h2_tiles, aff_t)], lw)
        x, pending = x_new, (moe, lat[5])
    return _residual(x, pending[0], pending[1], tm)
```

```python
import functools
import math

import numpy as np
import jax
import jax.numpy as jnp
from jax import lax
from jax.experimental import pallas as pl
from jax.experimental.pallas import tpu as pltpu

F32 = jnp.float32
BF16 = jnp.bfloat16

GRID_WIDTH = 64
SGU_HEADS = 4
SGU_HEAD_DIM = 64
SGU_W = SGU_HEADS * SGU_HEAD_DIM
CHUNK = 128
FNET_GROUP_DIM = 64
FNET_W = 256
MLA_HEADS = 4
QK_NOPE = 128
QK_ROPE = 64
QK_DIM = QK_NOPE + QK_ROPE
V_DIM = 128
Q_LORA = 256
KV_LORA = 128
OFF_V = 256
OFF_F = 512
OFF_Q = 768
OFF_KV = 1024
OFF_KR = 1152
N_EXPERTS = 16
EC_FACTOR = 2
ROPE_THETA = 10000.0
EPS = 1e-6

LANES = 128
SUBLANES = 8
VMEM_LIMIT_BYTES = 56 * 1024 * 1024

QK_PAD = 2 * LANES
ATTN_TQ = 4096
ATTN_TK = 512
LOG2_E = 1.4426950408889634
MIX_ROWS = 128
MIX_OUT_ROWS = 512


def _cparams(sem, vmem=None):
    return pltpu.CompilerParams(dimension_semantics=sem, vmem_limit_bytes=vmem or VMEM_LIMIT_BYTES)


def _rms_scale(v, width):
    return lax.rsqrt(jnp.sum(v * v, axis=-1, keepdims=True) * (1.0 / width) + EPS)


def _gelu_tanh(v):
    return 0.5 * v * (1.0 + jnp.tanh(0.7978845608028654 * (v + 0.044715 * v * v * v)))


def _silu(v):
    return v / (1.0 + jnp.exp(-v))


def _ada_kernel(cond_ref, w_ref, b_ref, o_ref):
    s = _silu(cond_ref[...])
    o_ref[...] = jnp.dot(s, w_ref[...], preferred_element_type=F32,
                         precision=lax.Precision.HIGHEST) + b_ref[...]


def _ada_modulation(cond, w_ada, b_ada):
    depth, d, six_d = w_ada.shape
    tn = 1536
    return pl.pallas_call(
        _ada_kernel,
        out_shape=jax.ShapeDtypeStruct((depth, SUBLANES, six_d), F32),
        grid=(depth, six_d // tn),
        in_specs=[
            pl.BlockSpec((SUBLANES, d), lambda l, j: (0, 0)),
            pl.BlockSpec((None, d, tn), lambda l, j: (l, 0, j)),
            pl.BlockSpec((None, 1, tn), lambda l, j: (l, 0, j)),
        ],
        out_specs=pl.BlockSpec((None, SUBLANES, tn), lambda l, j: (l, 0, j)),
        compiler_params=_cparams(("arbitrary", "arbitrary")),
        name="ada_modulation",
    )(cond, w_ada, b_ada.reshape(depth, 1, six_d))


def _swap16(v):
    lane = lax.broadcasted_iota(jnp.int32, v.shape, 1)
    up = pltpu.roll(v, LANES - 16, axis=1)
    dn = pltpu.roll(v, 16, axis=1)
    return jnp.where((lane % 32) < 16, up, dn)


def _mix_in_kernel(*refs, tm, use_rope, fuse_residual):
    rows_per_group = min(tm, MIX_ROWS)
    for g in range(tm // rows_per_group):
        _mix_in_rows(slice(g * rows_per_group, (g + 1) * rows_per_group), refs, use_rope, fuse_residual)


def _mix_in_rows(rg, refs, use_rope, fuse_residual):
    if fuse_residual:
        x_ref, moe_ref, g2_ref = refs[:3]
        xo_ref = refs[-1]
        refs = refs[:1] + refs[3:-1]
    (x_ref, sh_ref, sc_ref, win_ref, sgun_ref, wsgu_ref, bsgu_ref, fdft_ref, qln_ref, wuq_ref, qn_ref,
     kvn_ref, wukv_ref, kn_ref, cos_ref, sin_ref, ya_ref, a_ref, b_ref, q_ref, k_ref, v_ref) = refs
    x = x_ref[rg, :]
    if fuse_residual:
        nrows = rg.stop - rg.start
        moe = jnp.concatenate(
            [moe_ref[pl.ds(rg.start * SUBLANES + j, nrows, stride=SUBLANES), :]
             for j in range(x.shape[-1] // LANES)], axis=1)
        x = x + g2_ref[...] * moe
        xo_ref[rg, :] = x
    tm = x.shape[0]
    d = x.shape[-1]
    h = x * _rms_scale(x, d) * (1.0 + sc_ref[...]) + sh_ref[...]
    p = jnp.dot(h.astype(BF16), win_ref[...], preferred_element_type=F32)

    u = _gelu_tanh(p[:, 0:OFF_V])
    gv = _gelu_tanh(p[:, OFF_V:OFF_F])
    vn = gv * _rms_scale(gv, SGU_W) * sgun_ref[...]
    head_of_lane = lax.broadcasted_iota(jnp.int32, (CHUNK, SGU_W), 1) // SGU_HEAD_DIM
    for c in range(tm // CHUNK):
        rows = slice(c * CHUNK, (c + 1) * CHUNK)
        vc = vn[rows]
        vstack = jnp.concatenate(
            [jnp.where(head_of_lane == hh, vc, 0.0) for hh in range(SGU_HEADS)], axis=0).astype(BF16)
        z = jnp.dot(wsgu_ref[...], vstack, preferred_element_type=F32) + bsgu_ref[...]
        ya_ref[rg.start + c * CHUNK:rg.start + (c + 1) * CHUNK, :] = (u[rows] * z).astype(ya_ref.dtype)

    ab = jnp.dot(p[:, OFF_F:OFF_Q].astype(BF16), fdft_ref[...], preferred_element_type=F32)
    a_ref[rg, :] = ab[:, 0:FNET_W].astype(a_ref.dtype)
    b_ref[rg, :] = ab[:, FNET_W:2 * FNET_W].astype(b_ref.dtype)

    lane = lax.broadcasted_iota(jnp.int32, (1, LANES), 1)
    lo = lane < QK_ROPE
    if use_rope:
        cos = cos_ref[rg, :]
        sin = sin_ref[rg, :]

    def rope(t):
        return t * cos + _swap16(t) * sin if use_rope else t

    pq = p[:, OFF_Q:OFF_KV]
    cq = pq * _rms_scale(pq, Q_LORA) * qln_ref[...]
    q = jnp.dot(cq.astype(BF16), wuq_ref[...], preferred_element_type=F32)
    qnorm = qn_ref[...]
    for hh in range(MLA_HEADS):
        qn = q[:, LANES * hh:LANES * (hh + 1)]
        pair = hh // 2
        qr = q[:, MLA_HEADS * LANES + LANES * pair:MLA_HEADS * LANES + LANES * (pair + 1)]
        qr = jnp.where(lo if hh % 2 == 0 else jnp.logical_not(lo), qr, 0.0)
        ssq = jnp.sum(qn * qn, axis=-1, keepdims=True) + jnp.sum(qr * qr, axis=-1, keepdims=True)
        rinv = lax.rsqrt(ssq * (1.0 / QK_DIM) + EPS) * (QK_DIM ** -0.5 * LOG2_E)
        q_ref[hh, rg, 0:LANES] = (qn * rinv * qnorm[:, 0:LANES]).astype(q_ref.dtype)
        q_ref[hh, rg, LANES:QK_PAD] = rope(qr * rinv * qnorm[:, LANES:QK_PAD]).astype(q_ref.dtype)

    pkv = p[:, OFF_KV:OFF_KR]
    ckv = pkv * _rms_scale(pkv, KV_LORA) * kvn_ref[...]
    kv = jnp.dot(ckv.astype(BF16), wukv_ref[...], preferred_element_type=F32)
    krr = p[:, OFF_KR:OFF_KR + LANES]
    ssq_kr = 0.5 * jnp.sum(krr * krr, axis=-1, keepdims=True)
    knorm = kn_ref[...]
    rinvs = []
    for hh in range(MLA_HEADS):
        kn = kv[:, 2 * LANES * hh:2 * LANES * hh + LANES]
        ssq = jnp.sum(kn * kn, axis=-1, keepdims=True) + ssq_kr
        rinv = lax.rsqrt(ssq * (1.0 / QK_DIM) + EPS)
        rinvs.append(rinv)
        k_ref[hh, rg, 0:LANES] = (kn * rinv * knorm[:, 0:LANES]).astype(k_ref.dtype)
        v_ref[hh, rg, :] = kv[:, 2 * LANES * hh + LANES:2 * LANES * (hh + 1)].astype(v_ref.dtype)
    for pair in range(MLA_HEADS // 2):
        r2 = jnp.where(lo, rinvs[2 * pair], rinvs[2 * pair + 1])
        kr = rope(krr * r2 * knorm[:, LANES:QK_PAD]).astype(k_ref.dtype)
        k_ref[2 * pair, rg, LANES:QK_PAD] = kr
        k_ref[2 * pair + 1, rg, LANES:QK_PAD] = kr


def _mix_in(x, sh, sc, lw, rope_tabs, tm, pending=None):
    bsz, nt, d = x.shape
    use_rope = rope_tabs is not None
    if use_rope:
        cos_t, sin_t = rope_tabs
    else:
        cos_t = sin_t = jnp.zeros((nt, LANES), F32)
    full = lambda arr: pl.BlockSpec(arr.shape, lambda b, i: (0,) * arr.ndim)
    vec = pl.BlockSpec((None, 1, d), lambda b, i: (b, 0, 0))
    in_w = lw["w_in"].shape[1]
    fuse = pending is not None
    kern = functools.partial(_mix_in_kernel, tm=tm, use_rope=use_rope, fuse_residual=fuse)
    row_blk = pl.BlockSpec((None, tm, d), lambda b, i: (b, i, 0))
    out_shape = [
        jax.ShapeDtypeStruct((bsz, nt, SGU_W), BF16),
        jax.ShapeDtypeStruct((bsz, nt, FNET_W), F32),
        jax.ShapeDtypeStruct((bsz, nt, FNET_W), F32),
        jax.ShapeDtypeStruct((bsz, MLA_HEADS, nt, QK_PAD), BF16),
        jax.ShapeDtypeStruct((bsz, MLA_HEADS, nt, QK_PAD), BF16),
        jax.ShapeDtypeStruct((bsz, MLA_HEADS, nt, V_DIM), BF16),
    ]
    out_specs = [
        pl.BlockSpec((None, tm, SGU_W), lambda b, i: (b, i, 0)),
        pl.BlockSpec((None, tm, FNET_W), lambda b, i: (b, i, 0)),
        pl.BlockSpec((None, tm, FNET_W), lambda b, i: (b, i, 0)),
        pl.BlockSpec((None, MLA_HEADS, tm, QK_PAD), lambda b, i: (b, 0, i, 0)),
        pl.BlockSpec((None, MLA_HEADS, tm, QK_PAD), lambda b, i: (b, 0, i, 0)),
        pl.BlockSpec((None, MLA_HEADS, tm, V_DIM), lambda b, i: (b, 0, i, 0)),
    ]
    in_specs = [row_blk]
    args = [x]
    if fuse:
        in_specs += [pl.BlockSpec((None, tm * SUBLANES, LANES), lambda b, i: (b, i, 0)), vec]
        args += list(pending)
        out_shape.append(jax.ShapeDtypeStruct((bsz, nt, d), F32))
        out_specs.append(row_blk)
    in_specs += [
        vec, vec,
        full(lw["w_in"]), full(lw["sgu_norm"]), full(lw["w_sgu"]), full(lw["b_sgu"]), full(lw["fdft"]),
        full(lw["q_lora_norm"]), full(lw["w_uq"]), full(lw["q_norm"]),
        full(lw["kv_lora_norm"]), full(lw["w_ukv"]), full(lw["k_norm"]),
        pl.BlockSpec((tm, LANES), lambda b, i: (i, 0)),
        pl.BlockSpec((tm, LANES), lambda b, i: (i, 0)),
    ]
    args += [sh, sc, lw["w_in"], lw["sgu_norm"], lw["w_sgu"], lw["b_sgu"], lw["fdft"],
             lw["q_lora_norm"], lw["w_uq"], lw["q_norm"], lw["kv_lora_norm"], lw["w_ukv"], lw["k_norm"],
             cos_t, sin_t]
    return pl.pallas_call(
        kern,
        out_shape=tuple(out_shape),
        grid=(bsz, nt // tm),
        in_specs=in_specs,
        out_specs=tuple(out_specs),
        compiler_params=_cparams(("parallel", "parallel")),
        name="mix_in",
    )(*args)


DFT_UNROLL = 16


def _dft_kernel(a_ref, b_ref, w_ref, m_ref, o_ref, yr_ref, yi_ref, *, n1, n2):
    w1 = w_ref[...]
    c = a_ref.shape[-1]

    u1 = math.gcd(n2, DFT_UNROLL)
    u2 = math.gcd(n1, DFT_UNROLL)

    def step1(g, _):
        cols = []
        for u in range(u1):
            rows = pl.ds(g * u1 + u, n1, stride=n2)
            cols.append(jnp.concatenate([a_ref[rows, :], b_ref[rows, :]], axis=0))
        r = jnp.dot(w1, jnp.concatenate(cols, axis=1).astype(BF16), preferred_element_type=F32)
        for u in range(u1):
            out_rows = pl.ds(pl.multiple_of((g * u1 + u) * n1, n1), n1)
            yr_ref[out_rows, :] = r[0:n1, u * c:(u + 1) * c]
            yi_ref[out_rows, :] = r[n1:2 * n1, u * c:(u + 1) * c]
        return 0

    lax.fori_loop(0, n2 // u1, step1, 0)

    def step2(g, _):
        for u in range(u2):
            k1 = g * u2 + u
            rows = pl.ds(k1, n2, stride=n1)
            ycat = jnp.concatenate([yr_ref[rows, :], yi_ref[rows, :]], axis=0).astype(BF16)
            o_ref[rows, :] = jnp.dot(m_ref[k1], ycat, preferred_element_type=F32)
        return 0

    lax.fori_loop(0, n1 // u2, step2, 0)


def _dft_factors(n):
    n1 = 64 if n >= 1024 else 32
    return n1, n // n1


@functools.lru_cache(maxsize=None)
def _dft_tables(n):
    n1, n2 = _dft_factors(n)
    j = np.arange(n1, dtype=np.float64)
    ang1 = 2.0 * np.pi * np.outer(j, j) / n1
    c1 = np.cos(ang1) / math.sqrt(n1)
    s1 = np.sin(ang1) / math.sqrt(n1)
    w1 = np.block([[c1, -s1], [s1, c1]])
    k1 = np.arange(n1, dtype=np.float64)[:, None, None]
    k2 = np.arange(n2, dtype=np.float64)[None, :, None]
    m2 = np.arange(n2, dtype=np.float64)[None, None, :]
    ang2 = 2.0 * np.pi * m2 * (k1 + n1 * k2) / n
    m2cat = np.concatenate([np.cos(ang2), -np.sin(ang2)], axis=-1) / math.sqrt(n2)
    return w1.astype(np.float32), m2cat.astype(np.float32)


@functools.lru_cache(maxsize=None)
def _channel_dft_table():
    j = np.arange(FNET_GROUP_DIM, dtype=np.float64)
    ang = 2.0 * np.pi * np.outer(j, j) / FNET_GROUP_DIM
    groups = FNET_W // FNET_GROUP_DIM
    cg = np.kron(np.eye(groups), np.cos(ang)) / math.sqrt(FNET_GROUP_DIM)
    sg = np.kron(np.eye(groups), np.sin(ang)) / math.sqrt(FNET_GROUP_DIM)
    return np.concatenate([cg, sg], axis=1).astype(np.float32)


def _fourier_positions(a, b):
    bsz, n, c = a.shape
    n1, n2 = _dft_factors(n)
    w1_np, m2_np = _dft_tables(n)
    w1 = jnp.asarray(w1_np).astype(BF16)
    m2cat = jnp.asarray(m2_np).astype(BF16)
    cb = LANES
    blk = pl.BlockSpec((None, n, cb), lambda bi, j: (bi, 0, j))
    return pl.pallas_call(
        functools.partial(_dft_kernel, n1=n1, n2=n2),
        out_shape=jax.ShapeDtypeStruct((bsz, n, c), F32),
        grid=(bsz, c // cb),
        in_specs=[blk, blk,
                  pl.BlockSpec((2 * n1, 2 * n1), lambda bi, j: (0, 0)),
                  pl.BlockSpec((n1, n2, 2 * n2), lambda bi, j: (0, 0, 0))],
        out_specs=blk,
        scratch_shapes=[pltpu.VMEM((n, cb), F32)] * 2,
        compiler_params=_cparams(("parallel", "parallel")),
        name="fourier",
    )(a, b, w1, m2cat)


NEG_BIG = -1e30


def _attn_kernel(*refs, tk, n_lat_blocks):
    if n_lat_blocks:
        qt_ref, kc_ref, vtc_ref, kl_ref, vtl_ref, o_ref, acc_ref, sa_ref, sb_ref = refs
    else:
        qt_ref, kc_ref, vtc_ref, o_ref, acc_ref = refs
    qt = qt_ref[...]
    tq = qt.shape[1]

    def scores(kj):
        return jnp.dot(kj, qt, preferred_element_type=F32)

    def consume(s, vtj, m, l, first):
        m_new = jnp.maximum(m, jnp.max(s, axis=0, keepdims=True))
        alpha = jnp.exp2(m - m_new)
        p = jnp.exp2(s - m_new)
        l_new = alpha * l + jnp.sum(p, axis=0, keepdims=True)
        pv = jnp.dot(vtj, p.astype(BF16), preferred_element_type=F32)
        acc_ref[...] = pv if first else alpha * acc_ref[...] + pv
        return m_new, l_new

    def lat_keys(j):
        return kl_ref[pl.ds(pl.multiple_of(j * tk, tk), tk), :]

    m0 = jnp.full((1, tq), NEG_BIG, F32)
    l0 = jnp.zeros((1, tq), F32)
    if n_lat_blocks:
        sa_ref[...] = scores(lat_keys(0))
    m, l = consume(scores(kc_ref[...]), vtc_ref[...], m0, l0, True)
    if n_lat_blocks:
        def body(i, carry):
            sb_ref[...] = scores(lat_keys(2 * i + 1))
            m, l = consume(sa_ref[...], vtl_ref[2 * i], carry[0], carry[1], False)
            sa_ref[...] = scores(lat_keys(2 * i + 2))
            return consume(sb_ref[...], vtl_ref[2 * i + 1], m, l, False)

        m, l = lax.fori_loop(0, n_lat_blocks // 2 - 1, body, (m, l))
        sb_ref[...] = scores(lat_keys(n_lat_blocks - 1))
        m, l = consume(sa_ref[...], vtl_ref[n_lat_blocks - 2], m, l, False)
        m, l = consume(sb_ref[...], vtl_ref[n_lat_blocks - 1], m, l, False)
    o_ref[...] = jnp.transpose(acc_ref[...] / l).astype(o_ref.dtype)


def _attention(qt, k_ctx, vt_ctx, k_lat, vt_lat, tq, tk):
    bsz, heads, _, nq = qt.shape
    m_ctx = k_ctx.shape[2]
    n_lat_blocks = 0 if k_lat is None else k_lat.shape[2] // tk
    in_specs = [
        pl.BlockSpec((None, None, QK_PAD, tq), lambda b, h, i: (b, h, 0, i)),
        pl.BlockSpec((None, None, m_ctx, QK_PAD), lambda b, h, i: (b, h, 0, 0)),
        pl.BlockSpec((None, None, V_DIM, m_ctx), lambda b, h, i: (b, h, 0, 0)),
    ]
    args = [qt, k_ctx, vt_ctx]
    if n_lat_blocks:
        in_specs += [
            pl.BlockSpec((None, None, k_lat.shape[2], QK_PAD), lambda b, h, i: (b, h, 0, 0)),
            pl.BlockSpec((None, None, n_lat_blocks, V_DIM, tk), lambda b, h, i: (b, h, 0, 0, 0)),
        ]
        args += [k_lat, vt_lat]
    return pl.pallas_call(
        functools.partial(_attn_kernel, tk=tk, n_lat_blocks=n_lat_blocks),
        out_shape=jax.ShapeDtypeStruct((bsz, nq, heads * V_DIM), BF16),
        grid=(bsz, heads, nq // tq),
        in_specs=in_specs,
        out_specs=pl.BlockSpec((None, tq, V_DIM), lambda b, h, i: (b, i, h)),
        scratch_shapes=[pltpu.VMEM((V_DIM, tq), F32)] + [pltpu.VMEM((tk, tq), F32)] * (2 if n_lat_blocks else 0),
        compiler_params=_cparams(("parallel", "parallel", "arbitrary")),
        name="attention",
    )(*args)


def _mix_out_kernel(x_ref, ya_ref, yb_ref, yc_ref, wo_ref, g1_ref, sh_ref, sc_ref, wrh_ref, wrl_ref,
                    xn_ref, h2_ref, aff_ref, *, tm):
    rows_per_group = min(tm, MIX_OUT_ROWS)
    for g in range(tm // rows_per_group):
        _mix_out_rows(g * rows_per_group, rows_per_group, x_ref, ya_ref, yb_ref, yc_ref, wo_ref, g1_ref, sh_ref,
                      sc_ref, wrh_ref, wrl_ref, xn_ref, h2_ref, aff_ref)


def _mix_out_rows(r0, nr, x_ref, ya_ref, yb_ref, yc_ref, wo_ref, g1_ref, sh_ref, sc_ref, wrh_ref, wrl_ref,
                  xn_ref, h2_ref, aff_ref):
    rg = slice(r0, r0 + nr)
    mix = jnp.dot(ya_ref[rg, :], wo_ref[0:SGU_W, :], preferred_element_type=F32)
    mix += jnp.dot(yb_ref[rg, :].astype(BF16), wo_ref[SGU_W:SGU_W + FNET_W, :], preferred_element_type=F32)
    mix += jnp.dot(yc_ref[rg, :], wo_ref[SGU_W + FNET_W:, :], preferred_element_type=F32)
    xn = x_ref[rg, :] + g1_ref[...] * mix
    xn_ref[rg, :] = xn
    d = xn.shape[-1]
    h2 = xn * _rms_scale(xn, d) * (1.0 + sc_ref[...]) + sh_ref[...]
    for j in range(d // LANES):
        h2_ref[pl.ds(r0 * SUBLANES + j, nr, stride=SUBLANES), :] = h2[:, j * LANES:(j + 1) * LANES]
    h_hi = h2.astype(BF16)
    h_lo = (h2 - h_hi.astype(F32)).astype(BF16)
    nt_dims = (((1,), (1,)), ((), ()))
    logits = lax.dot_general(wrh_ref[...], h_hi, nt_dims, preferred_element_type=F32)
    logits += lax.dot_general(wrh_ref[...], h_lo, nt_dims, preferred_element_type=F32)
    logits += lax.dot_general(wrl_ref[...], h_hi, nt_dims, preferred_element_type=F32)
    e = jnp.exp(logits - jnp.max(logits, axis=0, keepdims=True))
    aff_ref[:, rg] = e / jnp.sum(e, axis=0, keepdims=True)


def _mix_out(x, ya, yb, yc, w_out, g1, sh2, sc2, wr_hi, wr_lo, tm):
    bsz, nt, d = x.shape
    vec = pl.BlockSpec((None, 1, d), lambda b, i: (b, 0, 0))
    full = lambda arr: pl.BlockSpec(arr.shape, lambda b, i: (0,) * arr.ndim)
    return pl.pallas_call(
        functools.partial(_mix_out_kernel, tm=tm),
        out_shape=(
            jax.ShapeDtypeStruct((bsz, nt, d), F32),
            jax.ShapeDtypeStruct((bsz, nt * SUBLANES, LANES), F32),
            jax.ShapeDtypeStruct((bsz, N_EXPERTS, nt), F32),
        ),
        grid=(bsz, nt // tm),
        in_specs=[
            pl.BlockSpec((None, tm, d), lambda b, i: (b, i, 0)),
            pl.BlockSpec((None, tm, SGU_W), lambda b, i: (b, i, 0)),
            pl.BlockSpec((None, tm, FNET_W), lambda b, i: (b, i, 0)),
            pl.BlockSpec((None, tm, MLA_HEADS * V_DIM), lambda b, i: (b, i, 0)),
            full(w_out), vec, vec, vec, full(wr_hi), full(wr_lo),
        ],
        out_specs=(
            pl.BlockSpec((None, tm, d), lambda b, i: (b, i, 0)),
            pl.BlockSpec((None, tm * SUBLANES, LANES), lambda b, i: (b, i, 0)),
            pl.BlockSpec((None, N_EXPERTS, tm), lambda b, i: (b, 0, i)),
        ),
        compiler_params=_cparams(("parallel", "parallel")),
        name="mix_out",
    )(x, ya, yb, yc, w_out, g1, sh2, sc2, wr_hi, wr_lo)


def _prefix_count(mask, tri):
    e, n = mask.shape
    w = tri.shape[0]
    carry = jnp.zeros((e, 1), F32)
    outs = []
    ends = []
    for c in range(n // w):
        local = jnp.dot(mask[:, c * w:(c + 1) * w].astype(BF16), tri, preferred_element_type=F32)
        outs.append(local + carry)
        carry = carry + local[:, w - 1:w]
        ends.append(carry)
    return jnp.concatenate(outs, axis=1), ends


ROUTE_CHUNK = 2 * LANES


def _route_select_kernel(aff_ref, key_ref, cend_ref, *, cap):
    n_exp, n = aff_ref.shape
    w = min(n, ROUTE_CHUNK)
    aff = aff_ref[...]

    def step(i, thr_bits):
        cand = thr_bits | jnp.left_shift(jnp.int32(1), 30 - i)
        cnt = jnp.sum((aff >= pltpu.bitcast(cand, F32)).astype(jnp.int32), axis=-1, keepdims=True)
        return jnp.where(cnt >= cap, cand, thr_bits)

    thr = pltpu.bitcast(lax.fori_loop(0, 31, step, jnp.zeros((n_exp, 1), jnp.int32)), F32)
    gt = aff > thr
    eq = aff == thr
    need = cap - jnp.sum(gt.astype(jnp.int32), axis=-1, keepdims=True)
    r = lax.broadcasted_iota(jnp.int32, (w, w), 0)
    c = lax.broadcasted_iota(jnp.int32, (w, w), 1)
    tri = (r <= c).astype(BF16)
    eq_rank, _ = _prefix_count(eq.astype(F32), tri)
    sel = jnp.logical_or(gt, jnp.logical_and(eq, eq_rank <= need.astype(F32)))
    cum, ends = _prefix_count(sel.astype(F32), tri)
    key_ref[...] = jnp.where(sel, cum, 0.0)
    lane = lax.broadcasted_iota(jnp.int32, (n_exp, LANES), 1)
    cend = jnp.zeros((n_exp, LANES), F32)
    for ci, end in enumerate(ends):
        cend = jnp.where(lane == ci, end, cend)
    cend_ref[...] = cend.astype(jnp.int32)


def _route_compact_kernel(cend_ref, key_ref, aff_ref, idx_ref, gate_ref, acc_i_ref, acc_g_ref, *, cap, jb):
    b = pl.program_id(0)
    e = pl.program_id(1)
    n_chunks, w = key_ref.shape
    n_sb = cap // jb
    base = (b * pl.num_programs(1) + e) * LANES
    shift = jb.bit_length() - 1
    acc_i_ref[...] = jnp.zeros_like(acc_i_ref)
    acc_g_ref[...] = jnp.zeros_like(acc_g_ref)

    def chunk(c, _):
        c_start = jnp.where(c > 0, cend_ref[base + jnp.maximum(c - 1, 0)], 0)
        c_end = cend_ref[base + c]

        @pl.when(c_end > c_start)
        def _any_selected():
            krow = key_ref[pl.ds(c, 1), :]
            arow = aff_ref[pl.ds(c, 1), :]
            tpos = (lax.broadcasted_iota(jnp.int32, (1, w), 1) + c * w).astype(F32)
            sb_lo = lax.shift_right_logical(c_start, shift)
            sb_hi = jnp.minimum(lax.shift_right_logical(c_end - 1, shift), n_sb - 1)

            def slot_block(sb, _):
                slot = (lax.broadcasted_iota(jnp.int32, (jb, 1), 0) + (sb * jb + 1)).astype(F32)
                hit = krow == slot
                pi = jnp.where(hit, tpos, 0.0)
                pg = jnp.where(hit, arow, 0.0)
                fi = pi[:, 0:LANES]
                fg = pg[:, 0:LANES]
                for k in range(1, w // LANES):
                    fi = fi + pi[:, k * LANES:(k + 1) * LANES]
                    fg = fg + pg[:, k * LANES:(k + 1) * LANES]
                acc_i_ref[sb] = acc_i_ref[sb] + fi
                acc_g_ref[sb] = acc_g_ref[sb] + fg
                return 0

            lax.fori_loop(sb_lo, sb_hi + 1, slot_block, 0)

        return 0

    lax.fori_loop(0, n_chunks, chunk, 0)
    for sb in range(n_sb):
        idx_ref[sb * jb:(sb + 1) * jb, :] = jnp.sum(acc_i_ref[sb], axis=-1, keepdims=True).astype(jnp.int32)
        gate_ref[sb * jb:(sb + 1) * jb, :] = jnp.sum(acc_g_ref[sb], axis=-1, keepdims=True)


def _route(aff_t, cap):
    bsz, n_exp, n = aff_t.shape
    jb = min(cap, LANES)
    w = min(n, ROUTE_CHUNK)
    key, cend = pl.pallas_call(
        functools.partial(_route_select_kernel, cap=cap),
        out_shape=(jax.ShapeDtypeStruct((bsz, n_exp, n), F32),
                   jax.ShapeDtypeStruct((bsz, n_exp, LANES), jnp.int32)),
        grid=(bsz,),
        in_specs=[pl.BlockSpec((None, n_exp, n), lambda b: (b, 0, 0))],
        out_specs=(pl.BlockSpec((None, n_exp, n), lambda b: (b, 0, 0)),
                   pl.BlockSpec((None, n_exp, LANES), lambda b: (b, 0, 0))),
        compiler_params=_cparams(("parallel",)),
        name="route_select",
    )(aff_t)
    chunked = pl.BlockSpec((None, None, n // w, w), lambda b, e, cend: (b, e, 0, 0))
    out_blk = pl.BlockSpec((None, None, cap, 1), lambda b, e, cend: (b, e, 0, 0))
    grid_spec = pltpu.PrefetchScalarGridSpec(
        num_scalar_prefetch=1,
        grid=(bsz, n_exp),
        in_specs=[chunked, chunked],
        out_specs=(out_blk, out_blk),
        scratch_shapes=[pltpu.VMEM((cap // jb, jb, LANES), F32)] * 2,
    )
    return pl.pallas_call(
        functools.partial(_route_compact_kernel, cap=cap, jb=jb),
        out_shape=(jax.ShapeDtypeStruct((bsz, n_exp, cap, 1), jnp.int32),
                   jax.ShapeDtypeStruct((bsz, n_exp, cap, 1), F32)),
        grid_spec=grid_spec,
        compiler_params=_cparams(("parallel", "parallel")),
        name="route_compact",
    )(cend.reshape(-1), key.reshape(bsz, n_exp, n // w, w), aff_t.reshape(bsz, n_exp, n // w, w))


GATHER_UNROLL = 8
BF16_ROWS = 16
EXPERT_ROW_CHUNKS = 4
GATHER_PARTS = 4 * EXPERT_ROW_CHUNKS


def _tile_copy(h_hbm, xs_ref, sem, b, tok, row):
    return pltpu.make_async_copy(
        h_hbm.at[b, pl.ds(pl.multiple_of(tok * SUBLANES, SUBLANES), SUBLANES), :],
        xs_ref.at[pl.ds(pl.multiple_of(row * SUBLANES, SUBLANES), SUBLANES), :],
        sem)


def _gather_start(idx_refs, h_refs, caps, xs_ref, sem, e, b, n_exp):
    row0 = 0
    for idx_ref, h_hbm, cap in zip(idx_refs, h_refs, caps):
        base = (b * n_exp + e) * cap
        unroll = math.gcd(cap, GATHER_UNROLL)

        def issue(g, _, idx_ref=idx_ref, h_hbm=h_hbm, base=base, unroll=unroll, row0=row0):
            for u in range(unroll):
                i = g * unroll + u
                _tile_copy(h_hbm, xs_ref, sem, b, idx_ref[base + i], row0 + i).start(priority=u % 2)
            return 0

        lax.fori_loop(0, cap // unroll, issue, 0)
        row0 += cap


def _gather_inline(idx_refs, h_refs, caps, xs_ref, sem, e, b, n_exp, part):
    row0 = 0
    for idx_ref, h_hbm, cap in zip(idx_refs, h_refs, caps):
        base = (b * n_exp + e) * cap
        for k, i in enumerate(range(part, cap, GATHER_PARTS)):
            _tile_copy(h_hbm, xs_ref, sem, b, idx_ref[base + i], row0 + i).start(priority=k % 2)
        row0 += cap


def _expert_kernel(*refs, caps, d):
    n_sets = len(caps)
    idx_refs = refs[:n_sets]
    h_refs = refs[n_sets:2 * n_sets]
    wg_ref, wu_ref, wd_ref = refs[2 * n_sets:2 * n_sets + 3]
    gate_refs = refs[2 * n_sets + 3:3 * n_sets + 3]
    ys_refs = refs[3 * n_sets + 3:4 * n_sets + 3]
    xs_a, xs_b, sem = refs[4 * n_sets + 3:]
    rows = sum(caps)
    e = pl.program_id(0)
    b = pl.program_id(1)
    n_exp = pl.num_programs(0)
    n_b = pl.num_programs(1)
    total = n_exp * n_b
    step = e * n_b + b
    slot = lax.rem(step, 2)

    def wait_all(buf, s):
        pltpu.make_async_copy(h_refs[0].at[b, pl.ds(0, rows * SUBLANES), :], buf, sem.at[s]).wait()

    @pl.when(step == 0)
    def _first():
        _gather_start(idx_refs, h_refs, caps, xs_a, sem.at[0], e, b, n_exp)

    nxt = jnp.where(step + 1 < total, step + 1, 0)
    e1 = lax.div(nxt, n_b)
    b1 = lax.rem(nxt, n_b)

    def body(cur, nxt_buf, s_cur, s_nxt):
        def start_part(part):
            _gather_inline(idx_refs, h_refs, caps, nxt_buf, sem.at[s_nxt], e1, b1, n_exp, part)

        wait_all(cur, s_cur)
        wg = wg_ref[...].astype(BF16)
        wu = wu_ref[...].astype(BF16)
        wd = wd_ref[...].astype(BF16)
        bounds = [0]
        for cap in caps:
            bounds.append(bounds[-1] + cap)
        n_chunks = min(EXPERT_ROW_CHUNKS, rows // BF16_ROWS)
        groups = rows // BF16_ROWS
        edges = [BF16_ROWS * (groups * c // n_chunks) for c in range(n_chunks)] + [rows]
        for c in range(n_chunks):
            r0, r1 = edges[c], edges[c + 1]
            chunk = r1 - r0
            x = jnp.concatenate(
                [cur[pl.ds(r0 * SUBLANES + j, chunk, stride=SUBLANES), :] for j in range(d // LANES)],
                axis=1).astype(BF16)
            start_part(4 * c)
            hid = _silu(jnp.dot(x, wg, preferred_element_type=F32))
            start_part(4 * c + 1)
            hid = hid * jnp.dot(x, wu, preferred_element_type=F32)
            start_part(4 * c + 2)
            y = jnp.dot(hid.astype(BF16), wd, preferred_element_type=F32)
            start_part(4 * c + 3)
            for s in range(n_sets):
                lo, hi = max(r0, bounds[s]), min(r1, bounds[s + 1])
                if lo < hi:
                    ys = y[lo - r0:hi - r0] * gate_refs[s][lo - bounds[s]:hi - bounds[s], :]
                    for j in range(d // LANES):
                        ys_refs[s][pl.ds((lo - bounds[s]) * SUBLANES + j, hi - lo, stride=SUBLANES), :] = (
                            ys[:, j * LANES:(j + 1) * LANES])
        for part in range(4 * n_chunks, GATHER_PARTS):
            start_part(part)

        @pl.when(step == total - 1)
        def _drain():
            wait_all(nxt_buf, s_nxt)

    @pl.when(slot == 0)
    def _even():
        body(xs_a, xs_b, 0, 1)

    @pl.when(slot == 1)
    def _odd():
        body(xs_b, xs_a, 1, 0)


def _experts(token_sets, wg, wu, wd, layer):
    caps = tuple(ts[3] for ts in token_sets)
    n_sets = len(token_sets)
    bsz = token_sets[0][1].shape[0]
    _, n_exp, d, f = wg.shape
    assert token_sets[0][1].shape[1] >= sum(caps) * SUBLANES
    wspec = lambda rows, cols: pl.BlockSpec((None, None, rows, cols), lambda e, b, *idx: (layer, e, 0, 0))
    grid_spec = pltpu.PrefetchScalarGridSpec(
        num_scalar_prefetch=n_sets,
        grid=(n_exp, bsz),
        in_specs=[pl.BlockSpec(memory_space=pl.ANY)] * n_sets + [wspec(d, f), wspec(d, f), wspec(f, d)] + [
            pl.BlockSpec((None, None, cap, 1), lambda e, b, *idx: (b, e, 0, 0)) for cap in caps],
        out_specs=tuple(pl.BlockSpec((None, None, cap * SUBLANES, LANES), lambda e, b, *idx: (b, e, 0, 0))
                        for cap in caps),
        scratch_shapes=[pltpu.VMEM((sum(caps) * SUBLANES, LANES), F32)] * 2 + [pltpu.SemaphoreType.DMA((2,))],
    )
    return pl.pallas_call(
        functools.partial(_expert_kernel, caps=caps, d=d),
        out_shape=tuple(jax.ShapeDtypeStruct((bsz, n_exp, cap * SUBLANES, LANES), F32) for cap in caps),
        grid_spec=grid_spec,
        compiler_params=_cparams(("arbitrary", "arbitrary")),
        name="experts",
    )(*[ts[0] for ts in token_sets], *[ts[1] for ts in token_sets], wg, wu, wd, *[ts[2] for ts in token_sets])


def _combine_kernel(idx_ref, ys_ref, o_hbm, acc_ref, sem, *, cap, unroll):
    b = pl.program_id(0)
    e = pl.program_id(1)
    n_exp = pl.num_programs(1)
    base = (b * n_exp + e) * cap

    @pl.when(e == 0)
    def _zero():
        acc_ref[...] = jnp.zeros_like(acc_ref)

    def rows(g, _):
        i0 = g * unroll
        toks = [pl.multiple_of(idx_ref[base + i0 + u] * SUBLANES, SUBLANES) for u in range(unroll)]
        sums = [acc_ref[pl.ds(toks[u], SUBLANES), :]
                + ys_ref[pl.ds(pl.multiple_of((i0 + u) * SUBLANES, SUBLANES), SUBLANES), :]
                for u in range(unroll)]
        for u in range(unroll):
            acc_ref[pl.ds(toks[u], SUBLANES), :] = sums[u]
        return 0

    lax.fori_loop(0, cap // unroll, rows, 0)

    @pl.when(e == n_exp - 1)
    def _flush():
        cp = pltpu.make_async_copy(acc_ref, o_hbm.at[b], sem)
        cp.start()
        cp.wait()


def _combine(idx_flat, ys_tiles, nt, cap):
    bsz, n_exp = ys_tiles.shape[:2]
    grid_spec = pltpu.PrefetchScalarGridSpec(
        num_scalar_prefetch=1,
        grid=(bsz, n_exp),
        in_specs=[pl.BlockSpec((None, None, cap * SUBLANES, LANES), lambda b, e, idx: (b, e, 0, 0))],
        out_specs=pl.BlockSpec(memory_space=pl.ANY),
        scratch_shapes=[pltpu.VMEM((nt * SUBLANES, LANES), F32), pltpu.SemaphoreType.DMA(())],
    )
    return pl.pallas_call(
        functools.partial(_combine_kernel, cap=cap, unroll=4),
        out_shape=jax.ShapeDtypeStruct((bsz, nt * SUBLANES, LANES), F32),
        grid_spec=grid_spec,
        compiler_params=_cparams(("arbitrary", "arbitrary")),
        name="combine",
    )(idx_flat, ys_tiles)


def _residual_kernel(x_ref, m_ref, g_ref, o_ref, *, tm):
    d = x_ref.shape[-1]
    moe = jnp.concatenate(
        [m_ref[pl.ds(j, tm, stride=SUBLANES), :] for j in range(d // LANES)], axis=1)
    o_ref[...] = x_ref[...] + g_ref[...] * moe


def _residual(x, moe_tiles, g2, tm):
    bsz, nt, d = x.shape
    return pl.pallas_call(
        functools.partial(_residual_kernel, tm=tm),
        out_shape=jax.ShapeDtypeStruct((bsz, nt, d), F32),
        grid=(bsz, nt // tm),
        in_specs=[
            pl.BlockSpec((None, tm, d), lambda b, i: (b, i, 0)),
            pl.BlockSpec((None, tm * SUBLANES, LANES), lambda b, i: (b, i, 0)),
            pl.BlockSpec((None, 1, d), lambda b, i: (b, 0, 0)),
        ],
        out_specs=pl.BlockSpec((None, tm, d), lambda b, i: (b, i, 0)),
        compiler_params=_cparams(("parallel", "parallel")),
        name="residual",
    )(x, moe_tiles, g2)


def _prepare_layer(l, w_in, sgu_norm, w_sgu, b_sgu, q_lora_norm, w_uq, kv_lora_norm, w_ukv, q_norm, k_norm,
                   w_out, w_router, w_gate, w_up, w_down):
    w_in_l = w_in[l]
    kr_cols = w_in_l[:, OFF_KR:OFF_KR + QK_ROPE]
    w_in_ext = jnp.concatenate([w_in_l[:, :OFF_KR], kr_cols, kr_cols], axis=1).astype(BF16)
    uq = w_uq[l].reshape(Q_LORA, MLA_HEADS, QK_DIM)
    w_uq_p = jnp.concatenate(
        [uq[:, :, :QK_NOPE].reshape(Q_LORA, MLA_HEADS * QK_NOPE),
         uq[:, :, QK_NOPE:].reshape(Q_LORA, MLA_HEADS * QK_ROPE)], axis=1).astype(BF16)

    def norm_pad(v):
        return jnp.concatenate([v[:QK_NOPE], v[QK_NOPE:], v[QK_NOPE:]])[None, :]

    wr_t = w_router[l].T
    wr_hi = wr_t.astype(BF16)
    wr_lo = (wr_t - wr_hi.astype(F32)).astype(BF16)
    return dict(
        w_in=w_in_ext,
        sgu_norm=sgu_norm[l][None, :],
        w_sgu=jnp.concatenate([w_sgu[l][hh] for hh in range(SGU_HEADS)], axis=1).astype(BF16),
        b_sgu=jnp.repeat(b_sgu[l].T, SGU_HEAD_DIM, axis=1),
        fdft=jnp.asarray(_channel_dft_table()).astype(BF16),
        q_lora_norm=q_lora_norm[l][None, :],
        w_uq=w_uq_p,
        q_norm=norm_pad(q_norm[l]),
        kv_lora_norm=kv_lora_norm[l][None, :],
        w_ukv=w_ukv[l].astype(BF16),
        k_norm=norm_pad(k_norm[l]),
        w_out=w_out[l].astype(BF16),
        wr_hi=wr_hi, wr_lo=wr_lo,
        w_gate=w_gate, w_up=w_up, w_down=w_down, layer=l,
    )


def _rope_tables(n):
    pos = np.arange(n)
    half = QK_ROPE // 4
    freqs = ROPE_THETA ** (-np.arange(half, dtype=np.float64) / half)
    ang_r = (pos // GRID_WIDTH)[:, None] * freqs
    ang_c = (pos % GRID_WIDTH)[:, None] * freqs
    cos64 = np.concatenate([np.cos(ang_r), np.cos(ang_r), np.cos(ang_c), np.cos(ang_c)], axis=1)
    sin64 = np.concatenate([-np.sin(ang_r), np.sin(ang_r), -np.sin(ang_c), np.sin(ang_c)], axis=1)
    return (jnp.asarray(np.tile(cos64, (1, 2)), F32), jnp.asarray(np.tile(sin64, (1, 2)), F32))


def _token_tile(nt, pref):
    return pref if nt % pref == 0 else nt


def _moe(populations, lw):
    token_sets = []
    for h2_tiles, aff_t in populations:
        nt = aff_t.shape[-1]
        cap = EC_FACTOR * nt // N_EXPERTS
        idx, gate = _route(aff_t, cap)
        token_sets.append((idx.reshape(-1), h2_tiles, gate, cap))
    ys = _experts(token_sets, lw["w_gate"], lw["w_up"], lw["w_down"], lw["layer"])
    return [_combine(ts[0], y, pop[1].shape[-1], ts[3]) for ts, y, pop in zip(token_sets, ys, populations)]


def kernel(x, c, ctx, c_ctx, w_ada, b_ada, w_in, sgu_norm, w_sgu, b_sgu, q_lora_norm, w_uq, kv_lora_norm, w_ukv,
           q_norm, k_norm, w_out, w_router, w_gate, w_up, w_down):
    bsz, n, d = x.shape
    n_ctx = ctx.shape[1]
    depth = w_ada.shape[0]
    assert bsz + 1 <= SUBLANES

    cond = jnp.zeros((SUBLANES, d), F32).at[:bsz].set(c).at[bsz].set(c_ctx)
    mod = _ada_modulation(cond, w_ada, b_ada)
    rope_tabs = _rope_tables(n)
    tm = _token_tile(n, 1024)
    tm_in = _token_tile(n, 1024)
    tm_c = _token_tile(n_ctx, 256)

    xc = ctx
    pending = pending_c = None
    for l in range(depth):
        last = l == depth - 1
        lw = _prepare_layer(l, w_in, sgu_norm, w_sgu, b_sgu, q_lora_norm, w_uq, kv_lora_norm, w_ukv,
                            q_norm, k_norm, w_out, w_router, w_gate, w_up, w_down)
        parts = [mod[l, :, i * d:(i + 1) * d] for i in range(6)]
        lat = [p[:bsz, None, :] for p in parts]
        cx = [jnp.broadcast_to(p[bsz][None, None, :], (bsz, 1, d)) for p in parts]

        outs_c = _mix_in(xc, cx[0], cx[1], lw, None, tm_c, pending_c)
        outs = _mix_in(x, lat[0], lat[1], lw, rope_tabs, tm_in, pending)
        ya_c, a_c, b_c, q_c, k_c, v_c = outs_c[:6]
        ya, a, b, q, k, v = outs[:6]
        if pending is not None:
            xc, x = outs_c[6], outs[6]
        yb = _fourier_positions(a, b)
        tk = _token_tile(n, ATTN_TK)
        vt_c = jnp.swapaxes(v_c, 2, 3)
        vt = jnp.swapaxes(v.reshape(bsz, MLA_HEADS, n // tk, tk, V_DIM), 3, 4)
        yc = _attention(jnp.swapaxes(q, 2, 3), k_c, vt_c, k, vt, tq=_token_tile(n, ATTN_TQ), tk=tk)
        x_new, h2_tiles, aff_t = _mix_out(x, ya, yb, yc, lw["w_out"], lat[2], lat[3], lat[4],
                                          lw["wr_hi"], lw["wr_lo"], tm)
        if not last:
            yb_c = _fourier_positions(a_c, b_c)
            yc_c = _attention(jnp.swapaxes(q_c, 2, 3), k_c, vt_c, None, None,
                              tq=_token_tile(n_ctx, ATTN_TQ), tk=n_ctx)
            xc_new, h2c_tiles, affc_t = _mix_out(xc, ya_c, yb_c, yc_c, lw["w_out"], cx[2], cx[3], cx[4],
                                                 lw["wr_hi"], lw["wr_lo"], tm_c)
            moe, moe_c = _moe([(h2_tiles, aff_t), (h2c_tiles, affc_t)], lw)
            xc, pending_c = xc_new, (moe_c, cx[5])
        else:
            moe, = _moe([(h2_tiles, aff_t)], lw)
        x, pending = x_new, (moe, lat[5])
    return _residual(x, pending[0], pending[1], tm)
```

```python
import functools
import math

import numpy as np
import jax
import jax.numpy as jnp
from jax import lax
from jax.experimental import pallas as pl
from jax.experimental.pallas import tpu as pltpu

F32 = jnp.float32
BF16 = jnp.bfloat16

GRID_WIDTH = 64
SGU_HEADS = 4
SGU_HEAD_DIM = 64
SGU_W = SGU_HEADS * SGU_HEAD_DIM
CHUNK = 128
FNET_GROUP_DIM = 64
FNET_W = 256
MLA_HEADS = 4
QK_NOPE = 128
QK_ROPE = 64
QK_DIM = QK_NOPE + QK_ROPE
V_DIM = 128
Q_LORA = 256
KV_LORA = 128
OFF_V = 256
OFF_F = 512
OFF_Q = 768
OFF_KV = 1024
OFF_KR = 1152
N_EXPERTS = 16
EC_FACTOR = 2
ROPE_THETA = 10000.0
EPS = 1e-6

LANES = 128
SUBLANES = 8
VMEM_LIMIT_BYTES = 56 * 1024 * 1024

QK_PAD = 2 * LANES
ATTN_TQ = 4096
ATTN_TK = 512
LOG2_E = 1.4426950408889634
MIX_ROWS = 128
MIX_OUT_ROWS = 512


def _cparams(sem, vmem=None):
    return pltpu.CompilerParams(dimension_semantics=sem, vmem_limit_bytes=vmem or VMEM_LIMIT_BYTES)


def _rms_scale(v, width):
    return lax.rsqrt(jnp.sum(v * v, axis=-1, keepdims=True) * (1.0 / width) + EPS)


def _gelu_tanh(v):
    return 0.5 * v * (1.0 + jnp.tanh(0.7978845608028654 * (v + 0.044715 * v * v * v)))


def _silu(v):
    return v / (1.0 + jnp.exp(-v))


def _ada_kernel(cond_ref, w_ref, b_ref, o_ref):
    s = _silu(cond_ref[...])
    o_ref[...] = jnp.dot(s, w_ref[...], preferred_element_type=F32,
                         precision=lax.Precision.HIGHEST) + b_ref[...]


def _ada_modulation(cond, w_ada, b_ada):
    depth, d, six_d = w_ada.shape
    tn = 1536
    return pl.pallas_call(
        _ada_kernel,
        out_shape=jax.ShapeDtypeStruct((depth, SUBLANES, six_d), F32),
        grid=(depth, six_d // tn),
        in_specs=[
            pl.BlockSpec((SUBLANES, d), lambda l, j: (0, 0)),
            pl.BlockSpec((None, d, tn), lambda l, j: (l, 0, j)),
            pl.BlockSpec((None, 1, tn), lambda l, j: (l, 0, j)),
        ],
        out_specs=pl.BlockSpec((None, SUBLANES, tn), lambda l, j: (l, 0, j)),
        compiler_params=_cparams(("arbitrary", "arbitrary")),
        name="ada_modulation",
    )(cond, w_ada, b_ada.reshape(depth, 1, six_d))


def _swap16(v):
    lane = lax.broadcasted_iota(jnp.int32, v.shape, 1)
    up = pltpu.roll(v, LANES - 16, axis=1)
    dn = pltpu.roll(v, 16, axis=1)
    return jnp.where((lane % 32) < 16, up, dn)


def _mix_in_kernel(*refs, tm, use_rope, fuse_residual):
    rows_per_group = min(tm, MIX_ROWS)
    for g in range(tm // rows_per_group):
        _mix_in_rows(slice(g * rows_per_group, (g + 1) * rows_per_group), refs, use_rope, fuse_residual)


def _mix_in_rows(rg, refs, use_rope, fuse_residual):
    if fuse_residual:
        x_ref, moe_ref, g2_ref = refs[:3]
        xo_ref = refs[-1]
        refs = refs[:1] + refs[3:-1]
    (x_ref, sh_ref, sc_ref, win_ref, sgun_ref, wsgu_ref, bsgu_ref, fdft_ref, qln_ref, wuq_ref, qn_ref,
     kvn_ref, wukv_ref, kn_ref, cos_ref, sin_ref, ya_ref, a_ref, b_ref, q_ref, k_ref, v_ref) = refs
    x = x_ref[rg, :]
    if fuse_residual:
        nrows = rg.stop - rg.start
        moe = jnp.concatenate(
            [moe_ref[pl.ds(rg.start * SUBLANES + j, nrows, stride=SUBLANES), :]
             for j in range(x.shape[-1] // LANES)], axis=1)
        x = x + g2_ref[...] * moe
        xo_ref[rg, :] = x
    tm = x.shape[0]
    d = x.shape[-1]
    h = x * _rms_scale(x, d) * (1.0 + sc_ref[...]) + sh_ref[...]
    p = jnp.dot(h.astype(BF16), win_ref[...], preferred_element_type=F32)

    u = _gelu_tanh(p[:, 0:OFF_V])
    gv = _gelu_tanh(p[:, OFF_V:OFF_F])
    vn = gv * _rms_scale(gv, SGU_W) * sgun_ref[...]
    head_of_lane = lax.broadcasted_iota(jnp.int32, (CHUNK, SGU_W), 1) // SGU_HEAD_DIM
    for c in range(tm // CHUNK):
        rows = slice(c * CHUNK, (c + 1) * CHUNK)
        vc = vn[rows]
        vstack = jnp.concatenate(
            [jnp.where(head_of_lane == hh, vc, 0.0) for hh in range(SGU_HEADS)], axis=0).astype(BF16)
        z = jnp.dot(wsgu_ref[...], vstack, preferred_element_type=F32) + bsgu_ref[...]
        ya_ref[rg.start + c * CHUNK:rg.start + (c + 1) * CHUNK, :] = (u[rows] * z).astype(ya_ref.dtype)

    ab = jnp.dot(p[:, OFF_F:OFF_Q].astype(BF16), fdft_ref[...], preferred_element_type=F32)
    a_ref[rg, :] = ab[:, 0:FNET_W].astype(a_ref.dtype)
    b_ref[rg, :] = ab[:, FNET_W:2 * FNET_W].astype(b_ref.dtype)

    lane = lax.broadcasted_iota(jnp.int32, (1, LANES), 1)
    lo = lane < QK_ROPE
    if use_rope:
        cos = cos_ref[rg, :]
        sin = sin_ref[rg, :]

    def rope(t):
        return t * cos + _swap16(t) * sin if use_rope else t

    pq = p[:, OFF_Q:OFF_KV]
    cq = pq * _rms_scale(pq, Q_LORA) * qln_ref[...]
    q = jnp.dot(cq.astype(BF16), wuq_ref[...], preferred_element_type=F32)
    qnorm = qn_ref[...]
    for hh in range(MLA_HEADS):
        qn = q[:, LANES * hh:LANES * (hh + 1)]
        pair = hh // 2
        qr = q[:, MLA_HEADS * LANES + LANES * pair:MLA_HEADS * LANES + LANES * (pair + 1)]
        qr = jnp.where(lo if hh % 2 == 0 else jnp.logical_not(lo), qr, 0.0)
        ssq = jnp.sum(qn * qn, axis=-1, keepdims=True) + jnp.sum(qr * qr, axis=-1, keepdims=True)
        rinv = lax.rsqrt(ssq * (1.0 / QK_DIM) + EPS) * (QK_DIM ** -0.5 * LOG2_E)
        q_ref[hh, rg, 0:LANES] = (qn * rinv * qnorm[:, 0:LANES]).astype(q_ref.dtype)
        q_ref[hh, rg, LANES:QK_PAD] = rope(qr * rinv * qnorm[:, LANES:QK_PAD]).astype(q_ref.dtype)

    pkv = p[:, OFF_KV:OFF_KR]
    ckv = pkv * _rms_scale(pkv, KV_LORA) * kvn_ref[...]
    kv = jnp.dot(ckv.astype(BF16), wukv_ref[...], preferred_element_type=F32)
    krr = p[:, OFF_KR:OFF_KR + LANES]
    ssq_kr = 0.5 * jnp.sum(krr * krr, axis=-1, keepdims=True)
    knorm = kn_ref[...]
    rinvs = []
    for hh in range(MLA_HEADS):
        kn = kv[:, 2 * LANES * hh:2 * LANES * hh + LANES]
        ssq = jnp.sum(kn * kn, axis=-1, keepdims=True) + ssq_kr
        rinv = lax.rsqrt(ssq * (1.0 / QK_DIM) + EPS)
        rinvs.append(rinv)
        k_ref[hh, rg, 0:LANES] = (kn * rinv * knorm[:, 0:LANES]).astype(k_ref.dtype)
        v_ref[hh, rg, :] = kv[:, 2 * LANES * hh + LANES:2 * LANES * (hh + 1)].astype(v_ref.dtype)
    for pair in range(MLA_HEADS // 2):
        r2 = jnp.where(lo, rinvs[2 * pair], rinvs[2 * pair + 1])
        kr = rope(krr * r2 * knorm[:, LANES:QK_PAD]).astype(k_ref.dtype)
        k_ref[2 * pair, rg, LANES:QK_PAD] = kr
        k_ref[2 * pair + 1, rg, LANES:QK_PAD] = kr


def _mix_in(x, sh, sc, lw, rope_tabs, tm, pending=None):
    bsz, nt, d = x.shape
    use_rope = rope_tabs is not None
    if use_rope:
        cos_t, sin_t = rope_tabs
    else:
        cos_t = sin_t = jnp.zeros((nt, LANES), F32)
    full = lambda arr: pl.BlockSpec(arr.shape, lambda b, i: (0,) * arr.ndim)
    vec = pl.BlockSpec((None, 1, d), lambda b, i: (b, 0, 0))
    in_w = lw["w_in"].shape[1]
    fuse = pending is not None
    kern = functools.partial(_mix_in_kernel, tm=tm, use_rope=use_rope, fuse_residual=fuse)
    row_blk = pl.BlockSpec((None, tm, d), lambda b, i: (b, i, 0))
    out_shape = [
        jax.ShapeDtypeStruct((bsz, nt, SGU_W), BF16),
        jax.ShapeDtypeStruct((bsz, nt, FNET_W), F32),
        jax.ShapeDtypeStruct((bsz, nt, FNET_W), F32),
        jax.ShapeDtypeStruct((bsz, MLA_HEADS, nt, QK_PAD), BF16),
        jax.ShapeDtypeStruct((bsz, MLA_HEADS, nt, QK_PAD), BF16),
        jax.ShapeDtypeStruct((bsz, MLA_HEADS, nt, V_DIM), BF16),
    ]
    out_specs = [
        pl.BlockSpec((None, tm, SGU_W), lambda b, i: (b, i, 0)),
        pl.BlockSpec((None, tm, FNET_W), lambda b, i: (b, i, 0)),
        pl.BlockSpec((None, tm, FNET_W), lambda b, i: (b, i, 0)),
        pl.BlockSpec((None, MLA_HEADS, tm, QK_PAD), lambda b, i: (b, 0, i, 0)),
        pl.BlockSpec((None, MLA_HEADS, tm, QK_PAD), lambda b, i: (b, 0, i, 0)),
        pl.BlockSpec((None, MLA_HEADS, tm, V_DIM), lambda b, i: (b, 0, i, 0)),
    ]
    in_specs = [row_blk]
    args = [x]
    if fuse:
        in_specs += [pl.BlockSpec((None, tm * SUBLANES, LANES), lambda b, i: (b, i, 0)), vec]
        args += list(pending)
        out_shape.append(jax.ShapeDtypeStruct((bsz, nt, d), F32))
        out_specs.append(row_blk)
    in_specs += [
        vec, vec,
        full(lw["w_in"]), full(lw["sgu_norm"]), full(lw["w_sgu"]), full(lw["b_sgu"]), full(lw["fdft"]),
        full(lw["q_lora_norm"]), full(lw["w_uq"]), full(lw["q_norm"]),
        full(lw["kv_lora_norm"]), full(lw["w_ukv"]), full(lw["k_norm"]),
        pl.BlockSpec((tm, LANES), lambda b, i: (i, 0)),
        pl.BlockSpec((tm, LANES), lambda b, i: (i, 0)),
    ]
    args += [sh, sc, lw["w_in"], lw["sgu_norm"], lw["w_sgu"], lw["b_sgu"], lw["fdft"],
             lw["q_lora_norm"], lw["w_uq"], lw["q_norm"], lw["kv_lora_norm"], lw["w_ukv"], lw["k_norm"],
             cos_t, sin_t]
    return pl.pallas_call(
        kern,
        out_shape=tuple(out_shape),
        grid=(bsz, nt // tm),
        in_specs=in_specs,
        out_specs=tuple(out_specs),
        compiler_params=_cparams(("parallel", "parallel")),
        name="mix_in",
    )(*args)


DFT_UNROLL = 16


def _dft_kernel(a_ref, b_ref, w_ref, m_ref, o_ref, yr_ref, yi_ref, *, n1, n2):
    w1 = w_ref[...]
    c = a_ref.shape[-1]

    u1 = math.gcd(n2, DFT_UNROLL)
    u2 = math.gcd(n1, DFT_UNROLL)

    def step1(g, _):
        cols = []
        for u in range(u1):
            rows = pl.ds(g * u1 + u, n1, stride=n2)
            cols.append(jnp.concatenate([a_ref[rows, :], b_ref[rows, :]], axis=0))
        r = jnp.dot(w1, jnp.concatenate(cols, axis=1).astype(BF16), preferred_element_type=F32)
        for u in range(u1):
            out_rows = pl.ds(pl.multiple_of((g * u1 + u) * n1, n1), n1)
            yr_ref[out_rows, :] = r[0:n1, u * c:(u + 1) * c]
            yi_ref[out_rows, :] = r[n1:2 * n1, u * c:(u + 1) * c]
        return 0

    lax.fori_loop(0, n2 // u1, step1, 0)

    def step2(g, _):
        for u in range(u2):
            k1 = g * u2 + u
            rows = pl.ds(k1, n2, stride=n1)
            ycat = jnp.concatenate([yr_ref[rows, :], yi_ref[rows, :]], axis=0).astype(BF16)
            o_ref[rows, :] = jnp.dot(m_ref[k1], ycat, preferred_element_type=F32)
        return 0

    lax.fori_loop(0, n1 // u2, step2, 0)


def _dft_factors(n):
    n1 = 64 if n >= 1024 else 32
    return n1, n // n1


@functools.lru_cache(maxsize=None)
def _dft_tables(n):
    n1, n2 = _dft_factors(n)
    j = np.arange(n1, dtype=np.float64)
    ang1 = 2.0 * np.pi * np.outer(j, j) / n1
    c1 = np.cos(ang1) / math.sqrt(n1)
    s1 = np.sin(ang1) / math.sqrt(n1)
    w1 = np.block([[c1, -s1], [s1, c1]])
    k1 = np.arange(n1, dtype=np.float64)[:, None, None]
    k2 = np.arange(n2, dtype=np.float64)[None, :, None]
    m2 = np.arange(n2, dtype=np.float64)[None, None, :]
    ang2 = 2.0 * np.pi * m2 * (k1 + n1 * k2) / n
    m2cat = np.concatenate([np.cos(ang2), -np.sin(ang2)], axis=-1) / math.sqrt(n2)
    return w1.astype(np.float32), m2cat.astype(np.float32)


@functools.lru_cache(maxsize=None)
def _channel_dft_table():
    j = np.arange(FNET_GROUP_DIM, dtype=np.float64)
    ang = 2.0 * np.pi * np.outer(j, j) / FNET_GROUP_DIM
    groups = FNET_W // FNET_GROUP_DIM
    cg = np.kron(np.eye(groups), np.cos(ang)) / math.sqrt(FNET_GROUP_DIM)
    sg = np.kron(np.eye(groups), np.sin(ang)) / math.sqrt(FNET_GROUP_DIM)
    return np.concatenate([cg, sg], axis=1).astype(np.float32)


def _fourier_positions(a, b):
    bsz, n, c = a.shape
    n1, n2 = _dft_factors(n)
    w1_np, m2_np = _dft_tables(n)
    w1 = jnp.asarray(w1_np).astype(BF16)
    m2cat = jnp.asarray(m2_np).astype(BF16)
    cb = LANES
    blk = pl.BlockSpec((None, n, cb), lambda bi, j: (bi, 0, j))
    return pl.pallas_call(
        functools.partial(_dft_kernel, n1=n1, n2=n2),
        out_shape=jax.ShapeDtypeStruct((bsz, n, c), F32),
        grid=(bsz, c // cb),
        in_specs=[blk, blk,
                  pl.BlockSpec((2 * n1, 2 * n1), lambda bi, j: (0, 0)),
                  pl.BlockSpec((n1, n2, 2 * n2), lambda bi, j: (0, 0, 0))],
        out_specs=blk,
        scratch_shapes=[pltpu.VMEM((n, cb), F32)] * 2,
        compiler_params=_cparams(("parallel", "parallel")),
        name="fourier",
    )(a, b, w1, m2cat)


NEG_BIG = -1e30


def _attn_kernel(*refs, tk, n_lat_blocks):
    if n_lat_blocks:
        qt_ref, kc_ref, vtc_ref, kl_ref, vtl_ref, o_ref, acc_ref, sa_ref, sb_ref = refs
    else:
        qt_ref, kc_ref, vtc_ref, o_ref, acc_ref = refs
    qt = qt_ref[...]
    tq = qt.shape[1]

    def scores(kj):
        return jnp.dot(kj, qt, preferred_element_type=F32)

    def consume(s, vtj, m, l, first):
        m_new = jnp.maximum(m, jnp.max(s, axis=0, keepdims=True))
        alpha = jnp.exp2(m - m_new)
        p = jnp.exp2(s - m_new)
        l_new = alpha * l + jnp.sum(p, axis=0, keepdims=True)
        pv = jnp.dot(vtj, p.astype(BF16), preferred_element_type=F32)
        acc_ref[...] = pv if first else alpha * acc_ref[...] + pv
        return m_new, l_new

    def lat_keys(j):
        return kl_ref[pl.ds(pl.multiple_of(j * tk, tk), tk), :]

    m0 = jnp.full((1, tq), NEG_BIG, F32)
    l0 = jnp.zeros((1, tq), F32)
    if n_lat_blocks:
        sa_ref[...] = scores(lat_keys(0))
    m, l = consume(scores(kc_ref[...]), vtc_ref[...], m0, l0, True)
    if n_lat_blocks:
        def body(i, carry):
            sb_ref[...] = scores(lat_keys(2 * i + 1))
            m, l = consume(sa_ref[...], vtl_ref[2 * i], carry[0], carry[1], False)
            sa_ref[...] = scores(lat_keys(2 * i + 2))
            return consume(sb_ref[...], vtl_ref[2 * i + 1], m, l, False)

        m, l = lax.fori_loop(0, n_lat_blocks // 2 - 1, body, (m, l))
        sb_ref[...] = scores(lat_keys(n_lat_blocks - 1))
        m, l = consume(sa_ref[...], vtl_ref[n_lat_blocks - 2], m, l, False)
        m, l = consume(sb_ref[...], vtl_ref[n_lat_blocks - 1], m, l, False)
    o_ref[...] = jnp.transpose(acc_ref[...] / l).astype(o_ref.dtype)


def _attention(qt, k_ctx, vt_ctx, k_lat, vt_lat, tq, tk):
    bsz, heads, _, nq = qt.shape
    m_ctx = k_ctx.shape[2]
    n_lat_blocks = 0 if k_lat is None else k_lat.shape[2] // tk
    in_specs = [
        pl.BlockSpec((None, None, QK_PAD, tq), lambda b, h, i: (b, h, 0, i)),
        pl.BlockSpec((None, None, m_ctx, QK_PAD), lambda b, h, i: (b, h, 0, 0)),
        pl.BlockSpec((None, None, V_DIM, m_ctx), lambda b, h, i: (b, h, 0, 0)),
    ]
    args = [qt, k_ctx, vt_ctx]
    if n_lat_blocks:
        in_specs += [
            pl.BlockSpec((None, None, k_lat.shape[2], QK_PAD), lambda b, h, i: (b, h, 0, 0)),
            pl.BlockSpec((None, None, n_lat_blocks, V_DIM, tk), lambda b, h, i: (b, h, 0, 0, 0)),
        ]
        args += [k_lat, vt_lat]
    return pl.pallas_call(
        functools.partial(_attn_kernel, tk=tk, n_lat_blocks=n_lat_blocks),
        out_shape=jax.ShapeDtypeStruct((bsz, nq, heads * V_DIM), BF16),
        grid=(bsz, heads, nq // tq),
        in_specs=in_specs,
        out_specs=pl.BlockSpec((None, tq, V_DIM), lambda b, h, i: (b, i, h)),
        scratch_shapes=[pltpu.VMEM((V_DIM, tq), F32)] + [pltpu.VMEM((tk, tq), F32)] * (2 if n_lat_blocks else 0),
        compiler_params=_cparams(("parallel", "parallel", "arbitrary")),
        name="attention",
    )(*args)


def _mix_out_kernel(x_ref, ya_ref, yb_ref, yc_ref, wo_ref, g1_ref, sh_ref, sc_ref, wrh_ref, wrl_ref,
                    xn_ref, h2_ref, aff_ref, *, tm):
    rows_per_group = min(tm, MIX_OUT_ROWS)
    for g in range(tm // rows_per_group):
        _mix_out_rows(g * rows_per_group, rows_per_group, x_ref, ya_ref, yb_ref, yc_ref, wo_ref, g1_ref, sh_ref,
                      sc_ref, wrh_ref, wrl_ref, xn_ref, h2_ref, aff_ref)


def _mix_out_rows(r0, nr, x_ref, ya_ref, yb_ref, yc_ref, wo_ref, g1_ref, sh_ref, sc_ref, wrh_ref, wrl_ref,
                  xn_ref, h2_ref, aff_ref):
    rg = slice(r0, r0 + nr)
    mix = jnp.dot(ya_ref[rg, :], wo_ref[0:SGU_W, :], preferred_element_type=F32)
    mix += jnp.dot(yb_ref[rg, :].astype(BF16), wo_ref[SGU_W:SGU_W + FNET_W, :], preferred_element_type=F32)
    mix += jnp.dot(yc_ref[rg, :], wo_ref[SGU_W + FNET_W:, :], preferred_element_type=F32)
    xn = x_ref[rg, :] + g1_ref[...] * mix
    xn_ref[rg, :] = xn
    d = xn.shape[-1]
    h2 = xn * _rms_scale(xn, d) * (1.0 + sc_ref[...]) + sh_ref[...]
    for j in range(d // LANES):
        h2_ref[pl.ds(r0 * SUBLANES + j, nr, stride=SUBLANES), :] = h2[:, j * LANES:(j + 1) * LANES]
    h_hi = h2.astype(BF16)
    h_lo = (h2 - h_hi.astype(F32)).astype(BF16)
    nt_dims = (((1,), (1,)), ((), ()))
    logits = lax.dot_general(wrh_ref[...], h_hi, nt_dims, preferred_element_type=F32)
    logits += lax.dot_general(wrh_ref[...], h_lo, nt_dims, preferred_element_type=F32)
    logits += lax.dot_general(wrl_ref[...], h_hi, nt_dims, preferred_element_type=F32)
    e = jnp.exp(logits - jnp.max(logits, axis=0, keepdims=True))
    aff_ref[:, rg] = e / jnp.sum(e, axis=0, keepdims=True)


def _mix_out(x, ya, yb, yc, w_out, g1, sh2, sc2, wr_hi, wr_lo, tm):
    bsz, nt, d = x.shape
    vec = pl.BlockSpec((None, 1, d), lambda b, i: (b, 0, 0))
    full = lambda arr: pl.BlockSpec(arr.shape, lambda b, i: (0,) * arr.ndim)
    return pl.pallas_call(
        functools.partial(_mix_out_kernel, tm=tm),
        out_shape=(
            jax.ShapeDtypeStruct((bsz, nt, d), F32),
            jax.ShapeDtypeStruct((bsz, nt * SUBLANES, LANES), F32),
            jax.ShapeDtypeStruct((bsz, N_EXPERTS, nt), F32),
        ),
        grid=(bsz, nt // tm),
        in_specs=[
            pl.BlockSpec((None, tm, d), lambda b, i: (b, i, 0)),
            pl.BlockSpec((None, tm, SGU_W), lambda b, i: (b, i, 0)),
            pl.BlockSpec((None, tm, FNET_W), lambda b, i: (b, i, 0)),
            pl.BlockSpec((None, tm, MLA_HEADS * V_DIM), lambda b, i: (b, i, 0)),
            full(w_out), vec, vec, vec, full(wr_hi), full(wr_lo),
        ],
        out_specs=(
            pl.BlockSpec((None, tm, d), lambda b, i: (b, i, 0)),
            pl.BlockSpec((None, tm * SUBLANES, LANES), lambda b, i: (b, i, 0)),
            pl.BlockSpec((None, N_EXPERTS, tm), lambda b, i: (b, 0, i)),
        ),
        compiler_params=_cparams(("parallel", "parallel")),
        name="mix_out",
    )(x, ya, yb, yc, w_out, g1, sh2, sc2, wr_hi, wr_lo)


def _prefix_count(mask, tri):
    e, n = mask.shape
    w = tri.shape[0]
    carry = jnp.zeros((e, 1), F32)
    outs = []
    ends = []
    for c in range(n // w):
        local = jnp.dot(mask[:, c * w:(c + 1) * w].astype(BF16), tri, preferred_element_type=F32)
        outs.append(local + carry)
        carry = carry + local[:, w - 1:w]
        ends.append(carry)
    return jnp.concatenate(outs, axis=1), ends


ROUTE_CHUNK = 4 * LANES


def _route_select_kernel(aff_ref, key_ref, cend_ref, *, cap):
    n_exp, n = aff_ref.shape
    w = min(n, ROUTE_CHUNK)
    aff = aff_ref[...]

    def step(i, thr_bits):
        cand = thr_bits | jnp.left_shift(jnp.int32(1), 30 - i)
        cnt = jnp.sum((aff >= pltpu.bitcast(cand, F32)).astype(jnp.int32), axis=-1, keepdims=True)
        return jnp.where(cnt >= cap, cand, thr_bits)

    thr = pltpu.bitcast(lax.fori_loop(0, 31, step, jnp.zeros((n_exp, 1), jnp.int32)), F32)
    gt = aff > thr
    eq = aff == thr
    need = cap - jnp.sum(gt.astype(jnp.int32), axis=-1, keepdims=True)
    r = lax.broadcasted_iota(jnp.int32, (w, w), 0)
    c = lax.broadcasted_iota(jnp.int32, (w, w), 1)
    tri = (r <= c).astype(BF16)
    eq_rank, _ = _prefix_count(eq.astype(F32), tri)
    sel = jnp.logical_or(gt, jnp.logical_and(eq, eq_rank <= need.astype(F32)))
    cum, ends = _prefix_count(sel.astype(F32), tri)
    key_ref[...] = jnp.where(sel, cum, 0.0)
    lane = lax.broadcasted_iota(jnp.int32, (n_exp, LANES), 1)
    cend = jnp.zeros((n_exp, LANES), F32)
    for ci, end in enumerate(ends):
        cend = jnp.where(lane == ci, end, cend)
    cend_ref[...] = cend.astype(jnp.int32)


def _route_compact_kernel(cend_ref, key_ref, aff_ref, idx_ref, gate_ref, acc_i_ref, acc_g_ref, *, cap, jb):
    b = pl.program_id(0)
    e = pl.program_id(1)
    n_chunks, w = key_ref.shape
    n_sb = cap // jb
    base = (b * pl.num_programs(1) + e) * LANES
    shift = jb.bit_length() - 1
    acc_i_ref[...] = jnp.zeros_like(acc_i_ref)
    acc_g_ref[...] = jnp.zeros_like(acc_g_ref)

    def chunk(c, _):
        c_start = jnp.where(c > 0, cend_ref[base + jnp.maximum(c - 1, 0)], 0)
        c_end = cend_ref[base + c]

        @pl.when(c_end > c_start)
        def _any_selected():
            krow = key_ref[pl.ds(c, 1), :]
            arow = aff_ref[pl.ds(c, 1), :]
            tpos = (lax.broadcasted_iota(jnp.int32, (1, w), 1) + c * w).astype(F32)
            sb_lo = lax.shift_right_logical(c_start, shift)
            sb_hi = jnp.minimum(lax.shift_right_logical(c_end - 1, shift), n_sb - 1)

            def slot_block(sb, _):
                slot = (lax.broadcasted_iota(jnp.int32, (jb, 1), 0) + (sb * jb + 1)).astype(F32)
                hit = krow == slot
                pi = jnp.where(hit, tpos, 0.0)
                pg = jnp.where(hit, arow, 0.0)
                fi = pi[:, 0:LANES]
                fg = pg[:, 0:LANES]
                for k in range(1, w // LANES):
                    fi = fi + pi[:, k * LANES:(k + 1) * LANES]
                    fg = fg + pg[:, k * LANES:(k + 1) * LANES]
                acc_i_ref[sb] = acc_i_ref[sb] + fi
                acc_g_ref[sb] = acc_g_ref[sb] + fg
                return 0

            lax.fori_loop(sb_lo, sb_hi + 1, slot_block, 0)

        return 0

    lax.fori_loop(0, n_chunks, chunk, 0)
    for sb in range(n_sb):
        idx_ref[sb * jb:(sb + 1) * jb, :] = jnp.sum(acc_i_ref[sb], axis=-1, keepdims=True).astype(jnp.int32)
        gate_ref[sb * jb:(sb + 1) * jb, :] = jnp.sum(acc_g_ref[sb], axis=-1, keepdims=True)


def _route(aff_t, cap):
    bsz, n_exp, n = aff_t.shape
    jb = min(cap, LANES)
    w = min(n, ROUTE_CHUNK)
    key, cend = pl.pallas_call(
        functools.partial(_route_select_kernel, cap=cap),
        out_shape=(jax.ShapeDtypeStruct((bsz, n_exp, n), F32),
                   jax.ShapeDtypeStruct((bsz, n_exp, LANES), jnp.int32)),
        grid=(bsz,),
        in_specs=[pl.BlockSpec((None, n_exp, n), lambda b: (b, 0, 0))],
        out_specs=(pl.BlockSpec((None, n_exp, n), lambda b: (b, 0, 0)),
                   pl.BlockSpec((None, n_exp, LANES), lambda b: (b, 0, 0))),
        compiler_params=_cparams(("parallel",)),
        name="route_select",
    )(aff_t)
    chunked = pl.BlockSpec((None, None, n // w, w), lambda b, e, cend: (b, e, 0, 0))
    out_blk = pl.BlockSpec((None, None, cap, 1), lambda b, e, cend: (b, e, 0, 0))
    grid_spec = pltpu.PrefetchScalarGridSpec(
        num_scalar_prefetch=1,
        grid=(bsz, n_exp),
        in_specs=[chunked, chunked],
        out_specs=(out_blk, out_blk),
        scratch_shapes=[pltpu.VMEM((cap // jb, jb, LANES), F32)] * 2,
    )
    return pl.pallas_call(
        functools.partial(_route_compact_kernel, cap=cap, jb=jb),
        out_shape=(jax.ShapeDtypeStruct((bsz, n_exp, cap, 1), jnp.int32),
                   jax.ShapeDtypeStruct((bsz, n_exp, cap, 1), F32)),
        grid_spec=grid_spec,
        compiler_params=_cparams(("parallel", "parallel")),
        name="route_compact",
    )(cend.reshape(-1), key.reshape(bsz, n_exp, n // w, w), aff_t.reshape(bsz, n_exp, n // w, w))


GATHER_UNROLL = 8
BF16_ROWS = 16
EXPERT_ROW_CHUNKS = 4
GATHER_PARTS = 4 * EXPERT_ROW_CHUNKS


def _tile_copy(h_hbm, xs_ref, sem, b, tok, row):
    return pltpu.make_async_copy(
        h_hbm.at[b, pl.ds(pl.multiple_of(tok * SUBLANES, SUBLANES), SUBLANES), :],
        xs_ref.at[pl.ds(pl.multiple_of(row * SUBLANES, SUBLANES), SUBLANES), :],
        sem)


def _gather_start(idx_refs, h_refs, caps, xs_ref, sem, e, b, n_exp):
    row0 = 0
    for idx_ref, h_hbm, cap in zip(idx_refs, h_refs, caps):
        base = (b * n_exp + e) * cap
        unroll = math.gcd(cap, GATHER_UNROLL)

        def issue(g, _, idx_ref=idx_ref, h_hbm=h_hbm, base=base, unroll=unroll, row0=row0):
            for u in range(unroll):
                i = g * unroll + u
                _tile_copy(h_hbm, xs_ref, sem, b, idx_ref[base + i], row0 + i).start(priority=u % 2)
            return 0

        lax.fori_loop(0, cap // unroll, issue, 0)
        row0 += cap


def _gather_inline(idx_refs, h_refs, caps, xs_ref, sem, e, b, n_exp, part):
    row0 = 0
    for idx_ref, h_hbm, cap in zip(idx_refs, h_refs, caps):
        base = (b * n_exp + e) * cap
        for k, i in enumerate(range(part, cap, GATHER_PARTS)):
            _tile_copy(h_hbm, xs_ref, sem, b, idx_ref[base + i], row0 + i).start(priority=k % 2)
        row0 += cap


def _expert_kernel(*refs, caps, d):
    n_sets = len(caps)
    idx_refs = refs[:n_sets]
    h_refs = refs[n_sets:2 * n_sets]
    wg_ref, wu_ref, wd_ref = refs[2 * n_sets:2 * n_sets + 3]
    gate_refs = refs[2 * n_sets + 3:3 * n_sets + 3]
    ys_refs = refs[3 * n_sets + 3:4 * n_sets + 3]
    xs_a, xs_b, sem = refs[4 * n_sets + 3:]
    rows = sum(caps)
    e = pl.program_id(0)
    b = pl.program_id(1)
    n_exp = pl.num_programs(0)
    n_b = pl.num_programs(1)
    total = n_exp * n_b
    step = e * n_b + b
    slot = lax.rem(step, 2)

    def wait_all(buf, s):
        pltpu.make_async_copy(h_refs[0].at[b, pl.ds(0, rows * SUBLANES), :], buf, sem.at[s]).wait()

    @pl.when(step == 0)
    def _first():
        _gather_start(idx_refs, h_refs, caps, xs_a, sem.at[0], e, b, n_exp)

    nxt = jnp.where(step + 1 < total, step + 1, 0)
    e1 = lax.div(nxt, n_b)
    b1 = lax.rem(nxt, n_b)

    def body(cur, nxt_buf, s_cur, s_nxt):
        def start_part(part):
            _gather_inline(idx_refs, h_refs, caps, nxt_buf, sem.at[s_nxt], e1, b1, n_exp, part)

        wait_all(cur, s_cur)
        wg = wg_ref[...].astype(BF16)
        wu = wu_ref[...].astype(BF16)
        wd = wd_ref[...].astype(BF16)
        bounds = [0]
        for cap in caps:
            bounds.append(bounds[-1] + cap)
        n_chunks = min(EXPERT_ROW_CHUNKS, rows // BF16_ROWS)
        groups = rows // BF16_ROWS
        edges = [BF16_ROWS * (groups * c // n_chunks) for c in range(n_chunks)] + [rows]
        for c in range(n_chunks):
            r0, r1 = edges[c], edges[c + 1]
            chunk = r1 - r0
            x = jnp.concatenate(
                [cur[pl.ds(r0 * SUBLANES + j, chunk, stride=SUBLANES), :] for j in range(d // LANES)],
                axis=1).astype(BF16)
            start_part(4 * c)
            hid = _silu(jnp.dot(x, wg, preferred_element_type=F32))
            start_part(4 * c + 1)
            hid = hid * jnp.dot(x, wu, preferred_element_type=F32)
            start_part(4 * c + 2)
            y = jnp.dot(hid.astype(BF16), wd, preferred_element_type=F32)
            start_part(4 * c + 3)
            for s in range(n_sets):
                lo, hi = max(r0, bounds[s]), min(r1, bounds[s + 1])
                if lo < hi:
                    ys = y[lo - r0:hi - r0] * gate_refs[s][lo - bounds[s]:hi - bounds[s], :]
                    for j in range(d // LANES):
                        ys_refs[s][pl.ds((lo - bounds[s]) * SUBLANES + j, hi - lo, stride=SUBLANES), :] = (
                            ys[:, j * LANES:(j + 1) * LANES])
        for part in range(4 * n_chunks, GATHER_PARTS):
            start_part(part)

        @pl.when(step == total - 1)
        def _drain():
            wait_all(nxt_buf, s_nxt)

    @pl.when(slot == 0)
    def _even():
        body(xs_a, xs_b, 0, 1)

    @pl.when(slot == 1)
    def _odd():
        body(xs_b, xs_a, 1, 0)


def _experts(token_sets, wg, wu, wd, layer):
    caps = tuple(ts[3] for ts in token_sets)
    n_sets = len(token_sets)
    bsz = token_sets[0][1].shape[0]
    _, n_exp, d, f = wg.shape
    assert token_sets[0][1].shape[1] >= sum(caps) * SUBLANES
    wspec = lambda rows, cols: pl.BlockSpec((None, None, rows, cols), lambda e, b, *idx: (layer, e, 0, 0))
    grid_spec = pltpu.PrefetchScalarGridSpec(
        num_scalar_prefetch=n_sets,
        grid=(n_exp, bsz),
        in_specs=[pl.BlockSpec(memory_space=pl.ANY)] * n_sets + [wspec(d, f), wspec(d, f), wspec(f, d)] + [
            pl.BlockSpec((None, None, cap, 1), lambda e, b, *idx: (b, e, 0, 0)) for cap in caps],
        out_specs=tuple(pl.BlockSpec((None, None, cap * SUBLANES, LANES), lambda e, b, *idx: (b, e, 0, 0))
                        for cap in caps),
        scratch_shapes=[pltpu.VMEM((sum(caps) * SUBLANES, LANES), F32)] * 2 + [pltpu.SemaphoreType.DMA((2,))],
    )
    return pl.pallas_call(
        functools.partial(_expert_kernel, caps=caps, d=d),
        out_shape=tuple(jax.ShapeDtypeStruct((bsz, n_exp, cap * SUBLANES, LANES), F32) for cap in caps),
        grid_spec=grid_spec,
        compiler_params=_cparams(("arbitrary", "arbitrary")),
        name="experts",
    )(*[ts[0] for ts in token_sets], *[ts[1] for ts in token_sets], wg, wu, wd, *[ts[2] for ts in token_sets])


def _combine_kernel(idx_ref, ys_ref, o_hbm, acc_ref, sem, *, cap, unroll):
    b = pl.program_id(0)
    e = pl.program_id(1)
    n_exp = pl.num_programs(1)
    base = (b * n_exp + e) * cap

    @pl.when(e == 0)
    def _zero():
        acc_ref[...] = jnp.zeros_like(acc_ref)

    def rows(g, _):
        i0 = g * unroll
        toks = [pl.multiple_of(idx_ref[base + i0 + u] * SUBLANES, SUBLANES) for u in range(unroll)]
        sums = [acc_ref[pl.ds(toks[u], SUBLANES), :]
                + ys_ref[pl.ds(pl.multiple_of((i0 + u) * SUBLANES, SUBLANES), SUBLANES), :]
                for u in range(unroll)]
        for u in range(unroll):
            acc_ref[pl.ds(toks[u], SUBLANES), :] = sums[u]
        return 0

    lax.fori_loop(0, cap // unroll, rows, 0)

    @pl.when(e == n_exp - 1)
    def _flush():
        cp = pltpu.make_async_copy(acc_ref, o_hbm.at[b], sem)
        cp.start()
        cp.wait()


def _combine(idx_flat, ys_tiles, nt, cap):
    bsz, n_exp = ys_tiles.shape[:2]
    grid_spec = pltpu.PrefetchScalarGridSpec(
        num_scalar_prefetch=1,
        grid=(bsz, n_exp),
        in_specs=[pl.BlockSpec((None, None, cap * SUBLANES, LANES), lambda b, e, idx: (b, e, 0, 0))],
        out_specs=pl.BlockSpec(memory_space=pl.ANY),
        scratch_shapes=[pltpu.VMEM((nt * SUBLANES, LANES), F32), pltpu.SemaphoreType.DMA(())],
    )
    return pl.pallas_call(
        functools.partial(_combine_kernel, cap=cap, unroll=4),
        out_shape=jax.ShapeDtypeStruct((bsz, nt * SUBLANES, LANES), F32),
        grid_spec=grid_spec,
        compiler_params=_cparams(("arbitrary", "arbitrary")),
        name="combine",
    )(idx_flat, ys_tiles)


def _residual_kernel(x_ref, m_ref, g_ref, o_ref, *, tm):
    d = x_ref.shape[-1]
    moe = jnp.concatenate(
        [m_ref[pl.ds(j, tm, stride=SUBLANES), :] for j in range(d // LANES)], axis=1)
    o_ref[...] = x_ref[...] + g_ref[...] * moe


def _residual(x, moe_tiles, g2, tm):
    bsz, nt, d = x.shape
    return pl.pallas_call(
        functools.partial(_residual_kernel, tm=tm),
        out_shape=jax.ShapeDtypeStruct((bsz, nt, d), F32),
        grid=(bsz, nt // tm),
        in_specs=[
            pl.BlockSpec((None, tm, d), lambda b, i: (b, i, 0)),
            pl.BlockSpec((None, tm * SUBLANES, LANES), lambda b, i: (b, i, 0)),
            pl.BlockSpec((None, 1, d), lambda b, i: (b, 0, 0)),
        ],
        out_specs=pl.BlockSpec((None, tm, d), lambda b, i: (b, i, 0)),
        compiler_params=_cparams(("parallel", "parallel")),
        name="residual",
    )(x, moe_tiles, g2)


def _prepare_layer(l, w_in, sgu_norm, w_sgu, b_sgu, q_lora_norm, w_uq, kv_lora_norm, w_ukv, q_norm, k_norm,
                   w_out, w_router, w_gate, w_up, w_down):
    w_in_l = w_in[l]
    kr_cols = w_in_l[:, OFF_KR:OFF_KR + QK_ROPE]
    w_in_ext = jnp.concatenate([w_in_l[:, :OFF_KR], kr_cols, kr_cols], axis=1).astype(BF16)
    uq = w_uq[l].reshape(Q_LORA, MLA_HEADS, QK_DIM)
    w_uq_p = jnp.concatenate(
        [uq[:, :, :QK_NOPE].reshape(Q_LORA, MLA_HEADS * QK_NOPE),
         uq[:, :, QK_NOPE:].reshape(Q_LORA, MLA_HEADS * QK_ROPE)], axis=1).astype(BF16)

    def norm_pad(v):
        return jnp.concatenate([v[:QK_NOPE], v[QK_NOPE:], v[QK_NOPE:]])[None, :]

    wr_t = w_router[l].T
    wr_hi = wr_t.astype(BF16)
    wr_lo = (wr_t - wr_hi.astype(F32)).astype(BF16)
    return dict(
        w_in=w_in_ext,
        sgu_norm=sgu_norm[l][None, :],
        w_sgu=jnp.concatenate([w_sgu[l][hh] for hh in range(SGU_HEADS)], axis=1).astype(BF16),
        b_sgu=jnp.repeat(b_sgu[l].T, SGU_HEAD_DIM, axis=1),
        fdft=jnp.asarray(_channel_dft_table()).astype(BF16),
        q_lora_norm=q_lora_norm[l][None, :],
        w_uq=w_uq_p,
        q_norm=norm_pad(q_norm[l]),
        kv_lora_norm=kv_lora_norm[l][None, :],
        w_ukv=w_ukv[l].astype(BF16),
        k_norm=norm_pad(k_norm[l]),
        w_out=w_out[l].astype(BF16),
        wr_hi=wr_hi, wr_lo=wr_lo,
        w_gate=w_gate, w_up=w_up, w_down=w_down, layer=l,
    )


def _rope_tables(n):
    pos = np.arange(n)
    half = QK_ROPE // 4
    freqs = ROPE_THETA ** (-np.arange(half, dtype=np.float64) / half)
    ang_r = (pos // GRID_WIDTH)[:, None] * freqs
    ang_c = (pos % GRID_WIDTH)[:, None] * freqs
    cos64 = np.concatenate([np.cos(ang_r), np.cos(ang_r), np.cos(ang_c), np.cos(ang_c)], axis=1)
    sin64 = np.concatenate([-np.sin(ang_r), np.sin(ang_r), -np.sin(ang_c), np.sin(ang_c)], axis=1)
    return (jnp.asarray(np.tile(cos64, (1, 2)), F32), jnp.asarray(np.tile(sin64, (1, 2)), F32))


def _token_tile(nt, pref):
    return pref if nt % pref == 0 else nt


def _moe(populations, lw):
    token_sets = []
    for h2_tiles, aff_t in populations:
        nt = aff_t.shape[-1]
        cap = EC_FACTOR * nt // N_EXPERTS
        idx, gate = _route(aff_t, cap)
        token_sets.append((idx.reshape(-1), h2_tiles, gate, cap))
    ys = _experts(token_sets, lw["w_gate"], lw["w_up"], lw["w_down"], lw["layer"])
    return [_combine(ts[0], y, pop[1].shape[-1], ts[3]) for ts, y, pop in zip(token_sets, ys, populations)]


def kernel(x, c, ctx, c_ctx, w_ada, b_ada, w_in, sgu_norm, w_sgu, b_sgu, q_lora_norm, w_uq, kv_lora_norm, w_ukv,
           q_norm, k_norm, w_out, w_router, w_gate, w_up, w_down):
    bsz, n, d = x.shape
    n_ctx = ctx.shape[1]
    depth = w_ada.shape[0]
    assert bsz + 1 <= SUBLANES

    cond = jnp.zeros((SUBLANES, d), F32).at[:bsz].set(c).at[bsz].set(c_ctx)
    mod = _ada_modulation(cond, w_ada, b_ada)
    rope_tabs = _rope_tables(n)
    tm = _token_tile(n, 1024)
    tm_in = _token_tile(n, 1024)
    tm_c = _token_tile(n_ctx, 256)

    xc = ctx
    pending = pending_c = None
    for l in range(depth):
        last = l == depth - 1
        lw = _prepare_layer(l, w_in, sgu_norm, w_sgu, b_sgu, q_lora_norm, w_uq, kv_lora_norm, w_ukv,
                            q_norm, k_norm, w_out, w_router, w_gate, w_up, w_down)
        parts = [mod[l, :, i * d:(i + 1) * d] for i in range(6)]
        lat = [p[:bsz, None, :] for p in parts]
        cx = [jnp.broadcast_to(p[bsz][None, None, :], (bsz, 1, d)) for p in parts]

        outs_c = _mix_in(xc, cx[0], cx[1], lw, None, tm_c, pending_c)
        outs = _mix_in(x, lat[0], lat[1], lw, rope_tabs, tm_in, pending)
        ya_c, a_c, b_c, q_c, k_c, v_c = outs_c[:6]
        ya, a, b, q, k, v = outs[:6]
        if pending is not None:
            xc, x = outs_c[6], outs[6]
        yb = _fourier_positions(a, b)
        tk = _token_tile(n, ATTN_TK)
        vt_c = jnp.swapaxes(v_c, 2, 3)
        vt = jnp.swapaxes(v.reshape(bsz, MLA_HEADS, n // tk, tk, V_DIM), 3, 4)
        yc = _attention(jnp.swapaxes(q, 2, 3), k_c, vt_c, k, vt, tq=_token_tile(n, ATTN_TQ), tk=tk)
        x_new, h2_tiles, aff_t = _mix_out(x, ya, yb, yc, lw["w_out"], lat[2], lat[3], lat[4],
                                          lw["wr_hi"], lw["wr_lo"], tm)
        if not last:
            yb_c = _fourier_positions(a_c, b_c)
            yc_c = _attention(jnp.swapaxes(q_c, 2, 3), k_c, vt_c, None, None,
                              tq=_token_tile(n_ctx, ATTN_TQ), tk=n_ctx)
            xc_new, h2c_tiles, affc_t = _mix_out(xc, ya_c, yb_c, yc_c, lw["w_out"], cx[2], cx[3], cx[4],
                                                 lw["wr_hi"], lw["wr_lo"], tm_c)
            moe, moe_c = _moe([(h2_tiles, aff_t), (h2c_tiles, affc_t)], lw)
            xc, pending_c = xc_new, (moe_c, cx[5])
        else:
            moe, = _moe([(h2_tiles, aff_t)], lw)
        x, pending = x_new, (moe, lat[5])
    return _residual(x, pending[0], pending[1], tm)
```

```python
import functools
import math

import numpy as np
import jax
import jax.numpy as jnp
from jax import lax
from jax.experimental import pallas as pl
from jax.experimental.pallas import tpu as pltpu

F32 = jnp.float32
BF16 = jnp.bfloat16

GRID_WIDTH = 64
SGU_HEADS = 4
SGU_HEAD_DIM = 64
SGU_W = SGU_HEADS * SGU_HEAD_DIM
CHUNK = 128
FNET_GROUP_DIM = 64
FNET_W = 256
MLA_HEADS = 4
QK_NOPE = 128
QK_ROPE = 64
QK_DIM = QK_NOPE + QK_ROPE
V_DIM = 128
Q_LORA = 256
KV_LORA = 128
OFF_V = 256
OFF_F = 512
OFF_Q = 768
OFF_KV = 1024
OFF_KR = 1152
N_EXPERTS = 16
EC_FACTOR = 2
ROPE_THETA = 10000.0
EPS = 1e-6

LANES = 128
SUBLANES = 8
VMEM_LIMIT_BYTES = 56 * 1024 * 1024

QK_PAD = 2 * LANES
ATTN_TQ = 4096
ATTN_TK = 512
LOG2_E = 1.4426950408889634
MIX_ROWS = 128
MIX_OUT_ROWS = 512


def _cparams(sem, vmem=None):
    return pltpu.CompilerParams(dimension_semantics=sem, vmem_limit_bytes=vmem or VMEM_LIMIT_BYTES)


def _rms_scale(v, width):
    return lax.rsqrt(jnp.sum(v * v, axis=-1, keepdims=True) * (1.0 / width) + EPS)


def _gelu_tanh(v):
    return 0.5 * v * (1.0 + jnp.tanh(0.7978845608028654 * (v + 0.044715 * v * v * v)))


def _silu(v):
    return v / (1.0 + jnp.exp(-v))


def _ada_kernel(cond_ref, w_ref, b_ref, o_ref):
    s = _silu(cond_ref[...])
    o_ref[...] = jnp.dot(s, w_ref[...], preferred_element_type=F32,
                         precision=lax.Precision.HIGHEST) + b_ref[...]


def _ada_modulation(cond, w_ada, b_ada):
    depth, d, six_d = w_ada.shape
    tn = 1536
    return pl.pallas_call(
        _ada_kernel,
        out_shape=jax.ShapeDtypeStruct((depth, SUBLANES, six_d), F32),
        grid=(depth, six_d // tn),
        in_specs=[
            pl.BlockSpec((SUBLANES, d), lambda l, j: (0, 0)),
            pl.BlockSpec((None, d, tn), lambda l, j: (l, 0, j)),
            pl.BlockSpec((None, 1, tn), lambda l, j: (l, 0, j)),
        ],
        out_specs=pl.BlockSpec((None, SUBLANES, tn), lambda l, j: (l, 0, j)),
        compiler_params=_cparams(("arbitrary", "arbitrary")),
        name="ada_modulation",
    )(cond, w_ada, b_ada.reshape(depth, 1, six_d))


def _swap16(v):
    lane = lax.broadcasted_iota(jnp.int32, v.shape, 1)
    up = pltpu.roll(v, LANES - 16, axis=1)
    dn = pltpu.roll(v, 16, axis=1)
    return jnp.where((lane % 32) < 16, up, dn)


def _mix_in_kernel(*refs, tm, use_rope, fuse_residual):
    rows_per_group = min(tm, MIX_ROWS)
    for g in range(tm // rows_per_group):
        _mix_in_rows(slice(g * rows_per_group, (g + 1) * rows_per_group), refs, use_rope, fuse_residual)


def _mix_in_rows(rg, refs, use_rope, fuse_residual):
    if fuse_residual:
        x_ref, moe_ref, g2_ref = refs[:3]
        xo_ref = refs[-1]
        refs = refs[:1] + refs[3:-1]
    (x_ref, sh_ref, sc_ref, win_ref, sgun_ref, wsgu_ref, bsgu_ref, fdft_ref, qln_ref, wuq_ref, qn_ref,
     kvn_ref, wukv_ref, kn_ref, cos_ref, sin_ref, ya_ref, a_ref, b_ref, q_ref, k_ref, v_ref) = refs
    x = x_ref[rg, :]
    if fuse_residual:
        nrows = rg.stop - rg.start
        moe = jnp.concatenate(
            [moe_ref[pl.ds(rg.start * SUBLANES + j, nrows, stride=SUBLANES), :]
             for j in range(x.shape[-1] // LANES)], axis=1)
        x = x + g2_ref[...] * moe
        xo_ref[rg, :] = x
    tm = x.shape[0]
    d = x.shape[-1]
    h = x * _rms_scale(x, d) * (1.0 + sc_ref[...]) + sh_ref[...]
    p = jnp.dot(h.astype(BF16), win_ref[...], preferred_element_type=F32)

    u = _gelu_tanh(p[:, 0:OFF_V])
    gv = _gelu_tanh(p[:, OFF_V:OFF_F])
    vn = gv * _rms_scale(gv, SGU_W) * sgun_ref[...]
    head_of_lane = lax.broadcasted_iota(jnp.int32, (CHUNK, SGU_W), 1) // SGU_HEAD_DIM
    for c in range(tm // CHUNK):
        rows = slice(c * CHUNK, (c + 1) * CHUNK)
        vc = vn[rows]
        vstack = jnp.concatenate(
            [jnp.where(head_of_lane == hh, vc, 0.0) for hh in range(SGU_HEADS)], axis=0).astype(BF16)
        z = jnp.dot(wsgu_ref[...], vstack, preferred_element_type=F32) + bsgu_ref[...]
        ya_ref[rg.start + c * CHUNK:rg.start + (c + 1) * CHUNK, :] = (u[rows] * z).astype(ya_ref.dtype)

    ab = jnp.dot(p[:, OFF_F:OFF_Q].astype(BF16), fdft_ref[...], preferred_element_type=F32)
    a_ref[rg, :] = ab[:, 0:FNET_W].astype(a_ref.dtype)
    b_ref[rg, :] = ab[:, FNET_W:2 * FNET_W].astype(b_ref.dtype)

    lane = lax.broadcasted_iota(jnp.int32, (1, LANES), 1)
    lo = lane < QK_ROPE
    if use_rope:
        cos = cos_ref[rg, :]
        sin = sin_ref[rg, :]

    def rope(t):
        return t * cos + _swap16(t) * sin if use_rope else t

    pq = p[:, OFF_Q:OFF_KV]
    cq = pq * _rms_scale(pq, Q_LORA) * qln_ref[...]
    q = jnp.dot(cq.astype(BF16), wuq_ref[...], preferred_element_type=F32)
    qnorm = qn_ref[...]
    for hh in range(MLA_HEADS):
        qn = q[:, LANES * hh:LANES * (hh + 1)]
        pair = hh // 2
        qr = q[:, MLA_HEADS * LANES + LANES * pair:MLA_HEADS * LANES + LANES * (pair + 1)]
        qr = jnp.where(lo if hh % 2 == 0 else jnp.logical_not(lo), qr, 0.0)
        ssq = jnp.sum(qn * qn, axis=-1, keepdims=True) + jnp.sum(qr * qr, axis=-1, keepdims=True)
        rinv = lax.rsqrt(ssq * (1.0 / QK_DIM) + EPS) * (QK_DIM ** -0.5 * LOG2_E)
        q_ref[hh, rg, 0:LANES] = (qn * rinv * qnorm[:, 0:LANES]).astype(q_ref.dtype)
        q_ref[hh, rg, LANES:QK_PAD] = rope(qr * rinv * qnorm[:, LANES:QK_PAD]).astype(q_ref.dtype)

    pkv = p[:, OFF_KV:OFF_KR]
    ckv = pkv * _rms_scale(pkv, KV_LORA) * kvn_ref[...]
    kv = jnp.dot(ckv.astype(BF16), wukv_ref[...], preferred_element_type=F32)
    krr = p[:, OFF_KR:OFF_KR + LANES]
    ssq_kr = 0.5 * jnp.sum(krr * krr, axis=-1, keepdims=True)
    knorm = kn_ref[...]
    rinvs = []
    for hh in range(MLA_HEADS):
        kn = kv[:, 2 * LANES * hh:2 * LANES * hh + LANES]
        ssq = jnp.sum(kn * kn, axis=-1, keepdims=True) + ssq_kr
        rinv = lax.rsqrt(ssq * (1.0 / QK_DIM) + EPS)
        rinvs.append(rinv)
        k_ref[hh, rg, 0:LANES] = (kn * rinv * knorm[:, 0:LANES]).astype(k_ref.dtype)
        v_ref[hh, rg, :] = kv[:, 2 * LANES * hh + LANES:2 * LANES * (hh + 1)].astype(v_ref.dtype)
    for pair in range(MLA_HEADS // 2):
        r2 = jnp.where(lo, rinvs[2 * pair], rinvs[2 * pair + 1])
        kr = rope(krr * r2 * knorm[:, LANES:QK_PAD]).astype(k_ref.dtype)
        k_ref[2 * pair, rg, LANES:QK_PAD] = kr
        k_ref[2 * pair + 1, rg, LANES:QK_PAD] = kr


def _mix_in(x, sh, sc, lw, rope_tabs, tm, pending=None):
    bsz, nt, d = x.shape
    use_rope = rope_tabs is not None
    if use_rope:
        cos_t, sin_t = rope_tabs
    else:
        cos_t = sin_t = jnp.zeros((nt, LANES), F32)
    full = lambda arr: pl.BlockSpec(arr.shape, lambda b, i: (0,) * arr.ndim)
    vec = pl.BlockSpec((None, 1, d), lambda b, i: (b, 0, 0))
    in_w = lw["w_in"].shape[1]
    fuse = pending is not None
    kern = functools.partial(_mix_in_kernel, tm=tm, use_rope=use_rope, fuse_residual=fuse)
    row_blk = pl.BlockSpec((None, tm, d), lambda b, i: (b, i, 0))
    out_shape = [
        jax.ShapeDtypeStruct((bsz, nt, SGU_W), BF16),
        jax.ShapeDtypeStruct((bsz, nt, FNET_W), F32),
        jax.ShapeDtypeStruct((bsz, nt, FNET_W), F32),
        jax.ShapeDtypeStruct((bsz, MLA_HEADS, nt, QK_PAD), BF16),
        jax.ShapeDtypeStruct((bsz, MLA_HEADS, nt, QK_PAD), BF16),
        jax.ShapeDtypeStruct((bsz, MLA_HEADS, nt, V_DIM), BF16),
    ]
    out_specs = [
        pl.BlockSpec((None, tm, SGU_W), lambda b, i: (b, i, 0)),
        pl.BlockSpec((None, tm, FNET_W), lambda b, i: (b, i, 0)),
        pl.BlockSpec((None, tm, FNET_W), lambda b, i: (b, i, 0)),
        pl.BlockSpec((None, MLA_HEADS, tm, QK_PAD), lambda b, i: (b, 0, i, 0)),
        pl.BlockSpec((None, MLA_HEADS, tm, QK_PAD), lambda b, i: (b, 0, i, 0)),
        pl.BlockSpec((None, MLA_HEADS, tm, V_DIM), lambda b, i: (b, 0, i, 0)),
    ]
    in_specs = [row_blk]
    args = [x]
    if fuse:
        in_specs += [pl.BlockSpec((None, tm * SUBLANES, LANES), lambda b, i: (b, i, 0)), vec]
        args += list(pending)
        out_shape.append(jax.ShapeDtypeStruct((bsz, nt, d), F32))
        out_specs.append(row_blk)
    in_specs += [
        vec, vec,
        full(lw["w_in"]), full(lw["sgu_norm"]), full(lw["w_sgu"]), full(lw["b_sgu"]), full(lw["fdft"]),
        full(lw["q_lora_norm"]), full(lw["w_uq"]), full(lw["q_norm"]),
        full(lw["kv_lora_norm"]), full(lw["w_ukv"]), full(lw["k_norm"]),
        pl.BlockSpec((tm, LANES), lambda b, i: (i, 0)),
        pl.BlockSpec((tm, LANES), lambda b, i: (i, 0)),
    ]
    args += [sh, sc, lw["w_in"], lw["sgu_norm"], lw["w_sgu"], lw["b_sgu"], lw["fdft"],
             lw["q_lora_norm"], lw["w_uq"], lw["q_norm"], lw["kv_lora_norm"], lw["w_ukv"], lw["k_norm"],
             cos_t, sin_t]
    return pl.pallas_call(
        kern,
        out_shape=tuple(out_shape),
        grid=(bsz, nt // tm),
        in_specs=in_specs,
        out_specs=tuple(out_specs),
        compiler_params=_cparams(("parallel", "parallel")),
        name="mix_in",
    )(*args)


DFT_UNROLL = 16


def _dft_kernel(a_ref, b_ref, w_ref, m_ref, o_ref, yr_ref, yi_ref, *, n1, n2):
    w1 = w_ref[...]
    c = a_ref.shape[-1]

    u1 = math.gcd(n2, DFT_UNROLL)
    u2 = math.gcd(n1, DFT_UNROLL)

    def step1(g, _):
        cols = []
        for u in range(u1):
            rows = pl.ds(g * u1 + u, n1, stride=n2)
            cols.append(jnp.concatenate([a_ref[rows, :], b_ref[rows, :]], axis=0))
        r = jnp.dot(w1, jnp.concatenate(cols, axis=1).astype(BF16), preferred_element_type=F32)
        for u in range(u1):
            out_rows = pl.ds(pl.multiple_of((g * u1 + u) * n1, n1), n1)
            yr_ref[out_rows, :] = r[0:n1, u * c:(u + 1) * c]
            yi_ref[out_rows, :] = r[n1:2 * n1, u * c:(u + 1) * c]
        return 0

    lax.fori_loop(0, n2 // u1, step1, 0)

    def step2(g, _):
        for u in range(u2):
            k1 = g * u2 + u
            rows = pl.ds(k1, n2, stride=n1)
            ycat = jnp.concatenate([yr_ref[rows, :], yi_ref[rows, :]], axis=0).astype(BF16)
            o_ref[rows, :] = jnp.dot(m_ref[k1], ycat, preferred_element_type=F32)
        return 0

    lax.fori_loop(0, n1 // u2, step2, 0)


def _dft_factors(n):
    n1 = 64 if n >= 1024 else 32
    return n1, n // n1


@functools.lru_cache(maxsize=None)
def _dft_tables(n):
    n1, n2 = _dft_factors(n)
    j = np.arange(n1, dtype=np.float64)
    ang1 = 2.0 * np.pi * np.outer(j, j) / n1
    c1 = np.cos(ang1) / math.sqrt(n1)
    s1 = np.sin(ang1) / math.sqrt(n1)
    w1 = np.block([[c1, -s1], [s1, c1]])
    k1 = np.arange(n1, dtype=np.float64)[:, None, None]
    k2 = np.arange(n2, dtype=np.float64)[None, :, None]
    m2 = np.arange(n2, dtype=np.float64)[None, None, :]
    ang2 = 2.0 * np.pi * m2 * (k1 + n1 * k2) / n
    m2cat = np.concatenate([np.cos(ang2), -np.sin(ang2)], axis=-1) / math.sqrt(n2)
    return w1.astype(np.float32), m2cat.astype(np.float32)


@functools.lru_cache(maxsize=None)
def _channel_dft_table():
    j = np.arange(FNET_GROUP_DIM, dtype=np.float64)
    ang = 2.0 * np.pi * np.outer(j, j) / FNET_GROUP_DIM
    groups = FNET_W // FNET_GROUP_DIM
    cg = np.kron(np.eye(groups), np.cos(ang)) / math.sqrt(FNET_GROUP_DIM)
    sg = np.kron(np.eye(groups), np.sin(ang)) / math.sqrt(FNET_GROUP_DIM)
    return np.concatenate([cg, sg], axis=1).astype(np.float32)


def _fourier_positions(a, b):
    bsz, n, c = a.shape
    n1, n2 = _dft_factors(n)
    w1_np, m2_np = _dft_tables(n)
    w1 = jnp.asarray(w1_np).astype(BF16)
    m2cat = jnp.asarray(m2_np).astype(BF16)
    cb = LANES
    blk = pl.BlockSpec((None, n, cb), lambda bi, j: (bi, 0, j))
    return pl.pallas_call(
        functools.partial(_dft_kernel, n1=n1, n2=n2),
        out_shape=jax.ShapeDtypeStruct((bsz, n, c), F32),
        grid=(bsz, c // cb),
        in_specs=[blk, blk,
                  pl.BlockSpec((2 * n1, 2 * n1), lambda bi, j: (0, 0)),
                  pl.BlockSpec((n1, n2, 2 * n2), lambda bi, j: (0, 0, 0))],
        out_specs=blk,
        scratch_shapes=[pltpu.VMEM((n, cb), F32)] * 2,
        compiler_params=_cparams(("parallel", "parallel")),
        name="fourier",
    )(a, b, w1, m2cat)


NEG_BIG = -1e30


def _attn_kernel(*refs, tk, n_lat_blocks):
    if n_lat_blocks:
        qt_ref, kc_ref, vtc_ref, kl_ref, vtl_ref, o_ref, acc_ref, sa_ref, sb_ref = refs
    else:
        qt_ref, kc_ref, vtc_ref, o_ref, acc_ref = refs
    qt = qt_ref[...]
    tq = qt.shape[1]

    def scores(kj):
        return jnp.dot(kj, qt, preferred_element_type=F32)

    def consume(s, vtj, m, l, first):
        m_new = jnp.maximum(m, jnp.max(s, axis=0, keepdims=True))
        alpha = jnp.exp2(m - m_new)
        p = jnp.exp2(s - m_new)
        l_new = alpha * l + jnp.sum(p, axis=0, keepdims=True)
        pv = jnp.dot(vtj, p.astype(BF16), preferred_element_type=F32)
        acc_ref[...] = pv if first else alpha * acc_ref[...] + pv
        return m_new, l_new

    def lat_keys(j):
        return kl_ref[pl.ds(pl.multiple_of(j * tk, tk), tk), :]

    m0 = jnp.full((1, tq), NEG_BIG, F32)
    l0 = jnp.zeros((1, tq), F32)
    if n_lat_blocks:
        sa_ref[...] = scores(lat_keys(0))
    m, l = consume(scores(kc_ref[...]), vtc_ref[...], m0, l0, True)
    if n_lat_blocks:
        def body(i, carry):
            sb_ref[...] = scores(lat_keys(2 * i + 1))
            m, l = consume(sa_ref[...], vtl_ref[2 * i], carry[0], carry[1], False)
            sa_ref[...] = scores(lat_keys(2 * i + 2))
            return consume(sb_ref[...], vtl_ref[2 * i + 1], m, l, False)

        m, l = lax.fori_loop(0, n_lat_blocks // 2 - 1, body, (m, l))
        sb_ref[...] = scores(lat_keys(n_lat_blocks - 1))
        m, l = consume(sa_ref[...], vtl_ref[n_lat_blocks - 2], m, l, False)
        m, l = consume(sb_ref[...], vtl_ref[n_lat_blocks - 1], m, l, False)
    o_ref[...] = jnp.transpose(acc_ref[...] / l).astype(o_ref.dtype)


def _attention(qt, k_ctx, vt_ctx, k_lat, vt_lat, tq, tk):
    bsz, heads, _, nq = qt.shape
    m_ctx = k_ctx.shape[2]
    n_lat_blocks = 0 if k_lat is None else k_lat.shape[2] // tk
    in_specs = [
        pl.BlockSpec((None, None, QK_PAD, tq), lambda b, h, i: (b, h, 0, i)),
        pl.BlockSpec((None, None, m_ctx, QK_PAD), lambda b, h, i: (b, h, 0, 0)),
        pl.BlockSpec((None, None, V_DIM, m_ctx), lambda b, h, i: (b, h, 0, 0)),
    ]
    args = [qt, k_ctx, vt_ctx]
    if n_lat_blocks:
        in_specs += [
            pl.BlockSpec((None, None, k_lat.shape[2], QK_PAD), lambda b, h, i: (b, h, 0, 0)),
            pl.BlockSpec((None, None, n_lat_blocks, V_DIM, tk), lambda b, h, i: (b, h, 0, 0, 0)),
        ]
        args += [k_lat, vt_lat]
    return pl.pallas_call(
        functools.partial(_attn_kernel, tk=tk, n_lat_blocks=n_lat_blocks),
        out_shape=jax.ShapeDtypeStruct((bsz, nq, heads * V_DIM), BF16),
        grid=(bsz, heads, nq // tq),
        in_specs=in_specs,
        out_specs=pl.BlockSpec((None, tq, V_DIM), lambda b, h, i: (b, i, h)),
        scratch_shapes=[pltpu.VMEM((V_DIM, tq), F32)] + [pltpu.VMEM((tk, tq), F32)] * (2 if n_lat_blocks else 0),
        compiler_params=_cparams(("parallel", "parallel", "arbitrary")),
        name="attention",
    )(*args)


def _mix_out_kernel(x_ref, ya_ref, yb_ref, yc_ref, wo_ref, g1_ref, sh_ref, sc_ref, wrh_ref, wrl_ref,
                    xn_ref, h2_ref, aff_ref, *, tm):
    rows_per_group = min(tm, MIX_OUT_ROWS)
    for g in range(tm // rows_per_group):
        _mix_out_rows(g * rows_per_group, rows_per_group, x_ref, ya_ref, yb_ref, yc_ref, wo_ref, g1_ref, sh_ref,
                      sc_ref, wrh_ref, wrl_ref, xn_ref, h2_ref, aff_ref)


def _mix_out_rows(r0, nr, x_ref, ya_ref, yb_ref, yc_ref, wo_ref, g1_ref, sh_ref, sc_ref, wrh_ref, wrl_ref,
                  xn_ref, h2_ref, aff_ref):
    rg = slice(r0, r0 + nr)
    mix = jnp.dot(ya_ref[rg, :], wo_ref[0:SGU_W, :], preferred_element_type=F32)
    mix += jnp.dot(yb_ref[rg, :].astype(BF16), wo_ref[SGU_W:SGU_W + FNET_W, :], preferred_element_type=F32)
    mix += jnp.dot(yc_ref[rg, :], wo_ref[SGU_W + FNET_W:, :], preferred_element_type=F32)
    xn = x_ref[rg, :] + g1_ref[...] * mix
    xn_ref[rg, :] = xn
    d = xn.shape[-1]
    h2 = xn * _rms_scale(xn, d) * (1.0 + sc_ref[...]) + sh_ref[...]
    for j in range(d // LANES):
        h2_ref[pl.ds(r0 * SUBLANES + j, nr, stride=SUBLANES), :] = h2[:, j * LANES:(j + 1) * LANES]
    h_hi = h2.astype(BF16)
    h_lo = (h2 - h_hi.astype(F32)).astype(BF16)
    nt_dims = (((1,), (1,)), ((), ()))
    logits = lax.dot_general(wrh_ref[...], h_hi, nt_dims, preferred_element_type=F32)
    logits += lax.dot_general(wrh_ref[...], h_lo, nt_dims, preferred_element_type=F32)
    logits += lax.dot_general(wrl_ref[...], h_hi, nt_dims, preferred_element_type=F32)
    e = jnp.exp(logits - jnp.max(logits, axis=0, keepdims=True))
    aff_ref[:, rg] = e / jnp.sum(e, axis=0, keepdims=True)


def _mix_out(x, ya, yb, yc, w_out, g1, sh2, sc2, wr_hi, wr_lo, tm):
    bsz, nt, d = x.shape
    vec = pl.BlockSpec((None, 1, d), lambda b, i: (b, 0, 0))
    full = lambda arr: pl.BlockSpec(arr.shape, lambda b, i: (0,) * arr.ndim)
    return pl.pallas_call(
        functools.partial(_mix_out_kernel, tm=tm),
        out_shape=(
            jax.ShapeDtypeStruct((bsz, nt, d), F32),
            jax.ShapeDtypeStruct((bsz, nt * SUBLANES, LANES), F32),
            jax.ShapeDtypeStruct((bsz, N_EXPERTS, nt), F32),
        ),
        grid=(bsz, nt // tm),
        in_specs=[
            pl.BlockSpec((None, tm, d), lambda b, i: (b, i, 0)),
            pl.BlockSpec((None, tm, SGU_W), lambda b, i: (b, i, 0)),
            pl.BlockSpec((None, tm, FNET_W), lambda b, i: (b, i, 0)),
            pl.BlockSpec((None, tm, MLA_HEADS * V_DIM), lambda b, i: (b, i, 0)),
            full(w_out), vec, vec, vec, full(wr_hi), full(wr_lo),
        ],
        out_specs=(
            pl.BlockSpec((None, tm, d), lambda b, i: (b, i, 0)),
            pl.BlockSpec((None, tm * SUBLANES, LANES), lambda b, i: (b, i, 0)),
            pl.BlockSpec((None, N_EXPERTS, tm), lambda b, i: (b, 0, i)),
        ),
        compiler_params=_cparams(("parallel", "parallel")),
        name="mix_out",
    )(x, ya, yb, yc, w_out, g1, sh2, sc2, wr_hi, wr_lo)


def _prefix_count(mask, tri):
    e, n = mask.shape
    w = tri.shape[0]
    carry = jnp.zeros((e, 1), F32)
    outs = []
    ends = []
    for c in range(n // w):
        local = jnp.dot(mask[:, c * w:(c + 1) * w].astype(BF16), tri, preferred_element_type=F32)
        outs.append(local + carry)
        carry = carry + local[:, w - 1:w]
        ends.append(carry)
    return jnp.concatenate(outs, axis=1), ends


ROUTE_CHUNK = 4 * LANES


def _route_select_kernel(aff_ref, key_ref, cend_ref, *, cap):
    n_exp, n = aff_ref.shape
    w = min(n, ROUTE_CHUNK)
    aff = aff_ref[...]

    def step(i, thr_bits):
        cand = thr_bits | jnp.left_shift(jnp.int32(1), 30 - i)
        cnt = jnp.sum((aff >= pltpu.bitcast(cand, F32)).astype(jnp.int32), axis=-1, keepdims=True)
        return jnp.where(cnt >= cap, cand, thr_bits)

    thr = pltpu.bitcast(lax.fori_loop(0, 31, step, jnp.zeros((n_exp, 1), jnp.int32)), F32)
    gt = aff > thr
    eq = aff == thr
    need = cap - jnp.sum(gt.astype(jnp.int32), axis=-1, keepdims=True)
    r = lax.broadcasted_iota(jnp.int32, (w, w), 0)
    c = lax.broadcasted_iota(jnp.int32, (w, w), 1)
    tri = (r <= c).astype(BF16)
    eq_rank, _ = _prefix_count(eq.astype(F32), tri)
    sel = jnp.logical_or(gt, jnp.logical_and(eq, eq_rank <= need.astype(F32)))
    cum, ends = _prefix_count(sel.astype(F32), tri)
    key_ref[...] = jnp.where(sel, cum, 0.0)
    lane = lax.broadcasted_iota(jnp.int32, (n_exp, LANES), 1)
    cend = jnp.zeros((n_exp, LANES), F32)
    for ci, end in enumerate(ends):
        cend = jnp.where(lane == ci, end, cend)
    cend_ref[...] = cend.astype(jnp.int32)


def _route_compact_kernel(cend_ref, key_ref, aff_ref, idx_ref, gate_ref, acc_i_ref, acc_g_ref, *, cap, jb):
    b = pl.program_id(0)
    e = pl.program_id(1)
    n_chunks, w = key_ref.shape
    n_sb = cap // jb
    base = (b * pl.num_programs(1) + e) * LANES
    shift = jb.bit_length() - 1
    acc_i_ref[...] = jnp.zeros_like(acc_i_ref)
    acc_g_ref[...] = jnp.zeros_like(acc_g_ref)

    def chunk(c, _):
        c_start = jnp.where(c > 0, cend_ref[base + jnp.maximum(c - 1, 0)], 0)
        c_end = cend_ref[base + c]

        @pl.when(c_end > c_start)
        def _any_selected():
            krow = key_ref[pl.ds(c, 1), :]
            arow = aff_ref[pl.ds(c, 1), :]
            tpos = (lax.broadcasted_iota(jnp.int32, (1, w), 1) + c * w).astype(F32)
            sb_lo = lax.shift_right_logical(c_start, shift)
            sb_hi = jnp.minimum(lax.shift_right_logical(c_end - 1, shift), n_sb - 1)

            def slot_block(sb, _):
                slot = (lax.broadcasted_iota(jnp.int32, (jb, 1), 0) + (sb * jb + 1)).astype(F32)
                hit = krow == slot
                pi = jnp.where(hit, tpos, 0.0)
                pg = jnp.where(hit, arow, 0.0)
                fi = pi[:, 0:LANES]
                fg = pg[:, 0:LANES]
                for k in range(1, w // LANES):
                    fi = fi + pi[:, k * LANES:(k + 1) * LANES]
                    fg = fg + pg[:, k * LANES:(k + 1) * LANES]
                acc_i_ref[sb] = acc_i_ref[sb] + fi
                acc_g_ref[sb] = acc_g_ref[sb] + fg
                return 0

            lax.fori_loop(sb_lo, sb_hi + 1, slot_block, 0)

        return 0

    lax.fori_loop(0, n_chunks, chunk, 0)
    for sb in range(n_sb):
        idx_ref[sb * jb:(sb + 1) * jb, :] = jnp.sum(acc_i_ref[sb], axis=-1, keepdims=True).astype(jnp.int32)
        gate_ref[sb * jb:(sb + 1) * jb, :] = jnp.sum(acc_g_ref[sb], axis=-1, keepdims=True)


def _route(aff_t, cap):
    bsz, n_exp, n = aff_t.shape
    jb = min(cap, LANES)
    w = min(n, ROUTE_CHUNK)
    key, cend = pl.pallas_call(
        functools.partial(_route_select_kernel, cap=cap),
        out_shape=(jax.ShapeDtypeStruct((bsz, n_exp, n), F32),
                   jax.ShapeDtypeStruct((bsz, n_exp, LANES), jnp.int32)),
        grid=(bsz,),
        in_specs=[pl.BlockSpec((None, n_exp, n), lambda b: (b, 0, 0))],
        out_specs=(pl.BlockSpec((None, n_exp, n), lambda b: (b, 0, 0)),
                   pl.BlockSpec((None, n_exp, LANES), lambda b: (b, 0, 0))),
        compiler_params=_cparams(("parallel",)),
        name="route_select",
    )(aff_t)
    chunked = pl.BlockSpec((None, None, n // w, w), lambda b, e, cend: (b, e, 0, 0))
    out_blk = pl.BlockSpec((None, None, cap, 1), lambda b, e, cend: (b, e, 0, 0))
    grid_spec = pltpu.PrefetchScalarGridSpec(
        num_scalar_prefetch=1,
        grid=(bsz, n_exp),
        in_specs=[chunked, chunked],
        out_specs=(out_blk, out_blk),
        scratch_shapes=[pltpu.VMEM((cap // jb, jb, LANES), F32)] * 2,
    )
    return pl.pallas_call(
        functools.partial(_route_compact_kernel, cap=cap, jb=jb),
        out_shape=(jax.ShapeDtypeStruct((bsz, n_exp, cap, 1), jnp.int32),
                   jax.ShapeDtypeStruct((bsz, n_exp, cap, 1), F32)),
        grid_spec=grid_spec,
        compiler_params=_cparams(("parallel", "parallel")),
        name="route_compact",
    )(cend.reshape(-1), key.reshape(bsz, n_exp, n // w, w), aff_t.reshape(bsz, n_exp, n // w, w))


GATHER_UNROLL = 8
BF16_ROWS = 16
EXPERT_ROW_CHUNKS = 4
GATHER_PARTS = 4 * EXPERT_ROW_CHUNKS


def _tile_copy(h_hbm, xs_ref, sem, b, tok, row):
    return pltpu.make_async_copy(
        h_hbm.at[b, pl.ds(pl.multiple_of(tok * SUBLANES, SUBLANES), SUBLANES), :],
        xs_ref.at[pl.ds(pl.multiple_of(row * SUBLANES, SUBLANES), SUBLANES), :],
        sem)


def _gather_start(idx_refs, h_refs, caps, xs_ref, sem, e, b, n_exp):
    row0 = 0
    for idx_ref, h_hbm, cap in zip(idx_refs, h_refs, caps):
        base = (b * n_exp + e) * cap
        unroll = math.gcd(cap, GATHER_UNROLL)

        def issue(g, _, idx_ref=idx_ref, h_hbm=h_hbm, base=base, unroll=unroll, row0=row0):
            for u in range(unroll):
                i = g * unroll + u
                _tile_copy(h_hbm, xs_ref, sem, b, idx_ref[base + i], row0 + i).start(priority=u % 2)
            return 0

        lax.fori_loop(0, cap // unroll, issue, 0)
        row0 += cap


def _gather_inline(idx_refs, h_refs, caps, xs_ref, sem, e, b, n_exp, part):
    row0 = 0
    for idx_ref, h_hbm, cap in zip(idx_refs, h_refs, caps):
        base = (b * n_exp + e) * cap
        for k, i in enumerate(range(part, cap, GATHER_PARTS)):
            _tile_copy(h_hbm, xs_ref, sem, b, idx_ref[base + i], row0 + i).start(priority=k % 2)
        row0 += cap


def _expert_kernel(*refs, caps, d):
    n_sets = len(caps)
    idx_refs = refs[:n_sets]
    h_refs = refs[n_sets:2 * n_sets]
    wg_ref, wu_ref, wd_ref = refs[2 * n_sets:2 * n_sets + 3]
    gate_refs = refs[2 * n_sets + 3:3 * n_sets + 3]
    ys_refs = refs[3 * n_sets + 3:4 * n_sets + 3]
    xs_a, xs_b, sem = refs[4 * n_sets + 3:]
    rows = sum(caps)
    e = pl.program_id(0)
    b = pl.program_id(1)
    n_exp = pl.num_programs(0)
    n_b = pl.num_programs(1)
    total = n_exp * n_b
    step = e * n_b + b
    slot = lax.rem(step, 2)

    def wait_all(buf, s):
        pltpu.make_async_copy(h_refs[0].at[b, pl.ds(0, rows * SUBLANES), :], buf, sem.at[s]).wait()

    @pl.when(step == 0)
    def _first():
        _gather_start(idx_refs, h_refs, caps, xs_a, sem.at[0], e, b, n_exp)

    nxt = jnp.where(step + 1 < total, step + 1, 0)
    e1 = lax.div(nxt, n_b)
    b1 = lax.rem(nxt, n_b)

    def body(cur, nxt_buf, s_cur, s_nxt):
        def start_part(part):
            _gather_inline(idx_refs, h_refs, caps, nxt_buf, sem.at[s_nxt], e1, b1, n_exp, part)

        wait_all(cur, s_cur)
        wg = wg_ref[...].astype(BF16)
        wu = wu_ref[...].astype(BF16)
        wd = wd_ref[...].astype(BF16)
        bounds = [0]
        for cap in caps:
            bounds.append(bounds[-1] + cap)
        n_chunks = min(EXPERT_ROW_CHUNKS, rows // BF16_ROWS)
        groups = rows // BF16_ROWS
        edges = [BF16_ROWS * (groups * c // n_chunks) for c in range(n_chunks)] + [rows]
        for c in range(n_chunks):
            r0, r1 = edges[c], edges[c + 1]
            chunk = r1 - r0
            x = jnp.concatenate(
                [cur[pl.ds(r0 * SUBLANES + j, chunk, stride=SUBLANES), :] for j in range(d // LANES)],
                axis=1).astype(BF16)
            start_part(4 * c)
            hid = _silu(jnp.dot(x, wg, preferred_element_type=F32))
            start_part(4 * c + 1)
            hid = hid * jnp.dot(x, wu, preferred_element_type=F32)
            start_part(4 * c + 2)
            y = jnp.dot(hid.astype(BF16), wd, preferred_element_type=F32)
            start_part(4 * c + 3)
            for s in range(n_sets):
                lo, hi = max(r0, bounds[s]), min(r1, bounds[s + 1])
                if lo < hi:
                    ys = y[lo - r0:hi - r0] * gate_refs[s][lo - bounds[s]:hi - bounds[s], :]
                    for j in range(d // LANES):
                        ys_refs[s][pl.ds((lo - bounds[s]) * SUBLANES + j, hi - lo, stride=SUBLANES), :] = (
                            ys[:, j * LANES:(j + 1) * LANES])
        for part in range(4 * n_chunks, GATHER_PARTS):
            start_part(part)

        @pl.when(step == total - 1)
        def _drain():
            wait_all(nxt_buf, s_nxt)

    @pl.when(slot == 0)
    def _even():
        body(xs_a, xs_b, 0, 1)

    @pl.when(slot == 1)
    def _odd():
        body(xs_b, xs_a, 1, 0)


def _experts(token_sets, wg, wu, wd, layer):
    caps = tuple(ts[3] for ts in token_sets)
    n_sets = len(token_sets)
    bsz = token_sets[0][1].shape[0]
    _, n_exp, d, f = wg.shape
    assert token_sets[0][1].shape[1] >= sum(caps) * SUBLANES
    wspec = lambda rows, cols: pl.BlockSpec((None, None, rows, cols), lambda e, b, *idx: (layer, e, 0, 0))
    grid_spec = pltpu.PrefetchScalarGridSpec(
        num_scalar_prefetch=n_sets,
        grid=(n_exp, bsz),
        in_specs=[pl.BlockSpec(memory_space=pl.ANY)] * n_sets + [wspec(d, f), wspec(d, f), wspec(f, d)] + [
            pl.BlockSpec((None, None, cap, 1), lambda e, b, *idx: (b, e, 0, 0)) for cap in caps],
        out_specs=tuple(pl.BlockSpec((None, None, cap * SUBLANES, LANES), lambda e, b, *idx: (b, e, 0, 0))
                        for cap in caps),
        scratch_shapes=[pltpu.VMEM((sum(caps) * SUBLANES, LANES), F32)] * 2 + [pltpu.SemaphoreType.DMA((2,))],
    )
    return pl.pallas_call(
        functools.partial(_expert_kernel, caps=caps, d=d),
        out_shape=tuple(jax.ShapeDtypeStruct((bsz, n_exp, cap * SUBLANES, LANES), F32) for cap in caps),
        grid_spec=grid_spec,
        compiler_params=_cparams(("arbitrary", "arbitrary")),
        name="experts",
    )(*[ts[0] for ts in token_sets], *[ts[1] for ts in token_sets], wg, wu, wd, *[ts[2] for ts in token_sets])


FINAL_ROWS = 256


def _combine_kernel(idx_ref, ys_ref, *rest, cap, unroll, final):
    if final:
        x_hbm, g_ref, o_hbm, acc_ref, sem, xbuf, obuf, sem_in, sem_out = rest
    else:
        o_hbm, acc_ref, sem = rest
    b = pl.program_id(0)
    e = pl.program_id(1)
    n_exp = pl.num_programs(1)
    base = (b * n_exp + e) * cap

    @pl.when(e == 0)
    def _zero():
        acc_ref[...] = jnp.zeros_like(acc_ref)

    def rows(g, _):
        i0 = g * unroll
        toks = [pl.multiple_of(idx_ref[base + i0 + u] * SUBLANES, SUBLANES) for u in range(unroll)]
        sums = [acc_ref[pl.ds(toks[u], SUBLANES), :]
                + ys_ref[pl.ds(pl.multiple_of((i0 + u) * SUBLANES, SUBLANES), SUBLANES), :]
                for u in range(unroll)]
        for u in range(unroll):
            acc_ref[pl.ds(toks[u], SUBLANES), :] = sums[u]
        return 0

    lax.fori_loop(0, cap // unroll, rows, 0)

    if not final:
        @pl.when(e == n_exp - 1)
        def _flush():
            cp = pltpu.make_async_copy(acc_ref, o_hbm.at[b], sem)
            cp.start()
            cp.wait()
        return

    rows_blk, d = xbuf.shape[1:]
    n_blk = acc_ref.shape[0] // (rows_blk * SUBLANES)

    def x_in(k, slot):
        return pltpu.make_async_copy(
            x_hbm.at[b, pl.ds(pl.multiple_of(k * rows_blk, rows_blk), rows_blk), :], xbuf.at[slot], sem_in.at[slot])

    def x_out(k, slot):
        return pltpu.make_async_copy(
            obuf.at[slot], o_hbm.at[b, pl.ds(pl.multiple_of(k * rows_blk, rows_blk), rows_blk), :], sem_out.at[slot])

    @pl.when(e == n_exp - 1)
    def _final():
        x_in(0, 0).start()

        def block(k, _):
            slot = lax.rem(k, 2)

            @pl.when(k + 1 < n_blk)
            def _next():
                x_in(k + 1, 1 - slot).start()

            x_in(k, slot).wait()

            @pl.when(k >= 2)
            def _reuse():
                x_out(k - 2, slot).wait()

            moe = jnp.concatenate(
                [acc_ref[pl.ds(k * (rows_blk * SUBLANES) + j, rows_blk, stride=SUBLANES), :]
                 for j in range(d // LANES)], axis=1)
            obuf[slot] = xbuf[slot] + g_ref[...] * moe
            x_out(k, slot).start()
            return 0

        lax.fori_loop(0, n_blk, block, 0)
        for k in range(max(n_blk - 2, 0), n_blk):
            x_out(k, k % 2).wait()


def _combine(idx_flat, ys_tiles, nt, cap, final=None):
    bsz, n_exp = ys_tiles.shape[:2]
    in_specs = [pl.BlockSpec((None, None, cap * SUBLANES, LANES), lambda b, e, idx: (b, e, 0, 0))]
    scratch = [pltpu.VMEM((nt * SUBLANES, LANES), F32), pltpu.SemaphoreType.DMA(())]
    args = [idx_flat, ys_tiles]
    if final is None:
        out_shape = jax.ShapeDtypeStruct((bsz, nt * SUBLANES, LANES), F32)
    else:
        x, gate = final
        d = x.shape[-1]
        rows_blk = min(nt, FINAL_ROWS)
        in_specs += [pl.BlockSpec(memory_space=pl.ANY), pl.BlockSpec((None, 1, d), lambda b, e, idx: (b, 0, 0))]
        scratch += [pltpu.VMEM((2, rows_blk, d), F32), pltpu.VMEM((2, rows_blk, d), F32),
                    pltpu.SemaphoreType.DMA((2,)), pltpu.SemaphoreType.DMA((2,))]
        args += [x, gate]
        out_shape = jax.ShapeDtypeStruct(x.shape, F32)
    grid_spec = pltpu.PrefetchScalarGridSpec(
        num_scalar_prefetch=1,
        grid=(bsz, n_exp),
        in_specs=in_specs,
        out_specs=pl.BlockSpec(memory_space=pl.ANY),
        scratch_shapes=scratch,
    )
    return pl.pallas_call(
        functools.partial(_combine_kernel, cap=cap, unroll=4, final=final is not None),
        out_shape=out_shape,
        grid_spec=grid_spec,
        compiler_params=_cparams(("arbitrary", "arbitrary")),
        name="combine",
    )(*args)


def _prepare_layer(l, w_in, sgu_norm, w_sgu, b_sgu, q_lora_norm, w_uq, kv_lora_norm, w_ukv, q_norm, k_norm,
                   w_out, w_router, w_gate, w_up, w_down):
    w_in_l = w_in[l]
    kr_cols = w_in_l[:, OFF_KR:OFF_KR + QK_ROPE]
    w_in_ext = jnp.concatenate([w_in_l[:, :OFF_KR], kr_cols, kr_cols], axis=1).astype(BF16)
    uq = w_uq[l].reshape(Q_LORA, MLA_HEADS, QK_DIM)
    w_uq_p = jnp.concatenate(
        [uq[:, :, :QK_NOPE].reshape(Q_LORA, MLA_HEADS * QK_NOPE),
         uq[:, :, QK_NOPE:].reshape(Q_LORA, MLA_HEADS * QK_ROPE)], axis=1).astype(BF16)

    def norm_pad(v):
        return jnp.concatenate([v[:QK_NOPE], v[QK_NOPE:], v[QK_NOPE:]])[None, :]

    wr_t = w_router[l].T
    wr_hi = wr_t.astype(BF16)
    wr_lo = (wr_t - wr_hi.astype(F32)).astype(BF16)
    return dict(
        w_in=w_in_ext,
        sgu_norm=sgu_norm[l][None, :],
        w_sgu=jnp.concatenate([w_sgu[l][hh] for hh in range(SGU_HEADS)], axis=1).astype(BF16),
        b_sgu=jnp.repeat(b_sgu[l].T, SGU_HEAD_DIM, axis=1),
        fdft=jnp.asarray(_channel_dft_table()).astype(BF16),
        q_lora_norm=q_lora_norm[l][None, :],
        w_uq=w_uq_p,
        q_norm=norm_pad(q_norm[l]),
        kv_lora_norm=kv_lora_norm[l][None, :],
        w_ukv=w_ukv[l].astype(BF16),
        k_norm=norm_pad(k_norm[l]),
        w_out=w_out[l].astype(BF16),
        wr_hi=wr_hi, wr_lo=wr_lo,
        w_gate=w_gate, w_up=w_up, w_down=w_down, layer=l,
    )


def _rope_tables(n):
    pos = np.arange(n)
    half = QK_ROPE // 4
    freqs = ROPE_THETA ** (-np.arange(half, dtype=np.float64) / half)
    ang_r = (pos // GRID_WIDTH)[:, None] * freqs
    ang_c = (pos % GRID_WIDTH)[:, None] * freqs
    cos64 = np.concatenate([np.cos(ang_r), np.cos(ang_r), np.cos(ang_c), np.cos(ang_c)], axis=1)
    sin64 = np.concatenate([-np.sin(ang_r), np.sin(ang_r), -np.sin(ang_c), np.sin(ang_c)], axis=1)
    return (jnp.asarray(np.tile(cos64, (1, 2)), F32), jnp.asarray(np.tile(sin64, (1, 2)), F32))


def _token_tile(nt, pref):
    return pref if nt % pref == 0 else nt


def _moe(populations, lw, final=None):
    token_sets = []
    for h2_tiles, aff_t in populations:
        nt = aff_t.shape[-1]
        cap = EC_FACTOR * nt // N_EXPERTS
        idx, gate = _route(aff_t, cap)
        token_sets.append((idx.reshape(-1), h2_tiles, gate, cap))
    ys = _experts(token_sets, lw["w_gate"], lw["w_up"], lw["w_down"], lw["layer"])
    return [_combine(ts[0], y, pop[1].shape[-1], ts[3], final if i == 0 else None)
            for i, (ts, y, pop) in enumerate(zip(token_sets, ys, populations))]


def kernel(x, c, ctx, c_ctx, w_ada, b_ada, w_in, sgu_norm, w_sgu, b_sgu, q_lora_norm, w_uq, kv_lora_norm, w_ukv,
           q_norm, k_norm, w_out, w_router, w_gate, w_up, w_down):
    bsz, n, d = x.shape
    n_ctx = ctx.shape[1]
    depth = w_ada.shape[0]
    assert bsz + 1 <= SUBLANES

    cond = jnp.zeros((SUBLANES, d), F32).at[:bsz].set(c).at[bsz].set(c_ctx)
    mod = _ada_modulation(cond, w_ada, b_ada)
    rope_tabs = _rope_tables(n)
    tm = _token_tile(n, 1024)
    tm_in = _token_tile(n, 1024)
    tm_c = _token_tile(n_ctx, 256)

    xc = ctx
    pending = pending_c = None
    for l in range(depth):
        last = l == depth - 1
        lw = _prepare_layer(l, w_in, sgu_norm, w_sgu, b_sgu, q_lora_norm, w_uq, kv_lora_norm, w_ukv,
                            q_norm, k_norm, w_out, w_router, w_gate, w_up, w_down)
        parts = [mod[l, :, i * d:(i + 1) * d] for i in range(6)]
        lat = [p[:bsz, None, :] for p in parts]
        cx = [jnp.broadcast_to(p[bsz][None, None, :], (bsz, 1, d)) for p in parts]

        outs_c = _mix_in(xc, cx[0], cx[1], lw, None, tm_c, pending_c)
        outs = _mix_in(x, lat[0], lat[1], lw, rope_tabs, tm_in, pending)
        ya_c, a_c, b_c, q_c, k_c, v_c = outs_c[:6]
        ya, a, b, q, k, v = outs[:6]
        if pending is not None:
            xc, x = outs_c[6], outs[6]
        yb = _fourier_positions(a, b)
        tk = _token_tile(n, ATTN_TK)
        vt_c = jnp.swapaxes(v_c, 2, 3)
        vt = jnp.swapaxes(v.reshape(bsz, MLA_HEADS, n // tk, tk, V_DIM), 3, 4)
        yc = _attention(jnp.swapaxes(q, 2, 3), k_c, vt_c, k, vt, tq=_token_tile(n, ATTN_TQ), tk=tk)
        x_new, h2_tiles, aff_t = _mix_out(x, ya, yb, yc, lw["w_out"], lat[2], lat[3], lat[4],
                                          lw["wr_hi"], lw["wr_lo"], tm)
        if not last:
            yb_c = _fourier_positions(a_c, b_c)
            yc_c = _attention(jnp.swapaxes(q_c, 2, 3), k_c, vt_c, None, None,
                              tq=_token_tile(n_ctx, ATTN_TQ), tk=n_ctx)
            xc_new, h2c_tiles, affc_t = _mix_out(xc, ya_c, yb_c, yc_c, lw["w_out"], cx[2], cx[3], cx[4],
                                                 lw["wr_hi"], lw["wr_lo"], tm_c)
            moe, moe_c = _moe([(h2_tiles, aff_t), (h2c_tiles, affc_t)], lw)
            xc, pending_c = xc_new, (moe_c, cx[5])
            x, pending = x_new, (moe, lat[5])
        else:
            x, = _moe([(h2_tiles, aff_t)], lw, final=(x_new, lat[5]))
    return x
```

```python
import functools
import math

import numpy as np
import jax
import jax.numpy as jnp
from jax import lax
from jax.experimental import pallas as pl
from jax.experimental.pallas import tpu as pltpu

F32 = jnp.float32
BF16 = jnp.bfloat16

GRID_WIDTH = 64
SGU_HEADS = 4
SGU_HEAD_DIM = 64
SGU_W = SGU_HEADS * SGU_HEAD_DIM
CHUNK = 128
FNET_GROUP_DIM = 64
FNET_W = 256
MLA_HEADS = 4
QK_NOPE = 128
QK_ROPE = 64
QK_DIM = QK_NOPE + QK_ROPE
V_DIM = 128
Q_LORA = 256
KV_LORA = 128
OFF_V = 256
OFF_F = 512
OFF_Q = 768
OFF_KV = 1024
OFF_KR = 1152
N_EXPERTS = 16
EC_FACTOR = 2
ROPE_THETA = 10000.0
EPS = 1e-6

LANES = 128
SUBLANES = 8
VMEM_LIMIT_BYTES = 56 * 1024 * 1024

QK_PAD = 2 * LANES
ATTN_TQ = 4096
ATTN_TK = 512
LOG2_E = 1.4426950408889634
MIX_ROWS = 128
MIX_OUT_ROWS = 512


def _cparams(sem, vmem=None):
    return pltpu.CompilerParams(dimension_semantics=sem, vmem_limit_bytes=vmem or VMEM_LIMIT_BYTES)


def _rms_scale(v, width):
    return lax.rsqrt(jnp.sum(v * v, axis=-1, keepdims=True) * (1.0 / width) + EPS)


def _gelu_tanh(v):
    return 0.5 * v * (1.0 + jnp.tanh(0.7978845608028654 * (v + 0.044715 * v * v * v)))


def _silu(v):
    return v / (1.0 + jnp.exp(-v))


def _ada_kernel(cond_ref, w_ref, b_ref, o_ref):
    s = _silu(cond_ref[...])
    o_ref[...] = jnp.dot(s, w_ref[...], preferred_element_type=F32,
                         precision=lax.Precision.HIGHEST) + b_ref[...]


def _ada_modulation(cond, w_ada, b_ada):
    depth, d, six_d = w_ada.shape
    tn = 1536
    return pl.pallas_call(
        _ada_kernel,
        out_shape=jax.ShapeDtypeStruct((depth, SUBLANES, six_d), F32),
        grid=(depth, six_d // tn),
        in_specs=[
            pl.BlockSpec((SUBLANES, d), lambda l, j: (0, 0)),
            pl.BlockSpec((None, d, tn), lambda l, j: (l, 0, j)),
            pl.BlockSpec((None, 1, tn), lambda l, j: (l, 0, j)),
        ],
        out_specs=pl.BlockSpec((None, SUBLANES, tn), lambda l, j: (l, 0, j)),
        compiler_params=_cparams(("arbitrary", "arbitrary")),
        name="ada_modulation",
    )(cond, w_ada, b_ada.reshape(depth, 1, six_d))


def _swap16(v):
    lane = lax.broadcasted_iota(jnp.int32, v.shape, 1)
    up = pltpu.roll(v, LANES - 16, axis=1)
    dn = pltpu.roll(v, 16, axis=1)
    return jnp.where((lane % 32) < 16, up, dn)


def _mix_in_kernel(*refs, tm, use_rope, fuse_residual):
    rows_per_group = min(tm, MIX_ROWS)
    for g in range(tm // rows_per_group):
        _mix_in_rows(slice(g * rows_per_group, (g + 1) * rows_per_group), refs, use_rope, fuse_residual)


def _mix_in_rows(rg, refs, use_rope, fuse_residual):
    if fuse_residual:
        x_ref, moe_ref, g2_ref = refs[:3]
        xo_ref = refs[-1]
        refs = refs[:1] + refs[3:-1]
    (x_ref, sh_ref, sc_ref, win_ref, sgun_ref, wsgu_ref, bsgu_ref, fdft_ref, qln_ref, wuq_ref, qn_ref,
     kvn_ref, wukv_ref, kn_ref, cos_ref, sin_ref, ya_ref, a_ref, b_ref, q_ref, k_ref, v_ref) = refs
    x = x_ref[rg, :]
    if fuse_residual:
        nrows = rg.stop - rg.start
        moe = jnp.concatenate(
            [moe_ref[pl.ds(rg.start * SUBLANES + j, nrows, stride=SUBLANES), :]
             for j in range(x.shape[-1] // LANES)], axis=1)
        x = x + g2_ref[...] * moe
        xo_ref[rg, :] = x
    tm = x.shape[0]
    d = x.shape[-1]
    h = x * _rms_scale(x, d) * (1.0 + sc_ref[...]) + sh_ref[...]
    p = jnp.dot(h.astype(BF16), win_ref[...], preferred_element_type=F32)

    u = _gelu_tanh(p[:, 0:OFF_V])
    gv = _gelu_tanh(p[:, OFF_V:OFF_F])
    vn = gv * _rms_scale(gv, SGU_W) * sgun_ref[...]
    head_of_lane = lax.broadcasted_iota(jnp.int32, (CHUNK, SGU_W), 1) // SGU_HEAD_DIM
    for c in range(tm // CHUNK):
        rows = slice(c * CHUNK, (c + 1) * CHUNK)
        vc = vn[rows]
        vstack = jnp.concatenate(
            [jnp.where(head_of_lane == hh, vc, 0.0) for hh in range(SGU_HEADS)], axis=0).astype(BF16)
        z = jnp.dot(wsgu_ref[...], vstack, preferred_element_type=F32) + bsgu_ref[...]
        ya_ref[rg.start + c * CHUNK:rg.start + (c + 1) * CHUNK, :] = (u[rows] * z).astype(ya_ref.dtype)

    ab = jnp.dot(p[:, OFF_F:OFF_Q].astype(BF16), fdft_ref[...], preferred_element_type=F32)
    a_ref[rg, :] = ab[:, 0:FNET_W].astype(a_ref.dtype)
    b_ref[rg, :] = ab[:, FNET_W:2 * FNET_W].astype(b_ref.dtype)

    lane = lax.broadcasted_iota(jnp.int32, (1, LANES), 1)
    lo = lane < QK_ROPE
    if use_rope:
        cos = cos_ref[rg, :]
        sin = sin_ref[rg, :]

    def rope(t):
        return t * cos + _swap16(t) * sin if use_rope else t

    pq = p[:, OFF_Q:OFF_KV]
    cq = pq * _rms_scale(pq, Q_LORA) * qln_ref[...]
    q = jnp.dot(cq.astype(BF16), wuq_ref[...], preferred_element_type=F32)
    qnorm = qn_ref[...]
    for hh in range(MLA_HEADS):
        qn = q[:, LANES * hh:LANES * (hh + 1)]
        pair = hh // 2
        qr = q[:, MLA_HEADS * LANES + LANES * pair:MLA_HEADS * LANES + LANES * (pair + 1)]
        qr = jnp.where(lo if hh % 2 == 0 else jnp.logical_not(lo), qr, 0.0)
        ssq = jnp.sum(qn * qn, axis=-1, keepdims=True) + jnp.sum(qr * qr, axis=-1, keepdims=True)
        rinv = lax.rsqrt(ssq * (1.0 / QK_DIM) + EPS) * (QK_DIM ** -0.5 * LOG2_E)
        q_ref[hh, rg, 0:LANES] = (qn * rinv * qnorm[:, 0:LANES]).astype(q_ref.dtype)
        q_ref[hh, rg, LANES:QK_PAD] = rope(qr * rinv * qnorm[:, LANES:QK_PAD]).astype(q_ref.dtype)

    pkv = p[:, OFF_KV:OFF_KR]
    ckv = pkv * _rms_scale(pkv, KV_LORA) * kvn_ref[...]
    kv = jnp.dot(ckv.astype(BF16), wukv_ref[...], preferred_element_type=F32)
    krr = p[:, OFF_KR:OFF_KR + LANES]
    ssq_kr = 0.5 * jnp.sum(krr * krr, axis=-1, keepdims=True)
    knorm = kn_ref[...]
    rinvs = []
    for hh in range(MLA_HEADS):
        kn = kv[:, 2 * LANES * hh:2 * LANES * hh + LANES]
        ssq = jnp.sum(kn * kn, axis=-1, keepdims=True) + ssq_kr
        rinv = lax.rsqrt(ssq * (1.0 / QK_DIM) + EPS)
        rinvs.append(rinv)
        k_ref[hh, rg, 0:LANES] = (kn * rinv * knorm[:, 0:LANES]).astype(k_ref.dtype)
        v_ref[hh, rg, :] = kv[:, 2 * LANES * hh + LANES:2 * LANES * (hh + 1)].astype(v_ref.dtype)
    for pair in range(MLA_HEADS // 2):
        r2 = jnp.where(lo, rinvs[2 * pair], rinvs[2 * pair + 1])
        kr = rope(krr * r2 * knorm[:, LANES:QK_PAD]).astype(k_ref.dtype)
        k_ref[2 * pair, rg, LANES:QK_PAD] = kr
        k_ref[2 * pair + 1, rg, LANES:QK_PAD] = kr


def _mix_in(x, sh, sc, lw, rope_tabs, tm, pending=None):
    bsz, nt, d = x.shape
    use_rope = rope_tabs is not None
    if use_rope:
        cos_t, sin_t = rope_tabs
    else:
        cos_t = sin_t = jnp.zeros((nt, LANES), F32)
    full = lambda arr: pl.BlockSpec(arr.shape, lambda b, i: (0,) * arr.ndim)
    vec = pl.BlockSpec((None, 1, d), lambda b, i: (b, 0, 0))
    in_w = lw["w_in"].shape[1]
    fuse = pending is not None
    kern = functools.partial(_mix_in_kernel, tm=tm, use_rope=use_rope, fuse_residual=fuse)
    row_blk = pl.BlockSpec((None, tm, d), lambda b, i: (b, i, 0))
    out_shape = [
        jax.ShapeDtypeStruct((bsz, nt, SGU_W), BF16),
        jax.ShapeDtypeStruct((bsz, nt, FNET_W), F32),
        jax.ShapeDtypeStruct((bsz, nt, FNET_W), F32),
        jax.ShapeDtypeStruct((bsz, MLA_HEADS, nt, QK_PAD), BF16),
        jax.ShapeDtypeStruct((bsz, MLA_HEADS, nt, QK_PAD), BF16),
        jax.ShapeDtypeStruct((bsz, MLA_HEADS, nt, V_DIM), BF16),
    ]
    out_specs = [
        pl.BlockSpec((None, tm, SGU_W), lambda b, i: (b, i, 0)),
        pl.BlockSpec((None, tm, FNET_W), lambda b, i: (b, i, 0)),
        pl.BlockSpec((None, tm, FNET_W), lambda b, i: (b, i, 0)),
        pl.BlockSpec((None, MLA_HEADS, tm, QK_PAD), lambda b, i: (b, 0, i, 0)),
        pl.BlockSpec((None, MLA_HEADS, tm, QK_PAD), lambda b, i: (b, 0, i, 0)),
        pl.BlockSpec((None, MLA_HEADS, tm, V_DIM), lambda b, i: (b, 0, i, 0)),
    ]
    in_specs = [row_blk]
    args = [x]
    if fuse:
        in_specs += [pl.BlockSpec((None, tm * SUBLANES, LANES), lambda b, i: (b, i, 0)), vec]
        args += list(pending)
        out_shape.append(jax.ShapeDtypeStruct((bsz, nt, d), F32))
        out_specs.append(row_blk)
    in_specs += [
        vec, vec,
        full(lw["w_in"]), full(lw["sgu_norm"]), full(lw["w_sgu"]), full(lw["b_sgu"]), full(lw["fdft"]),
        full(lw["q_lora_norm"]), full(lw["w_uq"]), full(lw["q_norm"]),
        full(lw["kv_lora_norm"]), full(lw["w_ukv"]), full(lw["k_norm"]),
        pl.BlockSpec((tm, LANES), lambda b, i: (i, 0)),
        pl.BlockSpec((tm, LANES), lambda b, i: (i, 0)),
    ]
    args += [sh, sc, lw["w_in"], lw["sgu_norm"], lw["w_sgu"], lw["b_sgu"], lw["fdft"],
             lw["q_lora_norm"], lw["w_uq"], lw["q_norm"], lw["kv_lora_norm"], lw["w_ukv"], lw["k_norm"],
             cos_t, sin_t]
    return pl.pallas_call(
        kern,
        out_shape=tuple(out_shape),
        grid=(bsz, nt // tm),
        in_specs=in_specs,
        out_specs=tuple(out_specs),
        compiler_params=_cparams(("parallel", "parallel")),
        name="mix_in",
    )(*args)


DFT_UNROLL = 16


def _dft_kernel(a_ref, b_ref, w_ref, m_ref, o_ref, yr_ref, yi_ref, *, n1, n2):
    w1 = w_ref[...]
    c = a_ref.shape[-1]

    u1 = math.gcd(n2, DFT_UNROLL)
    u2 = math.gcd(n1, DFT_UNROLL)

    def step1(g, _):
        cols = []
        for u in range(u1):
            rows = pl.ds(g * u1 + u, n1, stride=n2)
            cols.append(jnp.concatenate([a_ref[rows, :], b_ref[rows, :]], axis=0))
        r = jnp.dot(w1, jnp.concatenate(cols, axis=1).astype(BF16), preferred_element_type=F32)
        for u in range(u1):
            out_rows = pl.ds(pl.multiple_of((g * u1 + u) * n1, n1), n1)
            yr_ref[out_rows, :] = r[0:n1, u * c:(u + 1) * c]
            yi_ref[out_rows, :] = r[n1:2 * n1, u * c:(u + 1) * c]
        return 0

    lax.fori_loop(0, n2 // u1, step1, 0)

    def step2(g, _):
        for u in range(u2):
            k1 = g * u2 + u
            rows = pl.ds(k1, n2, stride=n1)
            ycat = jnp.concatenate([yr_ref[rows, :], yi_ref[rows, :]], axis=0).astype(BF16)
            o_ref[rows, :] = jnp.dot(m_ref[k1], ycat, preferred_element_type=F32)
        return 0

    lax.fori_loop(0, n1 // u2, step2, 0)


def _dft_factors(n):
    n1 = 64 if n >= 1024 else 32
    return n1, n // n1


@functools.lru_cache(maxsize=None)
def _dft_tables(n):
    n1, n2 = _dft_factors(n)
    j = np.arange(n1, dtype=np.float64)
    ang1 = 2.0 * np.pi * np.outer(j, j) / n1
    c1 = np.cos(ang1) / math.sqrt(n1)
    s1 = np.sin(ang1) / math.sqrt(n1)
    w1 = np.block([[c1, -s1], [s1, c1]])
    k1 = np.arange(n1, dtype=np.float64)[:, None, None]
    k2 = np.arange(n2, dtype=np.float64)[None, :, None]
    m2 = np.arange(n2, dtype=np.float64)[None, None, :]
    ang2 = 2.0 * np.pi * m2 * (k1 + n1 * k2) / n
    m2cat = np.concatenate([np.cos(ang2), -np.sin(ang2)], axis=-1) / math.sqrt(n2)
    return w1.astype(np.float32), m2cat.astype(np.float32)


@functools.lru_cache(maxsize=None)
def _channel_dft_table():
    j = np.arange(FNET_GROUP_DIM, dtype=np.float64)
    ang = 2.0 * np.pi * np.outer(j, j) / FNET_GROUP_DIM
    groups = FNET_W // FNET_GROUP_DIM
    cg = np.kron(np.eye(groups), np.cos(ang)) / math.sqrt(FNET_GROUP_DIM)
    sg = np.kron(np.eye(groups), np.sin(ang)) / math.sqrt(FNET_GROUP_DIM)
    return np.concatenate([cg, sg], axis=1).astype(np.float32)


def _fourier_positions(a, b):
    bsz, n, c = a.shape
    n1, n2 = _dft_factors(n)
    w1_np, m2_np = _dft_tables(n)
    w1 = jnp.asarray(w1_np).astype(BF16)
    m2cat = jnp.asarray(m2_np).astype(BF16)
    cb = LANES
    blk = pl.BlockSpec((None, n, cb), lambda bi, j: (bi, 0, j))
    return pl.pallas_call(
        functools.partial(_dft_kernel, n1=n1, n2=n2),
        out_shape=jax.ShapeDtypeStruct((bsz, n, c), F32),
        grid=(bsz, c // cb),
        in_specs=[blk, blk,
                  pl.BlockSpec((2 * n1, 2 * n1), lambda bi, j: (0, 0)),
                  pl.BlockSpec((n1, n2, 2 * n2), lambda bi, j: (0, 0, 0))],
        out_specs=blk,
        scratch_shapes=[pltpu.VMEM((n, cb), F32)] * 2,
        compiler_params=_cparams(("parallel", "parallel")),
        name="fourier",
    )(a, b, w1, m2cat)


NEG_BIG = -1e30


def _attn_kernel(*refs, tk, n_lat_blocks):
    if n_lat_blocks:
        qt_ref, kc_ref, vtc_ref, kl_ref, vtl_ref, o_ref, acc_ref, sa_ref, sb_ref = refs
    else:
        qt_ref, kc_ref, vtc_ref, o_ref, acc_ref = refs
    qt = qt_ref[...]
    tq = qt.shape[1]

    def scores(kj):
        return jnp.dot(kj, qt, preferred_element_type=F32)

    def consume(s, vtj, m, l, first):
        m_new = jnp.maximum(m, jnp.max(s, axis=0, keepdims=True))
        alpha = jnp.exp2(m - m_new)
        p = jnp.exp2(s - m_new)
        l_new = alpha * l + jnp.sum(p, axis=0, keepdims=True)
        pv = jnp.dot(vtj, p.astype(BF16), preferred_element_type=F32)
        acc_ref[...] = pv if first else alpha * acc_ref[...] + pv
        return m_new, l_new

    def lat_keys(j):
        return kl_ref[pl.ds(pl.multiple_of(j * tk, tk), tk), :]

    m0 = jnp.full((1, tq), NEG_BIG, F32)
    l0 = jnp.zeros((1, tq), F32)
    if n_lat_blocks:
        sa_ref[...] = scores(lat_keys(0))
    m, l = consume(scores(kc_ref[...]), vtc_ref[...], m0, l0, True)
    if n_lat_blocks:
        def body(i, carry):
            sb_ref[...] = scores(lat_keys(2 * i + 1))
            m, l = consume(sa_ref[...], vtl_ref[2 * i], carry[0], carry[1], False)
            sa_ref[...] = scores(lat_keys(2 * i + 2))
            return consume(sb_ref[...], vtl_ref[2 * i + 1], m, l, False)

        m, l = lax.fori_loop(0, n_lat_blocks // 2 - 1, body, (m, l))
        sb_ref[...] = scores(lat_keys(n_lat_blocks - 1))
        m, l = consume(sa_ref[...], vtl_ref[n_lat_blocks - 2], m, l, False)
        m, l = consume(sb_ref[...], vtl_ref[n_lat_blocks - 1], m, l, False)
    o_ref[...] = jnp.transpose(acc_ref[...] / l).astype(o_ref.dtype)


def _attention(qt, k_ctx, vt_ctx, k_lat, vt_lat, tq, tk):
    bsz, heads, _, nq = qt.shape
    m_ctx = k_ctx.shape[2]
    n_lat_blocks = 0 if k_lat is None else k_lat.shape[2] // tk
    in_specs = [
        pl.BlockSpec((None, None, QK_PAD, tq), lambda b, h, i: (b, h, 0, i)),
        pl.BlockSpec((None, None, m_ctx, QK_PAD), lambda b, h, i: (b, h, 0, 0)),
        pl.BlockSpec((None, None, V_DIM, m_ctx), lambda b, h, i: (b, h, 0, 0)),
    ]
    args = [qt, k_ctx, vt_ctx]
    if n_lat_blocks:
        in_specs += [
            pl.BlockSpec((None, None, k_lat.shape[2], QK_PAD), lambda b, h, i: (b, h, 0, 0)),
            pl.BlockSpec((None, None, n_lat_blocks, V_DIM, tk), lambda b, h, i: (b, h, 0, 0, 0)),
        ]
        args += [k_lat, vt_lat]
    return pl.pallas_call(
        functools.partial(_attn_kernel, tk=tk, n_lat_blocks=n_lat_blocks),
        out_shape=jax.ShapeDtypeStruct((bsz, nq, heads * V_DIM), BF16),
        grid=(bsz, heads, nq // tq),
        in_specs=in_specs,
        out_specs=pl.BlockSpec((None, tq, V_DIM), lambda b, h, i: (b, i, h)),
        scratch_shapes=[pltpu.VMEM((V_DIM, tq), F32)] + [pltpu.VMEM((tk, tq), F32)] * (2 if n_lat_blocks else 0),
        compiler_params=_cparams(("parallel", "parallel", "arbitrary")),
        name="attention",
    )(*args)


def _mix_out_kernel(x_ref, ya_ref, yb_ref, yc_ref, wo_ref, g1_ref, sh_ref, sc_ref, wrh_ref, wrl_ref,
                    xn_ref, h2_ref, aff_ref, *, tm):
    rows_per_group = min(tm, MIX_OUT_ROWS)
    for g in range(tm // rows_per_group):
        _mix_out_rows(g * rows_per_group, rows_per_group, x_ref, ya_ref, yb_ref, yc_ref, wo_ref, g1_ref, sh_ref,
                      sc_ref, wrh_ref, wrl_ref, xn_ref, h2_ref, aff_ref)


def _mix_out_rows(r0, nr, x_ref, ya_ref, yb_ref, yc_ref, wo_ref, g1_ref, sh_ref, sc_ref, wrh_ref, wrl_ref,
                  xn_ref, h2_ref, aff_ref):
    rg = slice(r0, r0 + nr)
    mix = jnp.dot(ya_ref[rg, :], wo_ref[0:SGU_W, :], preferred_element_type=F32)
    mix += jnp.dot(yb_ref[rg, :].astype(BF16), wo_ref[SGU_W:SGU_W + FNET_W, :], preferred_element_type=F32)
    mix += jnp.dot(yc_ref[rg, :], wo_ref[SGU_W + FNET_W:, :], preferred_element_type=F32)
    xn = x_ref[rg, :] + g1_ref[...] * mix
    xn_ref[rg, :] = xn
    d = xn.shape[-1]
    h2 = xn * _rms_scale(xn, d) * (1.0 + sc_ref[...]) + sh_ref[...]
    for j in range(d // LANES):
        h2_ref[pl.ds(r0 * SUBLANES + j, nr, stride=SUBLANES), :] = h2[:, j * LANES:(j + 1) * LANES]
    h_hi = h2.astype(BF16)
    h_lo = (h2 - h_hi.astype(F32)).astype(BF16)
    nt_dims = (((1,), (1,)), ((), ()))
    logits = lax.dot_general(wrh_ref[...], h_hi, nt_dims, preferred_element_type=F32)
    logits += lax.dot_general(wrh_ref[...], h_lo, nt_dims, preferred_element_type=F32)
    logits += lax.dot_general(wrl_ref[...], h_hi, nt_dims, preferred_element_type=F32)
    e = jnp.exp(logits - jnp.max(logits, axis=0, keepdims=True))
    aff_ref[:, rg] = e / jnp.sum(e, axis=0, keepdims=True)


def _mix_out(x, ya, yb, yc, w_out, g1, sh2, sc2, wr_hi, wr_lo, tm):
    bsz, nt, d = x.shape
    vec = pl.BlockSpec((None, 1, d), lambda b, i: (b, 0, 0))
    full = lambda arr: pl.BlockSpec(arr.shape, lambda b, i: (0,) * arr.ndim)
    return pl.pallas_call(
        functools.partial(_mix_out_kernel, tm=tm),
        out_shape=(
            jax.ShapeDtypeStruct((bsz, nt, d), F32),
            jax.ShapeDtypeStruct((bsz, nt * SUBLANES, LANES), F32),
            jax.ShapeDtypeStruct((bsz, N_EXPERTS, nt), F32),
        ),
        grid=(bsz, nt // tm),
        in_specs=[
            pl.BlockSpec((None, tm, d), lambda b, i: (b, i, 0)),
            pl.BlockSpec((None, tm, SGU_W), lambda b, i: (b, i, 0)),
            pl.BlockSpec((None, tm, FNET_W), lambda b, i: (b, i, 0)),
            pl.BlockSpec((None, tm, MLA_HEADS * V_DIM), lambda b, i: (b, i, 0)),
            full(w_out), vec, vec, vec, full(wr_hi), full(wr_lo),
        ],
        out_specs=(
            pl.BlockSpec((None, tm, d), lambda b, i: (b, i, 0)),
            pl.BlockSpec((None, tm * SUBLANES, LANES), lambda b, i: (b, i, 0)),
            pl.BlockSpec((None, N_EXPERTS, tm), lambda b, i: (b, 0, i)),
        ),
        compiler_params=_cparams(("parallel", "parallel")),
        name="mix_out",
    )(x, ya, yb, yc, w_out, g1, sh2, sc2, wr_hi, wr_lo)


def _prefix_count(mask, tri):
    e, n = mask.shape
    w = tri.shape[0]
    carry = jnp.zeros((e, 1), F32)
    outs = []
    ends = []
    for c in range(n // w):
        local = jnp.dot(mask[:, c * w:(c + 1) * w].astype(BF16), tri, preferred_element_type=F32)
        outs.append(local + carry)
        carry = carry + local[:, w - 1:w]
        ends.append(carry)
    return jnp.concatenate(outs, axis=1), ends


ROUTE_CHUNK = 4 * LANES


def _route_select_kernel(aff_ref, key_ref, cend_ref, *, cap):
    n_exp, n = aff_ref.shape
    w = min(n, ROUTE_CHUNK)
    aff = aff_ref[...]

    def step(i, thr_bits):
        cand = thr_bits | jnp.left_shift(jnp.int32(1), 30 - i)
        cnt = jnp.sum((aff >= pltpu.bitcast(cand, F32)).astype(jnp.int32), axis=-1, keepdims=True)
        return jnp.where(cnt >= cap, cand, thr_bits)

    thr = pltpu.bitcast(lax.fori_loop(0, 31, step, jnp.zeros((n_exp, 1), jnp.int32)), F32)
    gt = aff > thr
    eq = aff == thr
    need = cap - jnp.sum(gt.astype(jnp.int32), axis=-1, keepdims=True)
    r = lax.broadcasted_iota(jnp.int32, (w, w), 0)
    c = lax.broadcasted_iota(jnp.int32, (w, w), 1)
    tri = (r <= c).astype(BF16)
    eq_rank, _ = _prefix_count(eq.astype(F32), tri)
    sel = jnp.logical_or(gt, jnp.logical_and(eq, eq_rank <= need.astype(F32)))
    cum, ends = _prefix_count(sel.astype(F32), tri)
    key_ref[...] = jnp.where(sel, cum, 0.0)
    lane = lax.broadcasted_iota(jnp.int32, (n_exp, LANES), 1)
    cend = jnp.zeros((n_exp, LANES), F32)
    for ci, end in enumerate(ends):
        cend = jnp.where(lane == ci, end, cend)
    cend_ref[...] = cend.astype(jnp.int32)


def _route_compact_kernel(cend_ref, key_ref, aff_ref, idx_ref, gate_ref, acc_i_ref, acc_g_ref, *, cap, jb):
    b = pl.program_id(0)
    e = pl.program_id(1)
    n_chunks, w = key_ref.shape
    n_sb = cap // jb
    base = (b * pl.num_programs(1) + e) * LANES
    shift = jb.bit_length() - 1
    acc_i_ref[...] = jnp.zeros_like(acc_i_ref)
    acc_g_ref[...] = jnp.zeros_like(acc_g_ref)

    def chunk(c, _):
        c_start = jnp.where(c > 0, cend_ref[base + jnp.maximum(c - 1, 0)], 0)
        c_end = cend_ref[base + c]

        @pl.when(c_end > c_start)
        def _any_selected():
            krow = key_ref[pl.ds(c, 1), :]
            arow = aff_ref[pl.ds(c, 1), :]
            tpos = (lax.broadcasted_iota(jnp.int32, (1, w), 1) + c * w).astype(F32)
            sb_lo = lax.shift_right_logical(c_start, shift)
            sb_hi = jnp.minimum(lax.shift_right_logical(c_end - 1, shift), n_sb - 1)

            def slot_block(sb, _):
                slot = (lax.broadcasted_iota(jnp.int32, (jb, 1), 0) + (sb * jb + 1)).astype(F32)
                hit = krow == slot
                pi = jnp.where(hit, tpos, 0.0)
                pg = jnp.where(hit, arow, 0.0)
                fi = pi[:, 0:LANES]
                fg = pg[:, 0:LANES]
                for k in range(1, w // LANES):
                    fi = fi + pi[:, k * LANES:(k + 1) * LANES]
                    fg = fg + pg[:, k * LANES:(k + 1) * LANES]
                acc_i_ref[sb] = acc_i_ref[sb] + fi
                acc_g_ref[sb] = acc_g_ref[sb] + fg
                return 0

            lax.fori_loop(sb_lo, sb_hi + 1, slot_block, 0)

        return 0

    lax.fori_loop(0, n_chunks, chunk, 0)
    for sb in range(n_sb):
        idx_ref[sb * jb:(sb + 1) * jb, :] = jnp.sum(acc_i_ref[sb], axis=-1, keepdims=True).astype(jnp.int32)
        gate_ref[sb * jb:(sb + 1) * jb, :] = jnp.sum(acc_g_ref[sb], axis=-1, keepdims=True)


def _route(aff_t, cap):
    bsz, n_exp, n = aff_t.shape
    jb = min(cap, LANES)
    w = min(n, ROUTE_CHUNK)
    key, cend = pl.pallas_call(
        functools.partial(_route_select_kernel, cap=cap),
        out_shape=(jax.ShapeDtypeStruct((bsz, n_exp, n), F32),
                   jax.ShapeDtypeStruct((bsz, n_exp, LANES), jnp.int32)),
        grid=(bsz,),
        in_specs=[pl.BlockSpec((None, n_exp, n), lambda b: (b, 0, 0))],
        out_specs=(pl.BlockSpec((None, n_exp, n), lambda b: (b, 0, 0)),
                   pl.BlockSpec((None, n_exp, LANES), lambda b: (b, 0, 0))),
        compiler_params=_cparams(("parallel",)),
        name="route_select",
    )(aff_t)
    chunked = pl.BlockSpec((None, None, n // w, w), lambda b, e, cend: (b, e, 0, 0))
    out_blk = pl.BlockSpec((None, None, cap, 1), lambda b, e, cend: (b, e, 0, 0))
    grid_spec = pltpu.PrefetchScalarGridSpec(
        num_scalar_prefetch=1,
        grid=(bsz, n_exp),
        in_specs=[chunked, chunked],
        out_specs=(out_blk, out_blk),
        scratch_shapes=[pltpu.VMEM((cap // jb, jb, LANES), F32)] * 2,
    )
    return pl.pallas_call(
        functools.partial(_route_compact_kernel, cap=cap, jb=jb),
        out_shape=(jax.ShapeDtypeStruct((bsz, n_exp, cap, 1), jnp.int32),
                   jax.ShapeDtypeStruct((bsz, n_exp, cap, 1), F32)),
        grid_spec=grid_spec,
        compiler_params=_cparams(("parallel", "parallel")),
        name="route_compact",
    )(cend.reshape(-1), key.reshape(bsz, n_exp, n // w, w), aff_t.reshape(bsz, n_exp, n // w, w))


GATHER_UNROLL = 8
BF16_ROWS = 16
EXPERT_ROW_CHUNKS = 4
GATHER_PARTS = 4 * EXPERT_ROW_CHUNKS


def _tile_copy(h_hbm, xs_ref, sem, b, tok, row):
    return pltpu.make_async_copy(
        h_hbm.at[b, pl.ds(pl.multiple_of(tok * SUBLANES, SUBLANES), SUBLANES), :],
        xs_ref.at[pl.ds(pl.multiple_of(row * SUBLANES, SUBLANES), SUBLANES), :],
        sem)


def _gather_start(idx_refs, h_refs, caps, xs_ref, sem, e, b, n_exp):
    row0 = 0
    for idx_ref, h_hbm, cap in zip(idx_refs, h_refs, caps):
        base = (b * n_exp + e) * cap
        unroll = math.gcd(cap, GATHER_UNROLL)

        def issue(g, _, idx_ref=idx_ref, h_hbm=h_hbm, base=base, unroll=unroll, row0=row0):
            for u in range(unroll):
                i = g * unroll + u
                _tile_copy(h_hbm, xs_ref, sem, b, idx_ref[base + i], row0 + i).start(priority=u % 2)
            return 0

        lax.fori_loop(0, cap // unroll, issue, 0)
        row0 += cap


def _gather_inline(idx_refs, h_refs, caps, xs_ref, sem, e, b, n_exp, part):
    row0 = 0
    for idx_ref, h_hbm, cap in zip(idx_refs, h_refs, caps):
        base = (b * n_exp + e) * cap
        for k, i in enumerate(range(part, cap, GATHER_PARTS)):
            _tile_copy(h_hbm, xs_ref, sem, b, idx_ref[base + i], row0 + i).start(priority=k % 2)
        row0 += cap


def _expert_kernel(*refs, caps, d):
    n_sets = len(caps)
    idx_refs = refs[:n_sets]
    h_refs = refs[n_sets:2 * n_sets]
    wg_ref, wu_ref, wd_ref = refs[2 * n_sets:2 * n_sets + 3]
    gate_refs = refs[2 * n_sets + 3:3 * n_sets + 3]
    ys_refs = refs[3 * n_sets + 3:4 * n_sets + 3]
    xs_a, xs_b, sem = refs[4 * n_sets + 3:]
    rows = sum(caps)
    e = pl.program_id(0)
    b = pl.program_id(1)
    n_exp = pl.num_programs(0)
    n_b = pl.num_programs(1)
    total = n_exp * n_b
    step = e * n_b + b
    slot = lax.rem(step, 2)

    def wait_all(buf, s):
        pltpu.make_async_copy(h_refs[0].at[b, pl.ds(0, rows * SUBLANES), :], buf, sem.at[s]).wait()

    @pl.when(step == 0)
    def _first():
        _gather_start(idx_refs, h_refs, caps, xs_a, sem.at[0], e, b, n_exp)

    nxt = jnp.where(step + 1 < total, step + 1, 0)
    e1 = lax.div(nxt, n_b)
    b1 = lax.rem(nxt, n_b)

    def body(cur, nxt_buf, s_cur, s_nxt):
        def start_part(part):
            _gather_inline(idx_refs, h_refs, caps, nxt_buf, sem.at[s_nxt], e1, b1, n_exp, part)

        wait_all(cur, s_cur)
        wg = wg_ref[...].astype(BF16)
        wu = wu_ref[...].astype(BF16)
        wd = wd_ref[...].astype(BF16)
        bounds = [0]
        for cap in caps:
            bounds.append(bounds[-1] + cap)
        n_chunks = min(EXPERT_ROW_CHUNKS, rows // BF16_ROWS)
        groups = rows // BF16_ROWS
        edges = [BF16_ROWS * (groups * c // n_chunks) for c in range(n_chunks)] + [rows]
        for c in range(n_chunks):
            r0, r1 = edges[c], edges[c + 1]
            chunk = r1 - r0
            x = jnp.concatenate(
                [cur[pl.ds(r0 * SUBLANES + j, chunk, stride=SUBLANES), :] for j in range(d // LANES)],
                axis=1).astype(BF16)
            start_part(4 * c)
            hid = _silu(jnp.dot(x, wg, preferred_element_type=F32))
            start_part(4 * c + 1)
            hid = hid * jnp.dot(x, wu, preferred_element_type=F32)
            start_part(4 * c + 2)
            y = jnp.dot(hid.astype(BF16), wd, preferred_element_type=F32)
            start_part(4 * c + 3)
            for s in range(n_sets):
                lo, hi = max(r0, bounds[s]), min(r1, bounds[s + 1])
                if lo < hi:
                    ys = y[lo - r0:hi - r0] * gate_refs[s][lo - bounds[s]:hi - bounds[s], :]
                    for j in range(d // LANES):
                        ys_refs[s][pl.ds((lo - bounds[s]) * SUBLANES + j, hi - lo, stride=SUBLANES), :] = (
                            ys[:, j * LANES:(j + 1) * LANES])
        for part in range(4 * n_chunks, GATHER_PARTS):
            start_part(part)

        @pl.when(step == total - 1)
        def _drain():
            wait_all(nxt_buf, s_nxt)

    @pl.when(slot == 0)
    def _even():
        body(xs_a, xs_b, 0, 1)

    @pl.when(slot == 1)
    def _odd():
        body(xs_b, xs_a, 1, 0)


def _experts(token_sets, wg, wu, wd, layer):
    caps = tuple(ts[3] for ts in token_sets)
    n_sets = len(token_sets)
    bsz = token_sets[0][1].shape[0]
    _, n_exp, d, f = wg.shape
    assert token_sets[0][1].shape[1] >= sum(caps) * SUBLANES
    wspec = lambda rows, cols: pl.BlockSpec((None, None, rows, cols), lambda e, b, *idx: (layer, e, 0, 0))
    grid_spec = pltpu.PrefetchScalarGridSpec(
        num_scalar_prefetch=n_sets,
        grid=(n_exp, bsz),
        in_specs=[pl.BlockSpec(memory_space=pl.ANY)] * n_sets + [wspec(d, f), wspec(d, f), wspec(f, d)] + [
            pl.BlockSpec((None, None, cap, 1), lambda e, b, *idx: (b, e, 0, 0)) for cap in caps],
        out_specs=tuple(pl.BlockSpec((None, None, cap * SUBLANES, LANES), lambda e, b, *idx: (b, e, 0, 0))
                        for cap in caps),
        scratch_shapes=[pltpu.VMEM((sum(caps) * SUBLANES, LANES), F32)] * 2 + [pltpu.SemaphoreType.DMA((2,))],
    )
    return pl.pallas_call(
        functools.partial(_expert_kernel, caps=caps, d=d),
        out_shape=tuple(jax.ShapeDtypeStruct((bsz, n_exp, cap * SUBLANES, LANES), F32) for cap in caps),
        grid_spec=grid_spec,
        compiler_params=_cparams(("arbitrary", "arbitrary")),
        name="experts",
    )(*[ts[0] for ts in token_sets], *[ts[1] for ts in token_sets], wg, wu, wd, *[ts[2] for ts in token_sets])


FINAL_ROWS = 256
COMBINE_BLOCK_BYTES = 4 * 1024 * 1024


def _combine_kernel(idx_ref, ys_ref, *rest, cap, unroll, final):
    if final:
        x_hbm, g_ref, o_hbm, acc_ref, sem, xbuf, obuf, sem_in, sem_out = rest
    else:
        o_hbm, acc_ref, sem = rest
    b = pl.program_id(0)
    e = pl.program_id(1)
    n_exp = pl.num_programs(1)
    group = ys_ref.shape[0]

    @pl.when(e == 0)
    def _zero():
        acc_ref[...] = jnp.zeros_like(acc_ref)

    def expert(el, _):
        base = ((b * n_exp + e) * group + el) * cap

        def rows(g, _):
            i0 = g * unroll
            toks = [pl.multiple_of(idx_ref[base + i0 + u] * SUBLANES, SUBLANES) for u in range(unroll)]
            sums = [acc_ref[pl.ds(toks[u], SUBLANES), :]
                    + ys_ref[el, pl.ds(pl.multiple_of((i0 + u) * SUBLANES, SUBLANES), SUBLANES), :]
                    for u in range(unroll)]
            for u in range(unroll):
                acc_ref[pl.ds(toks[u], SUBLANES), :] = sums[u]
            return 0

        lax.fori_loop(0, cap // unroll, rows, 0)
        return 0

    lax.fori_loop(0, group, expert, 0)

    if not final:
        @pl.when(e == n_exp - 1)
        def _flush():
            cp = pltpu.make_async_copy(acc_ref, o_hbm.at[b], sem)
            cp.start()
            cp.wait()
        return

    rows_blk, d = xbuf.shape[1:]
    n_blk = acc_ref.shape[0] // (rows_blk * SUBLANES)

    def x_in(k, slot):
        return pltpu.make_async_copy(
            x_hbm.at[b, pl.ds(pl.multiple_of(k * rows_blk, rows_blk), rows_blk), :], xbuf.at[slot], sem_in.at[slot])

    def x_out(k, slot):
        return pltpu.make_async_copy(
            obuf.at[slot], o_hbm.at[b, pl.ds(pl.multiple_of(k * rows_blk, rows_blk), rows_blk), :], sem_out.at[slot])

    @pl.when(e == n_exp - 1)
    def _final():
        x_in(0, 0).start()

        def block(k, _):
            slot = lax.rem(k, 2)

            @pl.when(k + 1 < n_blk)
            def _next():
                x_in(k + 1, 1 - slot).start()

            x_in(k, slot).wait()

            @pl.when(k >= 2)
            def _reuse():
                x_out(k - 2, slot).wait()

            moe = jnp.concatenate(
                [acc_ref[pl.ds(k * (rows_blk * SUBLANES) + j, rows_blk, stride=SUBLANES), :]
                 for j in range(d // LANES)], axis=1)
            obuf[slot] = xbuf[slot] + g_ref[...] * moe
            x_out(k, slot).start()
            return 0

        lax.fori_loop(0, n_blk, block, 0)
        for k in range(max(n_blk - 2, 0), n_blk):
            x_out(k, k % 2).wait()


def _combine(idx_flat, ys_tiles, nt, cap, final=None):
    bsz, n_exp = ys_tiles.shape[:2]
    group = max(1, min(n_exp, COMBINE_BLOCK_BYTES // (cap * SUBLANES * LANES * 4)))
    while n_exp % group:
        group -= 1
    in_specs = [pl.BlockSpec((None, group, cap * SUBLANES, LANES), lambda b, e, idx: (b, e, 0, 0))]
    scratch = [pltpu.VMEM((nt * SUBLANES, LANES), F32), pltpu.SemaphoreType.DMA(())]
    args = [idx_flat, ys_tiles]
    if final is None:
        out_shape = jax.ShapeDtypeStruct((bsz, nt * SUBLANES, LANES), F32)
    else:
        x, gate = final
        d = x.shape[-1]
        rows_blk = min(nt, FINAL_ROWS)
        in_specs += [pl.BlockSpec(memory_space=pl.ANY), pl.BlockSpec((None, 1, d), lambda b, e, idx: (b, 0, 0))]
        scratch += [pltpu.VMEM((2, rows_blk, d), F32), pltpu.VMEM((2, rows_blk, d), F32),
                    pltpu.SemaphoreType.DMA((2,)), pltpu.SemaphoreType.DMA((2,))]
        args += [x, gate]
        out_shape = jax.ShapeDtypeStruct(x.shape, F32)
    grid_spec = pltpu.PrefetchScalarGridSpec(
        num_scalar_prefetch=1,
        grid=(bsz, n_exp // group),
        in_specs=in_specs,
        out_specs=pl.BlockSpec(memory_space=pl.ANY),
        scratch_shapes=scratch,
    )
    return pl.pallas_call(
        functools.partial(_combine_kernel, cap=cap, unroll=4, final=final is not None),
        out_shape=out_shape,
        grid_spec=grid_spec,
        compiler_params=_cparams(("arbitrary", "arbitrary")),
        name="combine",
    )(*args)


def _prepare_layer(l, w_in, sgu_norm, w_sgu, b_sgu, q_lora_norm, w_uq, kv_lora_norm, w_ukv, q_norm, k_norm,
                   w_out, w_router, w_gate, w_up, w_down):
    w_in_l = w_in[l]
    kr_cols = w_in_l[:, OFF_KR:OFF_KR + QK_ROPE]
    w_in_ext = jnp.concatenate([w_in_l[:, :OFF_KR], kr_cols, kr_cols], axis=1).astype(BF16)
    uq = w_uq[l].reshape(Q_LORA, MLA_HEADS, QK_DIM)
    w_uq_p = jnp.concatenate(
        [uq[:, :, :QK_NOPE].reshape(Q_LORA, MLA_HEADS * QK_NOPE),
         uq[:, :, QK_NOPE:].reshape(Q_LORA, MLA_HEADS * QK_ROPE)], axis=1).astype(BF16)

    def norm_pad(v):
        return jnp.concatenate([v[:QK_NOPE], v[QK_NOPE:], v[QK_NOPE:]])[None, :]

    wr_t = w_router[l].T
    wr_hi = wr_t.astype(BF16)
    wr_lo = (wr_t - wr_hi.astype(F32)).astype(BF16)
    return dict(
        w_in=w_in_ext,
        sgu_norm=sgu_norm[l][None, :],
        w_sgu=jnp.concatenate([w_sgu[l][hh] for hh in range(SGU_HEADS)], axis=1).astype(BF16),
        b_sgu=jnp.repeat(b_sgu[l].T, SGU_HEAD_DIM, axis=1),
        fdft=jnp.asarray(_channel_dft_table()).astype(BF16),
        q_lora_norm=q_lora_norm[l][None, :],
        w_uq=w_uq_p,
        q_norm=norm_pad(q_norm[l]),
        kv_lora_norm=kv_lora_norm[l][None, :],
        w_ukv=w_ukv[l].astype(BF16),
        k_norm=norm_pad(k_norm[l]),
        w_out=w_out[l].astype(BF16),
        wr_hi=wr_hi, wr_lo=wr_lo,
        w_gate=w_gate, w_up=w_up, w_down=w_down, layer=l,
    )


def _rope_tables(n):
    pos = np.arange(n)
    half = QK_ROPE // 4
    freqs = ROPE_THETA ** (-np.arange(half, dtype=np.float64) / half)
    ang_r = (pos // GRID_WIDTH)[:, None] * freqs
    ang_c = (pos % GRID_WIDTH)[:, None] * freqs
    cos64 = np.concatenate([np.cos(ang_r), np.cos(ang_r), np.cos(ang_c), np.cos(ang_c)], axis=1)
    sin64 = np.concatenate([-np.sin(ang_r), np.sin(ang_r), -np.sin(ang_c), np.sin(ang_c)], axis=1)
    return (jnp.asarray(np.tile(cos64, (1, 2)), F32), jnp.asarray(np.tile(sin64, (1, 2)), F32))


def _token_tile(nt, pref):
    return pref if nt % pref == 0 else nt


def _moe(populations, lw, final=None):
    token_sets = []
    for h2_tiles, aff_t in populations:
        nt = aff_t.shape[-1]
        cap = EC_FACTOR * nt // N_EXPERTS
        idx, gate = _route(aff_t, cap)
        token_sets.append((idx.reshape(-1), h2_tiles, gate, cap))
    ys = _experts(token_sets, lw["w_gate"], lw["w_up"], lw["w_down"], lw["layer"])
    return [_combine(ts[0], y, pop[1].shape[-1], ts[3], final if i == 0 else None)
            for i, (ts, y, pop) in enumerate(zip(token_sets, ys, populations))]


def kernel(x, c, ctx, c_ctx, w_ada, b_ada, w_in, sgu_norm, w_sgu, b_sgu, q_lora_norm, w_uq, kv_lora_norm, w_ukv,
           q_norm, k_norm, w_out, w_router, w_gate, w_up, w_down):
    bsz, n, d = x.shape
    n_ctx = ctx.shape[1]
    depth = w_ada.shape[0]
    assert bsz + 1 <= SUBLANES

    cond = jnp.zeros((SUBLANES, d), F32).at[:bsz].set(c).at[bsz].set(c_ctx)
    mod = _ada_modulation(cond, w_ada, b_ada)
    rope_tabs = _rope_tables(n)
    tm = _token_tile(n, 1024)
    tm_in = _token_tile(n, 1024)
    tm_c = _token_tile(n_ctx, 256)

    xc = ctx
    pending = pending_c = None
    for l in range(depth):
        last = l == depth - 1
        lw = _prepare_layer(l, w_in, sgu_norm, w_sgu, b_sgu, q_lora_norm, w_uq, kv_lora_norm, w_ukv,
                            q_norm, k_norm, w_out, w_router, w_gate, w_up, w_down)
        parts = [mod[l, :, i * d:(i + 1) * d] for i in range(6)]
        lat = [p[:bsz, None, :] for p in parts]
        cx = [jnp.broadcast_to(p[bsz][None, None, :], (bsz, 1, d)) for p in parts]

        outs_c = _mix_in(xc, cx[0], cx[1], lw, None, tm_c, pending_c)
        outs = _mix_in(x, lat[0], lat[1], lw, rope_tabs, tm_in, pending)
        ya_c, a_c, b_c, q_c, k_c, v_c = outs_c[:6]
        ya, a, b, q, k, v = outs[:6]
        if pending is not None:
            xc, x = outs_c[6], outs[6]
        yb = _fourier_positions(a, b)
        tk = _token_tile(n, ATTN_TK)
        vt_c = jnp.swapaxes(v_c, 2, 3)
        vt = jnp.swapaxes(v.reshape(bsz, MLA_HEADS, n // tk, tk, V_DIM), 3, 4)
        yc = _attention(jnp.swapaxes(q, 2, 3), k_c, vt_c, k, vt, tq=_token_tile(n, ATTN_TQ), tk=tk)
        x_new, h2_tiles, aff_t = _mix_out(x, ya, yb, yc, lw["w_out"], lat[2], lat[3], lat[4],
                                          lw["wr_hi"], lw["wr_lo"], tm)
        if not last:
            yb_c = _fourier_positions(a_c, b_c)
            yc_c = _attention(jnp.swapaxes(q_c, 2, 3), k_c, vt_c, None, None,
                              tq=_token_tile(n_ctx, ATTN_TQ), tk=n_ctx)
            xc_new, h2c_tiles, affc_t = _mix_out(xc, ya_c, yb_c, yc_c, lw["w_out"], cx[2], cx[3], cx[4],
                                                 lw["wr_hi"], lw["wr_lo"], tm_c)
            moe, moe_c = _moe([(h2_tiles, aff_t), (h2c_tiles, affc_t)], lw)
            xc, pending_c = xc_new, (moe_c, cx[5])
            x, pending = x_new, (moe, lat[5])
        else:
            x, = _moe([(h2_tiles, aff_t)], lw, final=(x_new, lat[5]))
    return x
```

```python
import functools
import math

import numpy as np
import jax
import jax.numpy as jnp
from jax import lax
from jax.experimental import pallas as pl
from jax.experimental.pallas import tpu as pltpu

F32 = jnp.float32
BF16 = jnp.bfloat16

GRID_WIDTH = 64
SGU_HEADS = 4
SGU_HEAD_DIM = 64
SGU_W = SGU_HEADS * SGU_HEAD_DIM
CHUNK = 128
FNET_GROUP_DIM = 64
FNET_W = 256
MLA_HEADS = 4
QK_NOPE = 128
QK_ROPE = 64
QK_DIM = QK_NOPE + QK_ROPE
V_DIM = 128
Q_LORA = 256
KV_LORA = 128
OFF_V = 256
OFF_F = 512
OFF_Q = 768
OFF_KV = 1024
OFF_KR = 1152
N_EXPERTS = 16
EC_FACTOR = 2
ROPE_THETA = 10000.0
EPS = 1e-6

LANES = 128
SUBLANES = 8
VMEM_LIMIT_BYTES = 56 * 1024 * 1024

QK_PAD = 2 * LANES
ATTN_TQ = 4096
ATTN_TK = 512
LOG2_E = 1.4426950408889634
MIX_ROWS = 128
MIX_OUT_ROWS = 512


def _cparams(sem, vmem=None):
    return pltpu.CompilerParams(dimension_semantics=sem, vmem_limit_bytes=vmem or VMEM_LIMIT_BYTES)


def _rms_scale(v, width):
    return lax.rsqrt(jnp.sum(v * v, axis=-1, keepdims=True) * (1.0 / width) + EPS)


def _gelu_tanh(v):
    return 0.5 * v * (1.0 + jnp.tanh(0.7978845608028654 * (v + 0.044715 * v * v * v)))


def _silu(v):
    return v / (1.0 + jnp.exp(-v))


def _ada_kernel(cond_ref, w_ref, b_ref, o_ref):
    s = _silu(cond_ref[...])
    o_ref[...] = jnp.dot(s, w_ref[...], preferred_element_type=F32,
                         precision=lax.Precision.HIGHEST) + b_ref[...]


def _ada_modulation(cond, w_ada, b_ada):
    depth, d, six_d = w_ada.shape
    tn = 1536
    return pl.pallas_call(
        _ada_kernel,
        out_shape=jax.ShapeDtypeStruct((depth, SUBLANES, six_d), F32),
        grid=(depth, six_d // tn),
        in_specs=[
            pl.BlockSpec((SUBLANES, d), lambda l, j: (0, 0)),
            pl.BlockSpec((None, d, tn), lambda l, j: (l, 0, j)),
            pl.BlockSpec((None, 1, tn), lambda l, j: (l, 0, j)),
        ],
        out_specs=pl.BlockSpec((None, SUBLANES, tn), lambda l, j: (l, 0, j)),
        compiler_params=_cparams(("arbitrary", "arbitrary")),
        name="ada_modulation",
    )(cond, w_ada, b_ada.reshape(depth, 1, six_d))


def _swap16(v):
    lane = lax.broadcasted_iota(jnp.int32, v.shape, 1)
    up = pltpu.roll(v, LANES - 16, axis=1)
    dn = pltpu.roll(v, 16, axis=1)
    return jnp.where((lane % 32) < 16, up, dn)


def _mix_in_kernel(*refs, tm, use_rope, fuse_residual):
    rows_per_group = min(tm, MIX_ROWS)
    for g in range(tm // rows_per_group):
        _mix_in_rows(slice(g * rows_per_group, (g + 1) * rows_per_group), refs, use_rope, fuse_residual)


def _mix_in_rows(rg, refs, use_rope, fuse_residual):
    if fuse_residual:
        x_ref, moe_ref, g2_ref = refs[:3]
        xo_ref = refs[-1]
        refs = refs[:1] + refs[3:-1]
    (x_ref, sh_ref, sc_ref, win_ref, sgun_ref, wsgu_ref, bsgu_ref, fdft_ref, qln_ref, wuq_ref, qn_ref,
     kvn_ref, wukv_ref, kn_ref, cos_ref, sin_ref, ya_ref, a_ref, b_ref, q_ref, k_ref, v_ref) = refs
    x = x_ref[rg, :]
    if fuse_residual:
        nrows = rg.stop - rg.start
        moe = jnp.concatenate(
            [moe_ref[pl.ds(rg.start * SUBLANES + j, nrows, stride=SUBLANES), :]
             for j in range(x.shape[-1] // LANES)], axis=1)
        x = x + g2_ref[...] * moe
        xo_ref[rg, :] = x
    tm = x.shape[0]
    d = x.shape[-1]
    h = x * _rms_scale(x, d) * (1.0 + sc_ref[...]) + sh_ref[...]
    p = jnp.dot(h.astype(BF16), win_ref[...], preferred_element_type=F32)

    u = _gelu_tanh(p[:, 0:OFF_V])
    gv = _gelu_tanh(p[:, OFF_V:OFF_F])
    vn = gv * _rms_scale(gv, SGU_W) * sgun_ref[...]
    head_of_lane = lax.broadcasted_iota(jnp.int32, (CHUNK, SGU_W), 1) // SGU_HEAD_DIM
    for c in range(tm // CHUNK):
        rows = slice(c * CHUNK, (c + 1) * CHUNK)
        vc = vn[rows]
        vstack = jnp.concatenate(
            [jnp.where(head_of_lane == hh, vc, 0.0) for hh in range(SGU_HEADS)], axis=0).astype(BF16)
        z = jnp.dot(wsgu_ref[...], vstack, preferred_element_type=F32) + bsgu_ref[...]
        ya_ref[rg.start + c * CHUNK:rg.start + (c + 1) * CHUNK, :] = (u[rows] * z).astype(ya_ref.dtype)

    ab = jnp.dot(p[:, OFF_F:OFF_Q].astype(BF16), fdft_ref[...], preferred_element_type=F32)
    a_ref[rg, :] = ab[:, 0:FNET_W].astype(a_ref.dtype)
    b_ref[rg, :] = ab[:, FNET_W:2 * FNET_W].astype(b_ref.dtype)

    lane = lax.broadcasted_iota(jnp.int32, (1, LANES), 1)
    lo = lane < QK_ROPE
    if use_rope:
        cos = cos_ref[rg, :]
        sin = sin_ref[rg, :]

    def rope(t):
        return t * cos + _swap16(t) * sin if use_rope else t

    pq = p[:, OFF_Q:OFF_KV]
    cq = pq * _rms_scale(pq, Q_LORA) * qln_ref[...]
    q = jnp.dot(cq.astype(BF16), wuq_ref[...], preferred_element_type=F32)
    qnorm = qn_ref[...]
    for hh in range(MLA_HEADS):
        qn = q[:, LANES * hh:LANES * (hh + 1)]
        pair = hh // 2
        qr = q[:, MLA_HEADS * LANES + LANES * pair:MLA_HEADS * LANES + LANES * (pair + 1)]
        qr = jnp.where(lo if hh % 2 == 0 else jnp.logical_not(lo), qr, 0.0)
        ssq = jnp.sum(qn * qn, axis=-1, keepdims=True) + jnp.sum(qr * qr, axis=-1, keepdims=True)
        rinv = lax.rsqrt(ssq * (1.0 / QK_DIM) + EPS) * (QK_DIM ** -0.5 * LOG2_E)
        q_ref[hh, rg, 0:LANES] = (qn * rinv * qnorm[:, 0:LANES]).astype(q_ref.dtype)
        q_ref[hh, rg, LANES:QK_PAD] = rope(qr * rinv * qnorm[:, LANES:QK_PAD]).astype(q_ref.dtype)

    pkv = p[:, OFF_KV:OFF_KR]
    ckv = pkv * _rms_scale(pkv, KV_LORA) * kvn_ref[...]
    kv = jnp.dot(ckv.astype(BF16), wukv_ref[...], preferred_element_type=F32)
    krr = p[:, OFF_KR:OFF_KR + LANES]
    ssq_kr = 0.5 * jnp.sum(krr * krr, axis=-1, keepdims=True)
    knorm = kn_ref[...]
    rinvs = []
    for hh in range(MLA_HEADS):
        kn = kv[:, 2 * LANES * hh:2 * LANES * hh + LANES]
        ssq = jnp.sum(kn * kn, axis=-1, keepdims=True) + ssq_kr
        rinv = lax.rsqrt(ssq * (1.0 / QK_DIM) + EPS)
        rinvs.append(rinv)
        k_ref[hh, rg, 0:LANES] = (kn * rinv * knorm[:, 0:LANES]).astype(k_ref.dtype)
        v_ref[hh, rg, :] = kv[:, 2 * LANES * hh + LANES:2 * LANES * (hh + 1)].astype(v_ref.dtype)
    for pair in range(MLA_HEADS // 2):
        r2 = jnp.where(lo, rinvs[2 * pair], rinvs[2 * pair + 1])
        kr = rope(krr * r2 * knorm[:, LANES:QK_PAD]).astype(k_ref.dtype)
        k_ref[2 * pair, rg, LANES:QK_PAD] = kr
        k_ref[2 * pair + 1, rg, LANES:QK_PAD] = kr


def _mix_in(x, sh, sc, lw, rope_tabs, tm, pending=None):
    bsz, nt, d = x.shape
    use_rope = rope_tabs is not None
    if use_rope:
        cos_t, sin_t = rope_tabs
    else:
        cos_t = sin_t = jnp.zeros((nt, LANES), F32)
    full = lambda arr: pl.BlockSpec(arr.shape, lambda b, i: (0,) * arr.ndim)
    vec = pl.BlockSpec((None, 1, d), lambda b, i: (b, 0, 0))
    in_w = lw["w_in"].shape[1]
    fuse = pending is not None
    kern = functools.partial(_mix_in_kernel, tm=tm, use_rope=use_rope, fuse_residual=fuse)
    row_blk = pl.BlockSpec((None, tm, d), lambda b, i: (b, i, 0))
    out_shape = [
        jax.ShapeDtypeStruct((bsz, nt, SGU_W), BF16),
        jax.ShapeDtypeStruct((bsz, nt, FNET_W), F32),
        jax.ShapeDtypeStruct((bsz, nt, FNET_W), F32),
        jax.ShapeDtypeStruct((bsz, MLA_HEADS, nt, QK_PAD), BF16),
        jax.ShapeDtypeStruct((bsz, MLA_HEADS, nt, QK_PAD), BF16),
        jax.ShapeDtypeStruct((bsz, MLA_HEADS, nt, V_DIM), BF16),
    ]
    out_specs = [
        pl.BlockSpec((None, tm, SGU_W), lambda b, i: (b, i, 0)),
        pl.BlockSpec((None, tm, FNET_W), lambda b, i: (b, i, 0)),
        pl.BlockSpec((None, tm, FNET_W), lambda b, i: (b, i, 0)),
        pl.BlockSpec((None, MLA_HEADS, tm, QK_PAD), lambda b, i: (b, 0, i, 0)),
        pl.BlockSpec((None, MLA_HEADS, tm, QK_PAD), lambda b, i: (b, 0, i, 0)),
        pl.BlockSpec((None, MLA_HEADS, tm, V_DIM), lambda b, i: (b, 0, i, 0)),
    ]
    in_specs = [row_blk]
    args = [x]
    if fuse:
        in_specs += [pl.BlockSpec((None, tm * SUBLANES, LANES), lambda b, i: (b, i, 0)), vec]
        args += list(pending)
        out_shape.append(jax.ShapeDtypeStruct((bsz, nt, d), F32))
        out_specs.append(row_blk)
    in_specs += [
        vec, vec,
        full(lw["w_in"]), full(lw["sgu_norm"]), full(lw["w_sgu"]), full(lw["b_sgu"]), full(lw["fdft"]),
        full(lw["q_lora_norm"]), full(lw["w_uq"]), full(lw["q_norm"]),
        full(lw["kv_lora_norm"]), full(lw["w_ukv"]), full(lw["k_norm"]),
        pl.BlockSpec((tm, LANES), lambda b, i: (i, 0)),
        pl.BlockSpec((tm, LANES), lambda b, i: (i, 0)),
    ]
    args += [sh, sc, lw["w_in"], lw["sgu_norm"], lw["w_sgu"], lw["b_sgu"], lw["fdft"],
             lw["q_lora_norm"], lw["w_uq"], lw["q_norm"], lw["kv_lora_norm"], lw["w_ukv"], lw["k_norm"],
             cos_t, sin_t]
    return pl.pallas_call(
        kern,
        out_shape=tuple(out_shape),
        grid=(bsz, nt // tm),
        in_specs=in_specs,
        out_specs=tuple(out_specs),
        compiler_params=_cparams(("parallel", "parallel")),
        name="mix_in",
    )(*args)


DFT_UNROLL = 16


def _dft_kernel(a_ref, b_ref, w_ref, m_ref, o_ref, yr_ref, yi_ref, *, n1, n2):
    w1 = w_ref[...]
    c = a_ref.shape[-1]

    u1 = math.gcd(n2, DFT_UNROLL)
    u2 = math.gcd(n1, DFT_UNROLL)

    def step1(g, _):
        cols = []
        for u in range(u1):
            rows = pl.ds(g * u1 + u, n1, stride=n2)
            cols.append(jnp.concatenate([a_ref[rows, :], b_ref[rows, :]], axis=0))
        r = jnp.dot(w1, jnp.concatenate(cols, axis=1).astype(BF16), preferred_element_type=F32)
        for u in range(u1):
            out_rows = pl.ds(pl.multiple_of((g * u1 + u) * (n1 + SUBLANES), SUBLANES), n1)
            yr_ref[out_rows, :] = r[0:n1, u * c:(u + 1) * c]
            yi_ref[out_rows, :] = r[n1:2 * n1, u * c:(u + 1) * c]
        return 0

    lax.fori_loop(0, n2 // u1, step1, 0)

    def step2(g, _):
        for u in range(u2):
            k1 = g * u2 + u
            rows = pl.ds(k1, n2, stride=n1 + SUBLANES)
            ycat = jnp.concatenate([yr_ref[rows, :], yi_ref[rows, :]], axis=0).astype(BF16)
            o_ref[pl.ds(k1, n2, stride=n1), :] = jnp.dot(m_ref[k1], ycat, preferred_element_type=F32)
        return 0

    lax.fori_loop(0, n1 // u2, step2, 0)


def _dft_factors(n):
    n1 = 64 if n >= 1024 else 32
    return n1, n // n1


@functools.lru_cache(maxsize=None)
def _dft_tables(n):
    n1, n2 = _dft_factors(n)
    j = np.arange(n1, dtype=np.float64)
    ang1 = 2.0 * np.pi * np.outer(j, j) / n1
    c1 = np.cos(ang1) / math.sqrt(n1)
    s1 = np.sin(ang1) / math.sqrt(n1)
    w1 = np.block([[c1, -s1], [s1, c1]])
    k1 = np.arange(n1, dtype=np.float64)[:, None, None]
    k2 = np.arange(n2, dtype=np.float64)[None, :, None]
    m2 = np.arange(n2, dtype=np.float64)[None, None, :]
    ang2 = 2.0 * np.pi * m2 * (k1 + n1 * k2) / n
    m2cat = np.concatenate([np.cos(ang2), -np.sin(ang2)], axis=-1) / math.sqrt(n2)
    return w1.astype(np.float32), m2cat.astype(np.float32)


@functools.lru_cache(maxsize=None)
def _channel_dft_table():
    j = np.arange(FNET_GROUP_DIM, dtype=np.float64)
    ang = 2.0 * np.pi * np.outer(j, j) / FNET_GROUP_DIM
    groups = FNET_W // FNET_GROUP_DIM
    cg = np.kron(np.eye(groups), np.cos(ang)) / math.sqrt(FNET_GROUP_DIM)
    sg = np.kron(np.eye(groups), np.sin(ang)) / math.sqrt(FNET_GROUP_DIM)
    return np.concatenate([cg, sg], axis=1).astype(np.float32)


def _fourier_positions(a, b):
    bsz, n, c = a.shape
    n1, n2 = _dft_factors(n)
    w1_np, m2_np = _dft_tables(n)
    w1 = jnp.asarray(w1_np).astype(BF16)
    m2cat = jnp.asarray(m2_np).astype(BF16)
    cb = LANES
    blk = pl.BlockSpec((None, n, cb), lambda bi, j: (bi, 0, j))
    return pl.pallas_call(
        functools.partial(_dft_kernel, n1=n1, n2=n2),
        out_shape=jax.ShapeDtypeStruct((bsz, n, c), F32),
        grid=(bsz, c // cb),
        in_specs=[blk, blk,
                  pl.BlockSpec((2 * n1, 2 * n1), lambda bi, j: (0, 0)),
                  pl.BlockSpec((n1, n2, 2 * n2), lambda bi, j: (0, 0, 0))],
        out_specs=blk,
        scratch_shapes=[pltpu.VMEM((n2 * (n1 + SUBLANES), cb), F32)] * 2,
        compiler_params=_cparams(("parallel", "parallel")),
        name="fourier",
    )(a, b, w1, m2cat)


NEG_BIG = -1e30


def _attn_kernel(*refs, tk, n_lat_blocks):
    if n_lat_blocks:
        qt_ref, kc_ref, vtc_ref, kl_ref, vtl_ref, o_ref, acc_ref, sa_ref, sb_ref = refs
    else:
        qt_ref, kc_ref, vtc_ref, o_ref, acc_ref = refs
    qt = qt_ref[...]
    tq = qt.shape[1]

    def scores(kj):
        return jnp.dot(kj, qt, preferred_element_type=F32)

    def consume(s, vtj, m, l, first):
        m_new = jnp.maximum(m, jnp.max(s, axis=0, keepdims=True))
        alpha = jnp.exp2(m - m_new)
        p = jnp.exp2(s - m_new)
        l_new = alpha * l + jnp.sum(p, axis=0, keepdims=True)
        pv = jnp.dot(vtj, p.astype(BF16), preferred_element_type=F32)
        acc_ref[...] = pv if first else alpha * acc_ref[...] + pv
        return m_new, l_new

    def lat_keys(j):
        return kl_ref[pl.ds(pl.multiple_of(j * tk, tk), tk), :]

    m0 = jnp.full((1, tq), NEG_BIG, F32)
    l0 = jnp.zeros((1, tq), F32)
    if n_lat_blocks:
        sa_ref[...] = scores(lat_keys(0))
    m, l = consume(scores(kc_ref[...]), vtc_ref[...], m0, l0, True)
    if n_lat_blocks:
        def body(i, carry):
            sb_ref[...] = scores(lat_keys(2 * i + 1))
            m, l = consume(sa_ref[...], vtl_ref[2 * i], carry[0], carry[1], False)
            sa_ref[...] = scores(lat_keys(2 * i + 2))
            return consume(sb_ref[...], vtl_ref[2 * i + 1], m, l, False)

        m, l = lax.fori_loop(0, n_lat_blocks // 2 - 1, body, (m, l))
        sb_ref[...] = scores(lat_keys(n_lat_blocks - 1))
        m, l = consume(sa_ref[...], vtl_ref[n_lat_blocks - 2], m, l, False)
        m, l = consume(sb_ref[...], vtl_ref[n_lat_blocks - 1], m, l, False)
    o_ref[...] = jnp.transpose(acc_ref[...] / l).astype(o_ref.dtype)


def _attention(qt, k_ctx, vt_ctx, k_lat, vt_lat, tq, tk):
    bsz, heads, _, nq = qt.shape
    m_ctx = k_ctx.shape[2]
    n_lat_blocks = 0 if k_lat is None else k_lat.shape[2] // tk
    in_specs = [
        pl.BlockSpec((None, None, QK_PAD, tq), lambda b, h, i: (b, h, 0, i)),
        pl.BlockSpec((None, None, m_ctx, QK_PAD), lambda b, h, i: (b, h, 0, 0)),
        pl.BlockSpec((None, None, V_DIM, m_ctx), lambda b, h, i: (b, h, 0, 0)),
    ]
    args = [qt, k_ctx, vt_ctx]
    if n_lat_blocks:
        in_specs += [
            pl.BlockSpec((None, None, k_lat.shape[2], QK_PAD), lambda b, h, i: (b, h, 0, 0)),
            pl.BlockSpec((None, None, n_lat_blocks, V_DIM, tk), lambda b, h, i: (b, h, 0, 0, 0)),
        ]
        args += [k_lat, vt_lat]
    return pl.pallas_call(
        functools.partial(_attn_kernel, tk=tk, n_lat_blocks=n_lat_blocks),
        out_shape=jax.ShapeDtypeStruct((bsz, nq, heads * V_DIM), BF16),
        grid=(bsz, heads, nq // tq),
        in_specs=in_specs,
        out_specs=pl.BlockSpec((None, tq, V_DIM), lambda b, h, i: (b, i, h)),
        scratch_shapes=[pltpu.VMEM((V_DIM, tq), F32)] + [pltpu.VMEM((tk, tq), F32)] * (2 if n_lat_blocks else 0),
        compiler_params=_cparams(("parallel", "parallel", "arbitrary")),
        name="attention",
    )(*args)


def _mix_out_kernel(x_ref, ya_ref, yb_ref, yc_ref, wo_ref, g1_ref, sh_ref, sc_ref, wrh_ref, wrl_ref,
                    xn_ref, h2_ref, aff_ref, *, tm):
    rows_per_group = min(tm, MIX_OUT_ROWS)
    for g in range(tm // rows_per_group):
        _mix_out_rows(g * rows_per_group, rows_per_group, x_ref, ya_ref, yb_ref, yc_ref, wo_ref, g1_ref, sh_ref,
                      sc_ref, wrh_ref, wrl_ref, xn_ref, h2_ref, aff_ref)


def _mix_out_rows(r0, nr, x_ref, ya_ref, yb_ref, yc_ref, wo_ref, g1_ref, sh_ref, sc_ref, wrh_ref, wrl_ref,
                  xn_ref, h2_ref, aff_ref):
    rg = slice(r0, r0 + nr)
    mix = jnp.dot(ya_ref[rg, :], wo_ref[0:SGU_W, :], preferred_element_type=F32)
    mix += jnp.dot(yb_ref[rg, :].astype(BF16), wo_ref[SGU_W:SGU_W + FNET_W, :], preferred_element_type=F32)
    mix += jnp.dot(yc_ref[rg, :], wo_ref[SGU_W + FNET_W:, :], preferred_element_type=F32)
    xn = x_ref[rg, :] + g1_ref[...] * mix
    xn_ref[rg, :] = xn
    d = xn.shape[-1]
    h2 = xn * _rms_scale(xn, d) * (1.0 + sc_ref[...]) + sh_ref[...]
    for j in range(d // LANES):
        h2_ref[pl.ds(r0 * SUBLANES + j, nr, stride=SUBLANES), :] = h2[:, j * LANES:(j + 1) * LANES]
    h_hi = h2.astype(BF16)
    h_lo = (h2 - h_hi.astype(F32)).astype(BF16)
    nt_dims = (((1,), (1,)), ((), ()))
    logits = lax.dot_general(wrh_ref[...], h_hi, nt_dims, preferred_element_type=F32)
    logits += lax.dot_general(wrh_ref[...], h_lo, nt_dims, preferred_element_type=F32)
    logits += lax.dot_general(wrl_ref[...], h_hi, nt_dims, preferred_element_type=F32)
    e = jnp.exp(logits - jnp.max(logits, axis=0, keepdims=True))
    aff_ref[:, rg] = e / jnp.sum(e, axis=0, keepdims=True)


def _mix_out(x, ya, yb, yc, w_out, g1, sh2, sc2, wr_hi, wr_lo, tm):
    bsz, nt, d = x.shape
    vec = pl.BlockSpec((None, 1, d), lambda b, i: (b, 0, 0))
    full = lambda arr: pl.BlockSpec(arr.shape, lambda b, i: (0,) * arr.ndim)
    return pl.pallas_call(
        functools.partial(_mix_out_kernel, tm=tm),
        out_shape=(
            jax.ShapeDtypeStruct((bsz, nt, d), F32),
            jax.ShapeDtypeStruct((bsz, nt * SUBLANES, LANES), F32),
            jax.ShapeDtypeStruct((bsz, N_EXPERTS, nt), F32),
        ),
        grid=(bsz, nt // tm),
        in_specs=[
            pl.BlockSpec((None, tm, d), lambda b, i: (b, i, 0)),
            pl.BlockSpec((None, tm, SGU_W), lambda b, i: (b, i, 0)),
            pl.BlockSpec((None, tm, FNET_W), lambda b, i: (b, i, 0)),
            pl.BlockSpec((None, tm, MLA_HEADS * V_DIM), lambda b, i: (b, i, 0)),
            full(w_out), vec, vec, vec, full(wr_hi), full(wr_lo),
        ],
        out_specs=(
            pl.BlockSpec((None, tm, d), lambda b, i: (b, i, 0)),
            pl.BlockSpec((None, tm * SUBLANES, LANES), lambda b, i: (b, i, 0)),
            pl.BlockSpec((None, N_EXPERTS, tm), lambda b, i: (b, 0, i)),
        ),
        compiler_params=_cparams(("parallel", "parallel")),
        name="mix_out",
    )(x, ya, yb, yc, w_out, g1, sh2, sc2, wr_hi, wr_lo)


def _prefix_count(mask, tri):
    e, n = mask.shape
    w = tri.shape[0]
    carry = jnp.zeros((e, 1), F32)
    outs = []
    ends = []
    for c in range(n // w):
        local = jnp.dot(mask[:, c * w:(c + 1) * w].astype(BF16), tri, preferred_element_type=F32)
        outs.append(local + carry)
        carry = carry + local[:, w - 1:w]
        ends.append(carry)
    return jnp.concatenate(outs, axis=1), ends


ROUTE_CHUNK = 4 * LANES


def _route_select_kernel(aff_ref, key_ref, cend_ref, *, cap):
    n_exp, n = aff_ref.shape
    w = min(n, ROUTE_CHUNK)
    aff = aff_ref[...]

    def step(i, thr_bits):
        cand = thr_bits | jnp.left_shift(jnp.int32(1), 30 - i)
        cnt = jnp.sum((aff >= pltpu.bitcast(cand, F32)).astype(jnp.int32), axis=-1, keepdims=True)
        return jnp.where(cnt >= cap, cand, thr_bits)

    thr = pltpu.bitcast(lax.fori_loop(0, 31, step, jnp.zeros((n_exp, 1), jnp.int32)), F32)
    gt = aff > thr
    eq = aff == thr
    need = cap - jnp.sum(gt.astype(jnp.int32), axis=-1, keepdims=True)
    r = lax.broadcasted_iota(jnp.int32, (w, w), 0)
    c = lax.broadcasted_iota(jnp.int32, (w, w), 1)
    tri = (r <= c).astype(BF16)
    eq_rank, _ = _prefix_count(eq.astype(F32), tri)
    sel = jnp.logical_or(gt, jnp.logical_and(eq, eq_rank <= need.astype(F32)))
    cum, ends = _prefix_count(sel.astype(F32), tri)
    key_ref[...] = jnp.where(sel, cum, 0.0)
    lane = lax.broadcasted_iota(jnp.int32, (n_exp, LANES), 1)
    cend = jnp.zeros((n_exp, LANES), F32)
    for ci, end in enumerate(ends):
        cend = jnp.where(lane == ci, end, cend)
    cend_ref[...] = cend.astype(jnp.int32)


def _route_compact_kernel(cend_ref, key_ref, aff_ref, idx_ref, gate_ref, acc_i_ref, acc_g_ref, *, cap, jb):
    b = pl.program_id(0)
    e = pl.program_id(1)
    n_chunks, w = key_ref.shape
    n_sb = cap // jb
    base = (b * pl.num_programs(1) + e) * LANES
    shift = jb.bit_length() - 1
    acc_i_ref[...] = jnp.zeros_like(acc_i_ref)
    acc_g_ref[...] = jnp.zeros_like(acc_g_ref)

    def chunk(c, _):
        c_start = jnp.where(c > 0, cend_ref[base + jnp.maximum(c - 1, 0)], 0)
        c_end = cend_ref[base + c]

        @pl.when(c_end > c_start)
        def _any_selected():
            krow = key_ref[pl.ds(c, 1), :]
            arow = aff_ref[pl.ds(c, 1), :]
            tpos = (lax.broadcasted_iota(jnp.int32, (1, w), 1) + c * w).astype(F32)
            sb_lo = lax.shift_right_logical(c_start, shift)
            sb_hi = jnp.minimum(lax.shift_right_logical(c_end - 1, shift), n_sb - 1)

            def slot_block(sb, _):
                slot = (lax.broadcasted_iota(jnp.int32, (jb, 1), 0) + (sb * jb + 1)).astype(F32)
                hit = krow == slot
                pi = jnp.where(hit, tpos, 0.0)
                pg = jnp.where(hit, arow, 0.0)
                fi = pi[:, 0:LANES]
                fg = pg[:, 0:LANES]
                for k in range(1, w // LANES):
                    fi = fi + pi[:, k * LANES:(k + 1) * LANES]
                    fg = fg + pg[:, k * LANES:(k + 1) * LANES]
                acc_i_ref[sb] = acc_i_ref[sb] + fi
                acc_g_ref[sb] = acc_g_ref[sb] + fg
                return 0

            lax.fori_loop(sb_lo, sb_hi + 1, slot_block, 0)

        return 0

    lax.fori_loop(0, n_chunks, chunk, 0)
    for sb in range(n_sb):
        idx_ref[sb * jb:(sb + 1) * jb, :] = jnp.sum(acc_i_ref[sb], axis=-1, keepdims=True).astype(jnp.int32)
        gate_ref[sb * jb:(sb + 1) * jb, :] = jnp.sum(acc_g_ref[sb], axis=-1, keepdims=True)


def _route(aff_t, cap):
    bsz, n_exp, n = aff_t.shape
    jb = min(cap, LANES)
    w = min(n, ROUTE_CHUNK)
    key, cend = pl.pallas_call(
        functools.partial(_route_select_kernel, cap=cap),
        out_shape=(jax.ShapeDtypeStruct((bsz, n_exp, n), F32),
                   jax.ShapeDtypeStruct((bsz, n_exp, LANES), jnp.int32)),
        grid=(bsz,),
        in_specs=[pl.BlockSpec((None, n_exp, n), lambda b: (b, 0, 0))],
        out_specs=(pl.BlockSpec((None, n_exp, n), lambda b: (b, 0, 0)),
                   pl.BlockSpec((None, n_exp, LANES), lambda b: (b, 0, 0))),
        compiler_params=_cparams(("parallel",)),
        name="route_select",
    )(aff_t)
    chunked = pl.BlockSpec((None, None, n // w, w), lambda b, e, cend: (b, e, 0, 0))
    out_blk = pl.BlockSpec((None, None, cap, 1), lambda b, e, cend: (b, e, 0, 0))
    grid_spec = pltpu.PrefetchScalarGridSpec(
        num_scalar_prefetch=1,
        grid=(bsz, n_exp),
        in_specs=[chunked, chunked],
        out_specs=(out_blk, out_blk),
        scratch_shapes=[pltpu.VMEM((cap // jb, jb, LANES), F32)] * 2,
    )
    return pl.pallas_call(
        functools.partial(_route_compact_kernel, cap=cap, jb=jb),
        out_shape=(jax.ShapeDtypeStruct((bsz, n_exp, cap, 1), jnp.int32),
                   jax.ShapeDtypeStruct((bsz, n_exp, cap, 1), F32)),
        grid_spec=grid_spec,
        compiler_params=_cparams(("parallel", "parallel")),
        name="route_compact",
    )(cend.reshape(-1), key.reshape(bsz, n_exp, n // w, w), aff_t.reshape(bsz, n_exp, n // w, w))


GATHER_UNROLL = 8
BF16_ROWS = 16
EXPERT_ROW_CHUNKS = 4
GATHER_PARTS = 4 * EXPERT_ROW_CHUNKS


def _tile_copy(h_hbm, xs_ref, sem, b, tok, row):
    return pltpu.make_async_copy(
        h_hbm.at[b, pl.ds(pl.multiple_of(tok * SUBLANES, SUBLANES), SUBLANES), :],
        xs_ref.at[pl.ds(pl.multiple_of(row * SUBLANES, SUBLANES), SUBLANES), :],
        sem)


def _gather_start(idx_refs, h_refs, caps, xs_ref, sem, e, b, n_exp):
    row0 = 0
    for idx_ref, h_hbm, cap in zip(idx_refs, h_refs, caps):
        base = (b * n_exp + e) * cap
        unroll = math.gcd(cap, GATHER_UNROLL)

        def issue(g, _, idx_ref=idx_ref, h_hbm=h_hbm, base=base, unroll=unroll, row0=row0):
            for u in range(unroll):
                i = g * unroll + u
                _tile_copy(h_hbm, xs_ref, sem, b, idx_ref[base + i], row0 + i).start(priority=u % 2)
            return 0

        lax.fori_loop(0, cap // unroll, issue, 0)
        row0 += cap


def _gather_inline(idx_refs, h_refs, caps, xs_ref, sem, e, b, n_exp, part):
    row0 = 0
    for idx_ref, h_hbm, cap in zip(idx_refs, h_refs, caps):
        base = (b * n_exp + e) * cap
        for k, i in enumerate(range(part, cap, GATHER_PARTS)):
            _tile_copy(h_hbm, xs_ref, sem, b, idx_ref[base + i], row0 + i).start(priority=k % 2)
        row0 += cap


def _expert_kernel(*refs, caps, d):
    n_sets = len(caps)
    idx_refs = refs[:n_sets]
    h_refs = refs[n_sets:2 * n_sets]
    wg_ref, wu_ref, wd_ref = refs[2 * n_sets:2 * n_sets + 3]
    gate_refs = refs[2 * n_sets + 3:3 * n_sets + 3]
    ys_refs = refs[3 * n_sets + 3:4 * n_sets + 3]
    xs_a, xs_b, sem = refs[4 * n_sets + 3:]
    rows = sum(caps)
    e = pl.program_id(0)
    b = pl.program_id(1)
    n_exp = pl.num_programs(0)
    n_b = pl.num_programs(1)
    total = n_exp * n_b
    step = e * n_b + b
    slot = lax.rem(step, 2)

    def wait_all(buf, s):
        pltpu.make_async_copy(h_refs[0].at[b, pl.ds(0, rows * SUBLANES), :], buf, sem.at[s]).wait()

    @pl.when(step == 0)
    def _first():
        _gather_start(idx_refs, h_refs, caps, xs_a, sem.at[0], e, b, n_exp)

    nxt = jnp.where(step + 1 < total, step + 1, 0)
    e1 = lax.div(nxt, n_b)
    b1 = lax.rem(nxt, n_b)

    def body(cur, nxt_buf, s_cur, s_nxt):
        def start_part(part):
            _gather_inline(idx_refs, h_refs, caps, nxt_buf, sem.at[s_nxt], e1, b1, n_exp, part)

        wait_all(cur, s_cur)
        wg = wg_ref[...].astype(BF16)
        wu = wu_ref[...].astype(BF16)
        wd = wd_ref[...].astype(BF16)
        bounds = [0]
        for cap in caps:
            bounds.append(bounds[-1] + cap)
        n_chunks = min(EXPERT_ROW_CHUNKS, rows // BF16_ROWS)
        groups = rows // BF16_ROWS
        edges = [BF16_ROWS * (groups * c // n_chunks) for c in range(n_chunks)] + [rows]
        for c in range(n_chunks):
            r0, r1 = edges[c], edges[c + 1]
            chunk = r1 - r0
            x = jnp.concatenate(
                [cur[pl.ds(r0 * SUBLANES + j, chunk, stride=SUBLANES), :] for j in range(d // LANES)],
                axis=1).astype(BF16)
            start_part(4 * c)
            hid = _silu(jnp.dot(x, wg, preferred_element_type=F32))
            start_part(4 * c + 1)
            hid = hid * jnp.dot(x, wu, preferred_element_type=F32)
            start_part(4 * c + 2)
            y = jnp.dot(hid.astype(BF16), wd, preferred_element_type=F32)
            start_part(4 * c + 3)
            for s in range(n_sets):
                lo, hi = max(r0, bounds[s]), min(r1, bounds[s + 1])
                if lo < hi:
                    ys = y[lo - r0:hi - r0] * gate_refs[s][lo - bounds[s]:hi - bounds[s], :]
                    for j in range(d // LANES):
                        ys_refs[s][pl.ds((lo - bounds[s]) * SUBLANES + j, hi - lo, stride=SUBLANES), :] = (
                            ys[:, j * LANES:(j + 1) * LANES])
        for part in range(4 * n_chunks, GATHER_PARTS):
            start_part(part)

        @pl.when(step == total - 1)
        def _drain():
            wait_all(nxt_buf, s_nxt)

    @pl.when(slot == 0)
    def _even():
        body(xs_a, xs_b, 0, 1)

    @pl.when(slot == 1)
    def _odd():
        body(xs_b, xs_a, 1, 0)


def _experts(token_sets, wg, wu, wd, layer):
    caps = tuple(ts[3] for ts in token_sets)
    n_sets = len(token_sets)
    bsz = token_sets[0][1].shape[0]
    _, n_exp, d, f = wg.shape
    assert token_sets[0][1].shape[1] >= sum(caps) * SUBLANES
    wspec = lambda rows, cols: pl.BlockSpec((None, None, rows, cols), lambda e, b, *idx: (layer, e, 0, 0))
    grid_spec = pltpu.PrefetchScalarGridSpec(
        num_scalar_prefetch=n_sets,
        grid=(n_exp, bsz),
        in_specs=[pl.BlockSpec(memory_space=pl.ANY)] * n_sets + [wspec(d, f), wspec(d, f), wspec(f, d)] + [
            pl.BlockSpec((None, None, cap, 1), lambda e, b, *idx: (b, e, 0, 0)) for cap in caps],
        out_specs=tuple(pl.BlockSpec((None, None, cap * SUBLANES, LANES), lambda e, b, *idx: (b, e, 0, 0))
                        for cap in caps),
        scratch_shapes=[pltpu.VMEM((sum(caps) * SUBLANES, LANES), F32)] * 2 + [pltpu.SemaphoreType.DMA((2,))],
    )
    return pl.pallas_call(
        functools.partial(_expert_kernel, caps=caps, d=d),
        out_shape=tuple(jax.ShapeDtypeStruct((bsz, n_exp, cap * SUBLANES, LANES), F32) for cap in caps),
        grid_spec=grid_spec,
        compiler_params=_cparams(("arbitrary", "arbitrary")),
        name="experts",
    )(*[ts[0] for ts in token_sets], *[ts[1] for ts in token_sets], wg, wu, wd, *[ts[2] for ts in token_sets])


FINAL_ROWS = 256
COMBINE_BLOCK_BYTES = 4 * 1024 * 1024


def _combine_kernel(idx_ref, ys_ref, *rest, cap, unroll, final):
    if final:
        x_hbm, g_ref, o_hbm, acc_ref, sem, xbuf, obuf, sem_in, sem_out = rest
    else:
        o_hbm, acc_ref, sem = rest
    b = pl.program_id(0)
    e = pl.program_id(1)
    n_exp = pl.num_programs(1)
    group = ys_ref.shape[0]

    @pl.when(e == 0)
    def _zero():
        acc_ref[...] = jnp.zeros_like(acc_ref)

    def expert(el, _):
        base = ((b * n_exp + e) * group + el) * cap

        def rows(g, _):
            i0 = g * unroll
            toks = [pl.multiple_of(idx_ref[base + i0 + u] * SUBLANES, SUBLANES) for u in range(unroll)]
            sums = [acc_ref[pl.ds(toks[u], SUBLANES), :]
                    + ys_ref[el, pl.ds(pl.multiple_of((i0 + u) * SUBLANES, SUBLANES), SUBLANES), :]
                    for u in range(unroll)]
            for u in range(unroll):
                acc_ref[pl.ds(toks[u], SUBLANES), :] = sums[u]
            return 0

        lax.fori_loop(0, cap // unroll, rows, 0)
        return 0

    lax.fori_loop(0, group, expert, 0)

    if not final:
        @pl.when(e == n_exp - 1)
        def _flush():
            cp = pltpu.make_async_copy(acc_ref, o_hbm.at[b], sem)
            cp.start()
            cp.wait()
        return

    rows_blk, d = xbuf.shape[1:]
    n_blk = acc_ref.shape[0] // (rows_blk * SUBLANES)

    def x_in(k, slot):
        return pltpu.make_async_copy(
            x_hbm.at[b, pl.ds(pl.multiple_of(k * rows_blk, rows_blk), rows_blk), :], xbuf.at[slot], sem_in.at[slot])

    def x_out(k, slot):
        return pltpu.make_async_copy(
            obuf.at[slot], o_hbm.at[b, pl.ds(pl.multiple_of(k * rows_blk, rows_blk), rows_blk), :], sem_out.at[slot])

    @pl.when(e == n_exp - 1)
    def _final():
        x_in(0, 0).start()

        def block(k, _):
            slot = lax.rem(k, 2)

            @pl.when(k + 1 < n_blk)
            def _next():
                x_in(k + 1, 1 - slot).start()

            x_in(k, slot).wait()

            @pl.when(k >= 2)
            def _reuse():
                x_out(k - 2, slot).wait()

            moe = jnp.concatenate(
                [acc_ref[pl.ds(k * (rows_blk * SUBLANES) + j, rows_blk, stride=SUBLANES), :]
                 for j in range(d // LANES)], axis=1)
            obuf[slot] = xbuf[slot] + g_ref[...] * moe
            x_out(k, slot).start()
            return 0

        lax.fori_loop(0, n_blk, block, 0)
        for k in range(max(n_blk - 2, 0), n_blk):
            x_out(k, k % 2).wait()


def _combine(idx_flat, ys_tiles, nt, cap, final=None):
    bsz, n_exp = ys_tiles.shape[:2]
    group = max(1, min(n_exp, COMBINE_BLOCK_BYTES // (cap * SUBLANES * LANES * 4)))
    while n_exp % group:
        group -= 1
    in_specs = [pl.BlockSpec((None, group, cap * SUBLANES, LANES), lambda b, e, idx: (b, e, 0, 0))]
    scratch = [pltpu.VMEM((nt * SUBLANES, LANES), F32), pltpu.SemaphoreType.DMA(())]
    args = [idx_flat, ys_tiles]
    if final is None:
        out_shape = jax.ShapeDtypeStruct((bsz, nt * SUBLANES, LANES), F32)
    else:
        x, gate = final
        d = x.shape[-1]
        rows_blk = min(nt, FINAL_ROWS)
        in_specs += [pl.BlockSpec(memory_space=pl.ANY), pl.BlockSpec((None, 1, d), lambda b, e, idx: (b, 0, 0))]
        scratch += [pltpu.VMEM((2, rows_blk, d), F32), pltpu.VMEM((2, rows_blk, d), F32),
                    pltpu.SemaphoreType.DMA((2,)), pltpu.SemaphoreType.DMA((2,))]
        args += [x, gate]
        out_shape = jax.ShapeDtypeStruct(x.shape, F32)
    grid_spec = pltpu.PrefetchScalarGridSpec(
        num_scalar_prefetch=1,
        grid=(bsz, n_exp // group),
        in_specs=in_specs,
        out_specs=pl.BlockSpec(memory_space=pl.ANY),
        scratch_shapes=scratch,
    )
    return pl.pallas_call(
        functools.partial(_combine_kernel, cap=cap, unroll=4, final=final is not None),
        out_shape=out_shape,
        grid_spec=grid_spec,
        compiler_params=_cparams(("arbitrary", "arbitrary")),
        name="combine",
    )(*args)


def _prepare_layer(l, w_in, sgu_norm, w_sgu, b_sgu, q_lora_norm, w_uq, kv_lora_norm, w_ukv, q_norm, k_norm,
                   w_out, w_router, w_gate, w_up, w_down):
    w_in_l = w_in[l]
    kr_cols = w_in_l[:, OFF_KR:OFF_KR + QK_ROPE]
    w_in_ext = jnp.concatenate([w_in_l[:, :OFF_KR], kr_cols, kr_cols], axis=1).astype(BF16)
    uq = w_uq[l].reshape(Q_LORA, MLA_HEADS, QK_DIM)
    w_uq_p = jnp.concatenate(
        [uq[:, :, :QK_NOPE].reshape(Q_LORA, MLA_HEADS * QK_NOPE),
         uq[:, :, QK_NOPE:].reshape(Q_LORA, MLA_HEADS * QK_ROPE)], axis=1).astype(BF16)

    def norm_pad(v):
        return jnp.concatenate([v[:QK_NOPE], v[QK_NOPE:], v[QK_NOPE:]])[None, :]

    wr_t = w_router[l].T
    wr_hi = wr_t.astype(BF16)
    wr_lo = (wr_t - wr_hi.astype(F32)).astype(BF16)
    return dict(
        w_in=w_in_ext,
        sgu_norm=sgu_norm[l][None, :],
        w_sgu=jnp.concatenate([w_sgu[l][hh] for hh in range(SGU_HEADS)], axis=1).astype(BF16),
        b_sgu=jnp.repeat(b_sgu[l].T, SGU_HEAD_DIM, axis=1),
        fdft=jnp.asarray(_channel_dft_table()).astype(BF16),
        q_lora_norm=q_lora_norm[l][None, :],
        w_uq=w_uq_p,
        q_norm=norm_pad(q_norm[l]),
        kv_lora_norm=kv_lora_norm[l][None, :],
        w_ukv=w_ukv[l].astype(BF16),
        k_norm=norm_pad(k_norm[l]),
        w_out=w_out[l].astype(BF16),
        wr_hi=wr_hi, wr_lo=wr_lo,
        w_gate=w_gate, w_up=w_up, w_down=w_down, layer=l,
    )


def _rope_tables(n):
    pos = np.arange(n)
    half = QK_ROPE // 4
    freqs = ROPE_THETA ** (-np.arange(half, dtype=np.float64) / half)
    ang_r = (pos // GRID_WIDTH)[:, None] * freqs
    ang_c = (pos % GRID_WIDTH)[:, None] * freqs
    cos64 = np.concatenate([np.cos(ang_r), np.cos(ang_r), np.cos(ang_c), np.cos(ang_c)], axis=1)
    sin64 = np.concatenate([-np.sin(ang_r), np.sin(ang_r), -np.sin(ang_c), np.sin(ang_c)], axis=1)
    return (jnp.asarray(np.tile(cos64, (1, 2)), F32), jnp.asarray(np.tile(sin64, (1, 2)), F32))


def _token_tile(nt, pref):
    return pref if nt % pref == 0 else nt


def _moe(populations, lw, final=None):
    token_sets = []
    for h2_tiles, aff_t in populations:
        nt = aff_t.shape[-1]
        cap = EC_FACTOR * nt // N_EXPERTS
        idx, gate = _route(aff_t, cap)
        token_sets.append((idx.reshape(-1), h2_tiles, gate, cap))
    ys = _experts(token_sets, lw["w_gate"], lw["w_up"], lw["w_down"], lw["layer"])
    return [_combine(ts[0], y, pop[1].shape[-1], ts[3], final if i == 0 else None)
            for i, (ts, y, pop) in enumerate(zip(token_sets, ys, populations))]


def kernel(x, c, ctx, c_ctx, w_ada, b_ada, w_in, sgu_norm, w_sgu, b_sgu, q_lora_norm, w_uq, kv_lora_norm, w_ukv,
           q_norm, k_norm, w_out, w_router, w_gate, w_up, w_down):
    bsz, n, d = x.shape
    n_ctx = ctx.shape[1]
    depth = w_ada.shape[0]
    assert bsz + 1 <= SUBLANES

    cond = jnp.zeros((SUBLANES, d), F32).at[:bsz].set(c).at[bsz].set(c_ctx)
    mod = _ada_modulation(cond, w_ada, b_ada)
    rope_tabs = _rope_tables(n)
    tm = _token_tile(n, 1024)
    tm_in = _token_tile(n, 1024)
    tm_c = _token_tile(n_ctx, 256)

    xc = ctx
    pending = pending_c = None
    for l in range(depth):
        last = l == depth - 1
        lw = _prepare_layer(l, w_in, sgu_norm, w_sgu, b_sgu, q_lora_norm, w_uq, kv_lora_norm, w_ukv,
                            q_norm, k_norm, w_out, w_router, w_gate, w_up, w_down)
        parts = [mod[l, :, i * d:(i + 1) * d] for i in range(6)]
        lat = [p[:bsz, None, :] for p in parts]
        cx = [jnp.broadcast_to(p[bsz][None, None, :], (bsz, 1, d)) for p in parts]

        outs_c = _mix_in(xc, cx[0], cx[1], lw, None, tm_c, pending_c)
        outs = _mix_in(x, lat[0], lat[1], lw, rope_tabs, tm_in, pending)
        ya_c, a_c, b_c, q_c, k_c, v_c = outs_c[:6]
        ya, a, b, q, k, v = outs[:6]
        if pending is not None:
            xc, x = outs_c[6], outs[6]
        yb = _fourier_positions(a, b)
        tk = _token_tile(n, ATTN_TK)
        vt_c = jnp.swapaxes(v_c, 2, 3)
        vt = jnp.swapaxes(v.reshape(bsz, MLA_HEADS, n // tk, tk, V_DIM), 3, 4)
        yc = _attention(jnp.swapaxes(q, 2, 3), k_c, vt_c, k, vt, tq=_token_tile(n, ATTN_TQ), tk=tk)
        x_new, h2_tiles, aff_t = _mix_out(x, ya, yb, yc, lw["w_out"], lat[2], lat[3], lat[4],
                                          lw["wr_hi"], lw["wr_lo"], tm)
        if not last:
            yb_c = _fourier_positions(a_c, b_c)
            yc_c = _attention(jnp.swapaxes(q_c, 2, 3), k_c, vt_c, None, None,
                              tq=_token_tile(n_ctx, ATTN_TQ), tk=n_ctx)
            xc_new, h2c_tiles, affc_t = _mix_out(xc, ya_c, yb_c, yc_c, lw["w_out"], cx[2], cx[3], cx[4],
                                                 lw["wr_hi"], lw["wr_lo"], tm_c)
            moe, moe_c = _moe([(h2_tiles, aff_t), (h2c_tiles, affc_t)], lw)
            xc, pending_c = xc_new, (moe_c, cx[5])
            x, pending = x_new, (moe, lat[5])
        else:
            x, = _moe([(h2_tiles, aff_t)], lw, final=(x_new, lat[5]))
    return x
```
